```python
import math
import jax, jax.numpy as jnp
from jax import lax
import numpy as np

D_MODEL = 1024
BATCH = 4
SEQ = 4096
DEPTH = 4

CHUNK = 64
S5_WIDTH = 512
S5_GROUP = 16
S5_GROUPS = S5_WIDTH // S5_GROUP
S5_STATE = 64
DT_MIN = 1e-3
DT_MAX = 1e-1
MLA_HEADS = 8
QK_NOPE = 64
QK_ROPE = 32
V_HEAD = 64
Q_LORA = 384
KV_LORA = 256
ROPE_THETA = 10000.0
Q_BLOCK = 128
SGU_WIDTH = 512
SGU_GROUPS = 4
SGU_CHUNK = 128
N_BRANCH = 3
BRANCH_WIDTH = 512
FF_HIDDEN = -(-8 * D_MODEL // (3 * 256)) * 256
DEEPNORM_ALPHA = (2 * DEPTH) ** 0.25
DEEPNORM_BETA = (8 * DEPTH) ** -0.25
LN_EPS = 1e-5
RMS_EPS = 1e-6
NEG_INF = -1e30
IN_WIDTHS = (S5_WIDTH, Q_LORA, KV_LORA, QK_ROPE, SGU_WIDTH, SGU_WIDTH, N_BRANCH * D_MODEL)
IN_OFFSETS = tuple(int(o) for o in np.cumsum(IN_WIDTHS)[:-1])
IN_WIDTH = sum(IN_WIDTHS)

kernel_name = 'hybrid_s5_mla_sgu_deepnorm_adaln'


def layer_norm(x, g, b):
    xf = x.astype(jnp.float32)
    mu = jnp.mean(xf, axis=-1, keepdims=True)
    var = jnp.mean(jnp.square(xf - mu), axis=-1, keepdims=True)
    return ((xf - mu) * lax.rsqrt(var + LN_EPS)).astype(x.dtype) * g + b


def rms_norm(x, g):
    xf = x.astype(jnp.float32)
    return (xf * lax.rsqrt(jnp.mean(xf * xf, axis=-1, keepdims=True) + RMS_EPS)).astype(x.dtype) * g


def rope(x, cos, sin):
    x1, x2 = jnp.split(x, 2, axis=-1)
    return jnp.concatenate([x1 * cos - x2 * sin, x2 * cos + x1 * sin], axis=-1)


def _complex_affine_combine(left, right):
    ar1, ai1, br1, bi1 = left
    ar2, ai2, br2, bi2 = right
    return (ar2 * ar1 - ai2 * ai1,
            ar2 * ai1 + ai2 * ar1,
            ar2 * br1 - ai2 * bi1 + br2,
            ar2 * bi1 + ai2 * br1 + bi2)


def s5_mixer(u, lam_re, lam_im, log_dt, b_re, b_im, c_re, c_im, d, w_glu, b_glu):
    f32 = jnp.float32
    bsz, seq, _ = u.shape
    uf = u.astype(f32)
    ug = uf.reshape(bsz, seq, S5_GROUPS, S5_GROUP)
    dt = jnp.exp(log_dt.astype(f32))[:, None]
    lr = lam_re.astype(f32)
    li = lam_im.astype(f32)
    mag = jnp.exp(lr * dt)
    a_re = mag * jnp.cos(li * dt)
    a_im = mag * jnp.sin(li * dt)
    den = lr * lr + li * li
    f_re = ((a_re - 1.0) * lr + a_im * li) / den
    f_im = (a_im * lr - (a_re - 1.0) * li) / den
    br = b_re.astype(f32)
    bi = b_im.astype(f32)
    bb_re = f_re[..., None] * br - f_im[..., None] * bi
    bb_im = f_re[..., None] * bi + f_im[..., None] * br
    bu_re = jnp.einsum('bsgc,gpc->bsgp', ug, bb_re)
    bu_im = jnp.einsum('bsgc,gpc->bsgp', ug, bb_im)
    a_re_t = jnp.broadcast_to(a_re, (1, seq) + a_re.shape)
    a_im_t = jnp.broadcast_to(a_im, (1, seq) + a_im.shape)
    _, _, h_re, h_im = lax.associative_scan(
        _complex_affine_combine, (a_re_t, a_im_t, bu_re, bu_im), axis=1)
    y = (jnp.einsum('bsgp,gcp->bsgc', h_re, c_re.astype(f32))
         - jnp.einsum('bsgp,gcp->bsgc', h_im, c_im.astype(f32)))
    y = y.reshape(bsz, seq, S5_WIDTH) + d.astype(f32) * uf
    z = jax.nn.gelu(y)
    out = z * jax.nn.sigmoid(z @ w_glu.astype(f32) + b_glu.astype(f32))
    return out.astype(u.dtype)


def mla_mixer(cq, ckv, k_pe, q_norm, w_q_up, kv_norm, w_kv_up, cos, sin):
    bsz, seq, _ = cq.shape
    q = (rms_norm(cq, q_norm) @ w_q_up).reshape(bsz, seq, MLA_HEADS, QK_NOPE + QK_ROPE)
    q_nope = q[..., :QK_NOPE]
    q_pe = rope(q[..., QK_NOPE:], cos[:, None, :], sin[:, None, :])
    kv = (rms_norm(ckv, kv_norm) @ w_kv_up).reshape(bsz, seq, MLA_HEADS, QK_NOPE + V_HEAD)
    k_nope = kv[..., :QK_NOPE]
    v = kv[..., QK_NOPE:]
    k_pe = rope(k_pe, cos, sin)
    n_blk = seq // Q_BLOCK
    scale = (QK_NOPE + QK_ROPE) ** -0.5
    q_nope_b = q_nope.reshape(bsz, n_blk, Q_BLOCK, MLA_HEADS, QK_NOPE).transpose(1, 0, 2, 3, 4)
    q_pe_b = q_pe.reshape(bsz, n_blk, Q_BLOCK, MLA_HEADS, QK_ROPE).transpose(1, 0, 2, 3, 4)
    key_chunk = jnp.arange(seq) // CHUNK

    def attend_block(args):
        blk, qn, qp = args
        s = (jnp.einsum('bqhd,bkhd->bhqk', qn, k_nope)
             + jnp.einsum('bqhr,bkr->bhqk', qp, k_pe))
        s = s.astype(jnp.float32) * scale
        q_chunk = (blk * Q_BLOCK + jnp.arange(Q_BLOCK)) // CHUNK
        mask = key_chunk[None, :] <= q_chunk[:, None]
        s = jnp.where(mask, s, NEG_INF)
        p = jax.nn.softmax(s, axis=-1).astype(v.dtype)
        return jnp.einsum('bhqk,bkhd->bqhd', p, v)

    o = lax.map(attend_block, (jnp.arange(n_blk), q_nope_b, q_pe_b))
    return o.transpose(1, 0, 2, 3, 4).reshape(bsz, seq, MLA_HEADS * V_HEAD)


def sgu_mixer(u, v, ln_g, ln_b, w_s, b_s):
    bsz, seq, _ = u.shape
    u = jax.nn.gelu(u)
    v = layer_norm(jax.nn.gelu(v), ln_g, ln_b)
    n_chunk = seq // SGU_CHUNK
    vg = v.reshape(bsz, n_chunk, SGU_CHUNK, SGU_GROUPS, SGU_WIDTH // SGU_GROUPS)
    pos_chunk = jnp.arange(SGU_CHUNK) // CHUNK
    mask = pos_chunk[None, :] <= pos_chunk[:, None]
    w = jnp.where(mask[None], w_s, 0.0)
    mixed = jnp.einsum('gij,bnjgc->bnigc', w, vg) + b_s.T[:, :, None]
    return u * mixed.reshape(bsz, seq, SGU_WIDTH)


def hybrid_mixer(h, w_in, b_in, lam_re, lam_im, log_dt, b_re, b_im, c_re, c_im, d, w_glu, b_glu,
                 q_norm, w_q_up, kv_norm, w_kv_up, sgu_g, sgu_b, w_s, b_s, w_branch, w_out, cos, sin):
    bsz, seq, _ = h.shape
    proj = h @ w_in + b_in
    u_s5, cq, ckv, k_pe, u_sgu, v_sgu, gate_logits = jnp.split(proj, IN_OFFSETS, axis=-1)
    y_s5 = s5_mixer(u_s5, lam_re, lam_im, log_dt, b_re, b_im, c_re, c_im, d, w_glu, b_glu)
    y_mla = mla_mixer(cq, ckv, k_pe, q_norm, w_q_up, kv_norm, w_kv_up, cos, sin)
    y_sgu = sgu_mixer(u_sgu, v_sgu, sgu_g, sgu_b, w_s, b_s)
    gates = jax.nn.sigmoid(gate_logits).reshape(bsz, seq, N_BRANCH, D_MODEL)
    merged = (gates[:, :, 0] * (y_s5 @ w_branch[0])
              + gates[:, :, 1] * (y_mla @ w_branch[1])
              + gates[:, :, 2] * (y_sgu @ w_branch[2]))
    return merged @ w_out


def swiglu(h, w_in, w_out):
    a, b = jnp.split(h @ w_in, 2, axis=-1)
    return (jax.nn.silu(a) * b) @ w_out


def setup_inputs(seed: int = 0) -> dict:
    key = jax.random.key(seed)
    ks = jax.random.split(key, 32)
    L = DEPTH

    def nrm(k, shape, std):
        return jax.random.normal(k, shape, jnp.float32) * std

    def gain(k, shape):
        return 1.0 + nrm(k, shape, 0.01)

    lam_im0 = jnp.broadcast_to(math.pi * jnp.arange(S5_STATE, dtype=jnp.float32), (L, S5_GROUPS, S5_STATE))
    return {
        'x': nrm(ks[0], (BATCH, SEQ, D_MODEL), 1.0),
        'c': nrm(ks[1], (BATCH, D_MODEL), 1.0),
        'w_ada': nrm(ks[2], (L, D_MODEL, 6 * D_MODEL), 0.02),
        'b_ada': nrm(ks[3], (L, 6 * D_MODEL), 0.01),
        'w_in': nrm(ks[4], (L, D_MODEL, IN_WIDTH), D_MODEL ** -0.5),
        'b_in': nrm(ks[5], (L, IN_WIDTH), 0.01),
        's5_lambda_re': -0.5 + nrm(ks[6], (L, S5_GROUPS, S5_STATE), 0.01),
        's5_lambda_im': lam_im0 + nrm(ks[7], (L, S5_GROUPS, S5_STATE), 0.01),
        's5_log_dt': jax.random.uniform(ks[8], (L, S5_GROUPS), jnp.float32, math.log(DT_MIN), math.log(DT_MAX)),
        's5_b_re': nrm(ks[9], (L, S5_GROUPS, S5_STATE, S5_GROUP), (2 * S5_GROUP) ** -0.5),
        's5_b_im': nrm(ks[10], (L, S5_GROUPS, S5_STATE, S5_GROUP), (2 * S5_GROUP) ** -0.5),
        's5_c_re': nrm(ks[11], (L, S5_GROUPS, S5_GROUP, S5_STATE), S5_STATE ** -0.5),
        's5_c_im': nrm(ks[12], (L, S5_GROUPS, S5_GROUP, S5_STATE), S5_STATE ** -0.5),
        's5_d': nrm(ks[13], (L, S5_WIDTH), 1.0),
        's5_w_glu': nrm(ks[14], (L, S5_WIDTH, S5_WIDTH), S5_WIDTH ** -0.5),
        's5_b_glu': nrm(ks[15], (L, S5_WIDTH), 0.01),
        'mla_q_norm': gain(ks[16], (L, Q_LORA)),
        'mla_w_q_up': nrm(ks[17], (L, Q_LORA, MLA_HEADS * (QK_NOPE + QK_ROPE)), Q_LORA ** -0.5),
        'mla_kv_norm': gain(ks[18], (L, KV_LORA)),
        'mla_w_kv_up': nrm(ks[19], (L, KV_LORA, MLA_HEADS * (QK_NOPE + V_HEAD)), KV_LORA ** -0.5),
        'sgu_ln_g': gain(ks[20], (L, SGU_WIDTH)),
        'sgu_ln_b': nrm(ks[21], (L, SGU_WIDTH), 0.01),
        'sgu_w_s': nrm(ks[22], (L, SGU_GROUPS, SGU_CHUNK, SGU_CHUNK), SGU_CHUNK ** -0.5),
        'sgu_b_s': gain(ks[23], (L, SGU_GROUPS, SGU_CHUNK)),
        'w_branch': nrm(ks[24], (L, N_BRANCH, BRANCH_WIDTH, D_MODEL), BRANCH_WIDTH ** -0.5),
        'w_out': nrm(ks[25], (L, D_MODEL, D_MODEL), DEEPNORM_BETA * D_MODEL ** -0.5),
        'ln1_g': gain(ks[26], (L, D_MODEL)),
        'ln1_b': nrm(ks[27], (L, D_MODEL), 0.01),
        'ffn_w_in': nrm(ks[28], (L, D_MODEL, 2 * FF_HIDDEN), D_MODEL ** -0.5),
        'ffn_w_out': nrm(ks[29], (L, FF_HIDDEN, D_MODEL), DEEPNORM_BETA * FF_HIDDEN ** -0.5),
        'ln2_g': gain(ks[30], (L, D_MODEL)),
        'ln2_b': nrm(ks[31], (L, D_MODEL), 0.01),
    }


def reference(x, c, w_ada, b_ada, w_in, b_in, s5_lambda_re, s5_lambda_im, s5_log_dt, s5_b_re, s5_b_im,
              s5_c_re, s5_c_im, s5_d, s5_w_glu, s5_b_glu, mla_q_norm, mla_w_q_up, mla_kv_norm, mla_w_kv_up,
              sgu_ln_g, sgu_ln_b, sgu_w_s, sgu_b_s, w_branch, w_out, ln1_g, ln1_b, ffn_w_in, ffn_w_out,
              ln2_g, ln2_b):
    seq = x.shape[1]
    inv_freq = 1.0 / (ROPE_THETA ** (jnp.arange(0, QK_ROPE, 2, dtype=jnp.float32) / QK_ROPE))
    ang = jnp.arange(seq, dtype=jnp.float32)[:, None] * inv_freq[None, :]
    cos = jnp.cos(ang).astype(x.dtype)
    sin = jnp.sin(ang).astype(x.dtype)
    c_act = jax.nn.silu(c)
    for l in range(DEPTH):
        ada = (c_act @ w_ada[l] + b_ada[l])[:, None, :]
        sh1, sc1, g1, sh2, sc2, g2 = jnp.split(ada, 6, axis=-1)
        h = x * (1.0 + sc1) + sh1
        y = hybrid_mixer(h, w_in[l], b_in[l], s5_lambda_re[l], s5_lambda_im[l], s5_log_dt[l],
                         s5_b_re[l], s5_b_im[l], s5_c_re[l], s5_c_im[l], s5_d[l], s5_w_glu[l], s5_b_glu[l],
                         mla_q_norm[l], mla_w_q_up[l], mla_kv_norm[l], mla_w_kv_up[l],
                         sgu_ln_g[l], sgu_ln_b[l], sgu_w_s[l], sgu_b_s[l], w_branch[l], w_out[l], cos, sin)
        x = layer_norm(DEEPNORM_ALPHA * x + (1.0 + g1) * y, ln1_g[l], ln1_b[l])
        h = x * (1.0 + sc2) + sh2
        f = swiglu(h, ffn_w_in[l], ffn_w_out[l])
        x = layer_norm(DEEPNORM_ALPHA * x + (1.0 + g2) * f, ln2_g[l], ln2_b[l])
    return x
```

```python
import functools
import math

import numpy as np
import jax
import jax.numpy as jnp
from jax import lax
from jax.experimental import pallas as pl
from jax.experimental.pallas import tpu as pltpu

F32 = jnp.float32
BF16 = jnp.bfloat16

CHUNK = 64
S5_WIDTH = 512
S5_GROUP = 16
S5_GROUPS = S5_WIDTH // S5_GROUP
S5_STATE = 64
MLA_HEADS = 8
QK_NOPE = 64
QK_ROPE = 32
V_HEAD = 64
Q_LORA = 384
KV_LORA = 256
ROPE_THETA = 10000.0
SGU_WIDTH = 512
SGU_GROUPS = 4
SGU_CHUNK = 128
N_BRANCH = 3
BRANCH_WIDTH = 512
LN_EPS = 1e-5
RMS_EPS = 1e-6
NEG_INF = -1e30

LANES = 128
HEAD_PAD = 128
S5_SUB = 16
S5_NSUB = CHUNK // S5_SUB

OFF_U5 = 0
OFF_CQ = OFF_U5 + S5_WIDTH
OFF_CKV = OFF_CQ + Q_LORA
OFF_USGU = OFF_CKV + KV_LORA
OFF_VSGU = OFF_USGU + SGU_WIDTH
OFF_KPE = OFF_VSGU + SGU_WIDTH
NA = OFF_KPE + LANES

VMEM_LIMIT = 56 * 1024 * 1024

HI = lax.Precision.HIGHEST


def _gelu(x):
    return 0.5 * x * (1.0 + jnp.tanh(0.7978845608028654 * (x + 0.044715 * (x * x * x))))


def _sigmoid(x):
    return 0.5 * jnp.tanh(0.5 * x) + 0.5


def _layer_norm(r, g, b):
    mu = jnp.mean(r, axis=-1, keepdims=True)
    rc = r - mu
    var = jnp.mean(rc * rc, axis=-1, keepdims=True)
    return rc * lax.rsqrt(var + LN_EPS) * g + b


def _const_spec(shape):
    nd = len(shape)
    return pl.BlockSpec(shape, lambda *_: (0,) * nd, pipeline_mode=pl.Buffered(1))


def _params(sem):
    return pltpu.CompilerParams(dimension_semantics=sem, vmem_limit_bytes=VMEM_LIMIT)


def _ada_kernel(c_ref, w_ref, b_ref, o_ref):
    c = c_ref[...]
    ca = c * _sigmoid(c)
    o_ref[0] = jnp.dot(ca, w_ref[0], preferred_element_type=F32, precision=HI) + b_ref[0]


def _ada_call(c_pad, w_ada, b_ada):
    depth, d, n6 = w_ada.shape
    tn = n6 // 4
    rows = c_pad.shape[0]
    return pl.pallas_call(
        _ada_kernel,
        grid=(depth, n6 // tn),
        in_specs=[
            pl.BlockSpec((rows, d), lambda l, j: (0, 0)),
            pl.BlockSpec((1, d, tn), lambda l, j: (l, 0, j)),
            pl.BlockSpec((1, 1, tn), lambda l, j: (l, 0, j)),
        ],
        out_specs=pl.BlockSpec((1, rows, tn), lambda l, j: (l, 0, j)),
        out_shape=jax.ShapeDtypeStruct((depth, rows, n6), F32),
        compiler_params=_params(("arbitrary", "arbitrary")),
        name="ada",
    )(c_pad, w_ada, b_ada.reshape(depth, 1, n6))


def _inproj_kernel(x_ref, ada_ref, w_ref, b_ref, qn_ref, kvn_ref, lng_ref, lnb_ref, ws_ref, bs_ref,
                   cosk_ref, sink_ref,
                   h_ref, u5_ref, cq_ref, ckv_ref, ysgu_ref, krot_ref):
    tm = x_ref.shape[0]
    ada = ada_ref[0]
    h = (x_ref[...] * (1.0 + ada[1:2]) + ada[0:1]).astype(BF16)
    h_ref[...] = h
    acc = jnp.dot(h, w_ref[...], preferred_element_type=F32) + b_ref[...]

    u5_ref[...] = acc[:, OFF_U5:OFF_U5 + S5_WIDTH].astype(BF16)

    cq = acc[:, OFF_CQ:OFF_CQ + Q_LORA]
    cq_ref[...] = (cq * lax.rsqrt(jnp.mean(cq * cq, axis=-1, keepdims=True) + RMS_EPS) * qn_ref[...]).astype(BF16)
    ckv = acc[:, OFF_CKV:OFF_CKV + KV_LORA]
    ckv_ref[...] = (ckv * lax.rsqrt(jnp.mean(ckv * ckv, axis=-1, keepdims=True) + RMS_EPS)
                    * kvn_ref[...]).astype(BF16)

    kb = acc[:, OFF_KPE:OFF_KPE + LANES]
    krot = kb * cosk_ref[...] + pltpu.roll(kb, LANES - QK_ROPE, 1) * sink_ref[...]
    krot_ref[...] = krot.astype(BF16)

    us = _gelu(acc[:, OFF_USGU:OFF_USGU + SGU_WIDTH])
    vn = _layer_norm(_gelu(acc[:, OFF_VSGU:OFF_VSGU + SGU_WIDTH]), lng_ref[...], lnb_ref[...]).astype(BF16)
    gw = SGU_WIDTH // SGU_GROUPS
    for r in range(tm // SGU_CHUNK):
        rows = slice(r * SGU_CHUNK, (r + 1) * SGU_CHUNK)
        parts = [jnp.dot(ws_ref[g], vn[rows, g * gw:(g + 1) * gw], preferred_element_type=F32)
                 for g in range(SGU_GROUPS)]
        mixed = jnp.concatenate(parts, axis=1) + bs_ref[...]
        ysgu_ref[rows, :] = (us[rows, :] * mixed).astype(BF16)


def _inproj_call(x2, ada, w, b, qn, kvn, lng, lnb, ws, bs, cosk, sink, seq, tm):
    t, d = x2.shape
    tpb = seq // tm
    tok = lambda n: pl.BlockSpec((tm, n), lambda i: (i, 0))
    return pl.pallas_call(
        _inproj_kernel,
        grid=(t // tm,),
        in_specs=[
            tok(d),
            pl.BlockSpec((1, 6, d), lambda i: (i // tpb, 0, 0)),
            _const_spec(w.shape), _const_spec(b.shape), _const_spec(qn.shape), _const_spec(kvn.shape),
            _const_spec(lng.shape), _const_spec(lnb.shape), _const_spec(ws.shape), _const_spec(bs.shape),
            pl.BlockSpec((tm, LANES), lambda i: (i % tpb, 0)),
            pl.BlockSpec((tm, LANES), lambda i: (i % tpb, 0)),
        ],
        out_specs=[tok(d), tok(S5_WIDTH), tok(Q_LORA), tok(KV_LORA), tok(SGU_WIDTH), tok(LANES)],
        out_shape=[
            jax.ShapeDtypeStruct((t, d), BF16),
            jax.ShapeDtypeStruct((t, S5_WIDTH), BF16),
            jax.ShapeDtypeStruct((t, Q_LORA), BF16),
            jax.ShapeDtypeStruct((t, KV_LORA), BF16),
            jax.ShapeDtypeStruct((t, SGU_WIDTH), BF16),
            jax.ShapeDtypeStruct((t, LANES), BF16),
        ],
        compiler_params=_params(("arbitrary",)),
        name="inproj",
    )(x2, ada, w, b, qn, kvn, lng, lnb, ws, bs, cosk, sink)


def _mla_proj_kernel(cq_ref, ckv_ref, krot_ref, aq_ref, sq_ref, wq_ref, wkt_ref, pkt_ref, wv_ref, ones_ref,
                     q_ref, kt_ref, v_ref):
    q = jnp.dot(cq_ref[...], wq_ref[...], preferred_element_type=F32)
    aq = aq_ref[...]
    sq = sq_ref[...]
    for h in range(MLA_HEADS):
        cols = slice(h * HEAD_PAD, (h + 1) * HEAD_PAD)
        qh = q[:, cols]
        q_ref[:, cols] = (qh * aq + pltpu.roll(qh, HEAD_PAD - QK_ROPE, 1) * sq).astype(BF16)
    nt = (((1,), (1,)), ((), ()))
    kt = (lax.dot_general(wkt_ref[...], ckv_ref[...], nt, preferred_element_type=F32)
          + lax.dot_general(pkt_ref[...], krot_ref[...], nt, preferred_element_type=F32))
    kt_ref[0] = kt.astype(BF16)
    v_ref[...] = (jnp.dot(ckv_ref[...], wv_ref[...], preferred_element_type=F32) + ones_ref[...]).astype(BF16)


def _mla_proj_call(cqn, ckvn, krot, aq, sq, wq, wkt, pkt, wv, ones, batch, seq, tm):
    t = cqn.shape[0]
    tpb = seq // tm
    hw = MLA_HEADS * HEAD_PAD
    tok = lambda n: pl.BlockSpec((tm, n), lambda i: (i, 0))
    tab = pl.BlockSpec((tm, LANES), lambda i: (i % tpb, 0))
    return pl.pallas_call(
        _mla_proj_kernel,
        grid=(t // tm,),
        in_specs=[tok(Q_LORA), tok(KV_LORA), tok(LANES), tab, tab,
                  _const_spec(wq.shape), _const_spec(wkt.shape), _const_spec(pkt.shape), _const_spec(wv.shape),
                  _const_spec(ones.shape)],
        out_specs=[tok(hw), pl.BlockSpec((1, hw, tm), lambda i: (i // tpb, 0, i % tpb)), tok(hw)],
        out_shape=[
            jax.ShapeDtypeStruct((t, hw), BF16),
            jax.ShapeDtypeStruct((batch, hw, seq), BF16),
            jax.ShapeDtypeStruct((t, hw), BF16),
        ],
        compiler_params=_params(("arbitrary",)),
        name="mla_proj",
    )(cqn, ckvn, krot, aq, sq, wq, wkt, pkt, wv, ones)


def _flash_kernel(qi_ref, ki_ref, q_ref, kt_ref, v_ref, o_ref, m_sc, acc_sc):
    p_idx = pl.program_id(1)
    qi = qi_ref[p_idx]
    ki = ki_ref[p_idx]
    tq = q_ref.shape[0]
    tk = v_ref.shape[0]

    @pl.when(ki == 0)
    def _init():
        m_sc[...] = jnp.full(m_sc.shape, NEG_INF, F32)
        acc_sc[...] = jnp.zeros(acc_sc.shape, F32)

    def step(masked):
        if masked:
            qc = lax.broadcasted_iota(jnp.int32, (tq, tk), 0) // CHUNK
            kc = lax.broadcasted_iota(jnp.int32, (tq, tk), 1) // CHUNK
            visible = kc <= qc
        for h in range(MLA_HEADS):
            cols = slice(h * HEAD_PAD, (h + 1) * HEAD_PAD)
            s = jnp.dot(q_ref[:, cols], kt_ref[0, cols, :], preferred_element_type=F32)
            if masked:
                s = jnp.where(visible, s, NEG_INF)
            m_prev = m_sc[h]
            m_new = jnp.maximum(m_prev, jnp.max(s, axis=1, keepdims=True))
            alpha = jnp.exp(m_prev - m_new)
            p = jnp.exp(s - jnp.concatenate([m_new] * (tk // LANES), axis=1))
            m_sc[h] = m_new
            acc_sc[h] = alpha * acc_sc[h] + jnp.dot(p.astype(BF16), v_ref[:, cols], preferred_element_type=F32)

    @pl.when(ki < qi)
    def _full():
        step(False)

    @pl.when(ki == qi)
    def _diag():
        step(True)
        outs = []
        for h in range(MLA_HEADS):
            acc = acc_sc[h]
            outs.append(acc[:, :V_HEAD] / acc[:, V_HEAD:V_HEAD + 1])
        o_ref[...] = jnp.concatenate(outs, axis=1).astype(BF16)


def _flash_call(q, kt, v, batch, seq, tq):
    t, hw = q.shape
    nq = seq // tq
    pairs = [(a, b) for a in range(nq) for b in range(a + 1)]
    qi_tab = jnp.asarray([p[0] for p in pairs], jnp.int32)
    ki_tab = jnp.asarray([p[1] for p in pairs], jnp.int32)
    grid_spec = pltpu.PrefetchScalarGridSpec(
        num_scalar_prefetch=2,
        grid=(batch, len(pairs)),
        in_specs=[
            pl.BlockSpec((tq, hw), lambda b, p, qt, kt_: (b * nq + qt[p], 0)),
            pl.BlockSpec((1, hw, tq), lambda b, p, qt, kt_: (b, 0, kt_[p])),
            pl.BlockSpec((tq, hw), lambda b, p, qt, kt_: (b * nq + kt_[p], 0)),
        ],
        out_specs=pl.BlockSpec((tq, MLA_HEADS * V_HEAD), lambda b, p, qt, kt_: (b * nq + qt[p], 0)),
        scratch_shapes=[
            pltpu.VMEM((MLA_HEADS, tq, LANES), F32),
            pltpu.VMEM((MLA_HEADS, tq, HEAD_PAD), F32),
        ],
    )
    return pl.pallas_call(
        _flash_kernel,
        grid_spec=grid_spec,
        out_shape=jax.ShapeDtypeStruct((t, MLA_HEADS * V_HEAD), BF16),
        compiler_params=_params(("arbitrary", "arbitrary")),
        name="flash",
    )(qi_tab, ki_tab, q, kt, v)


def _s5_kernel(u_ref, d_ref, ws_ref, wo_ref, pa_ref, pb_ref, y_ref, *, nch):
    u = u_ref[0]
    rows = u.shape[0]
    pw = 2 * S5_STATE
    h = jnp.dot(u, ws_ref[0], preferred_element_type=F32)
    n_idx = lax.broadcasted_iota(jnp.int32, (rows, pw), 0) % nch
    step = 1
    j = 0
    while step < nch:
        sh = jnp.where(n_idx >= step, pltpu.roll(h, step, 0), 0.0)
        h = h + sh * pa_ref[0, j:j + 1, :] + pltpu.roll(sh, S5_STATE, 1) * pb_ref[0, j:j + 1, :]
        step *= 2
        j += 1
    hprev = jnp.where(n_idx >= 1, pltpu.roll(h, 1, 0), 0.0).astype(BF16)
    blk = S5_SUB * S5_GROUP
    for jj in range(S5_NSUB):
        cols = slice(jj * blk, (jj + 1) * blk)
        acc = jnp.dot(hprev, wo_ref[0, :, cols], preferred_element_type=F32)
        for ii in range(jj + 1):
            acc = acc + jnp.dot(u[:, ii * blk:(ii + 1) * blk], d_ref[0, jj - ii], preferred_element_type=F32)
        y_ref[0, :, cols] = acc


def _s5_call(u_g, dblk, ws, wo, pa, pb, nch):
    g, rows, width = u_g.shape
    spec = lambda shape: pl.BlockSpec((1,) + shape, lambda i: (i,) + (0,) * len(shape))
    return pl.pallas_call(
        functools.partial(_s5_kernel, nch=nch),
        grid=(g,),
        in_specs=[spec((rows, width)), spec(dblk.shape[1:]), spec(ws.shape[1:]), spec(wo.shape[1:]),
                  spec(pa.shape[1:]), spec(pb.shape[1:])],
        out_specs=spec((rows, width)),
        out_shape=jax.ShapeDtypeStruct((g, rows, width), F32),
        compiler_params=_params(("arbitrary",)),
        name="s5",
    )(u_g, dblk, ws, wo, pa, pb)


def _merge_kernel(h_ref, yssm_ref, u5_ref, ymla_ref, ysgu_ref, x_ref, ada_ref,
                  wg_ref, bg_ref, d_ref, wglu_ref, bglu_ref, wb_ref, wout_ref, lng_ref, lnb_ref,
                  o_ref, *, alpha):
    d = x_ref.shape[1]
    h = h_ref[...]
    z = _gelu(yssm_ref[...] + d_ref[...] * u5_ref[...].astype(F32))
    y5 = (z * _sigmoid(jnp.dot(z.astype(BF16), wglu_ref[...], preferred_element_type=F32)
                       + bglu_ref[...])).astype(BF16)
    branches = (y5, ymla_ref[...], ysgu_ref[...])
    merged = None
    for i in range(N_BRANCH):
        cols = slice(i * d, (i + 1) * d)
        gate = _sigmoid(jnp.dot(h, wg_ref[:, cols], preferred_element_type=F32) + bg_ref[:, cols])
        term = gate * jnp.dot(branches[i], wb_ref[i], preferred_element_type=F32)
        merged = term if merged is None else merged + term
    y = jnp.dot(merged.astype(BF16), wout_ref[...], preferred_element_type=F32)
    g1 = ada_ref[0][2:3]
    o_ref[...] = _layer_norm(alpha * x_ref[...] + (1.0 + g1) * y, lng_ref[...], lnb_ref[...])


def _merge_call(h, yssm, u5, ymla, ysgu, x2, ada, wg, bg, dskip, wglu, bglu, wb, wout, lng, lnb, seq, tm, alpha):
    t, d = x2.shape
    tpb = seq // tm
    tok = lambda n: pl.BlockSpec((tm, n), lambda i: (i, 0))
    consts = [wg, bg, dskip, wglu, bglu, wb, wout, lng, lnb]
    return pl.pallas_call(
        functools.partial(_merge_kernel, alpha=alpha),
        grid=(t // tm,),
        in_specs=[tok(d), tok(S5_WIDTH), tok(S5_WIDTH), tok(BRANCH_WIDTH), tok(BRANCH_WIDTH), tok(d),
                  pl.BlockSpec((1, 6, d), lambda i: (i // tpb, 0, 0))] + [_const_spec(a.shape) for a in consts],
        out_specs=tok(d),
        out_shape=jax.ShapeDtypeStruct((t, d), F32),
        compiler_params=_params(("arbitrary",)),
        name="merge",
    )(h, yssm, u5, ymla, ysgu, x2, ada, *consts)


def _ffn_kernel(x_ref, ada_ref, wa_ref, wb_ref, wo_ref, lng_ref, lnb_ref, o_ref, *, alpha):
    x = x_ref[...]
    ada = ada_ref[0]
    h = (x * (1.0 + ada[4:5]) + ada[3:4]).astype(BF16)
    a = jnp.dot(h, wa_ref[...], preferred_element_type=F32)
    b = jnp.dot(h, wb_ref[...], preferred_element_type=F32)
    act = (a * _sigmoid(a) * b).astype(BF16)
    f = jnp.dot(act, wo_ref[...], preferred_element_type=F32)
    o_ref[...] = _layer_norm(alpha * x + (1.0 + ada[5:6]) * f, lng_ref[...], lnb_ref[...])


def _ffn_call(x2, ada, wa, wb, wo, lng, lnb, seq, tm, alpha):
    t, d = x2.shape
    tpb = seq // tm
    tok = pl.BlockSpec((tm, d), lambda i: (i, 0))
    consts = [wa, wb, wo, lng, lnb]
    return pl.pallas_call(
        functools.partial(_ffn_kernel, alpha=alpha),
        grid=(t // tm,),
        in_specs=[tok, pl.BlockSpec((1, 6, d), lambda i: (i // tpb, 0, 0))] + [_const_spec(a.shape) for a in consts],
        out_specs=tok,
        out_shape=jax.ShapeDtypeStruct((t, d), F32),
        compiler_params=_params(("arbitrary",)),
        name="ffn",
    )(x2, ada, *consts)


def _s5_operators(lam_re, lam_im, log_dt, b_re, b_im, c_re, c_im, nch):
    dt = jnp.exp(log_dt)[:, None]
    lr, li = lam_re, lam_im
    mag = jnp.exp(lr * dt)
    a_re = mag * jnp.cos(li * dt)
    a_im = mag * jnp.sin(li * dt)
    den = lr * lr + li * li
    f_re = ((a_re - 1.0) * lr + a_im * li) / den
    f_im = (a_im * lr - (a_re - 1.0) * li) / den
    bb_re = f_re[..., None] * b_re - f_im[..., None] * b_im
    bb_im = f_re[..., None] * b_im + f_im[..., None] * b_re

    def a_pow(k):
        k = jnp.asarray(k, F32)
        m = jnp.exp(lr[..., None] * dt[..., None] * k)
        ang = li[..., None] * dt[..., None] * k
        return m * jnp.cos(ang), m * jnp.sin(ang)

    ak_re, ak_im = a_pow(np.arange(CHUNK + 1))
    ab_re = ak_re.transpose(0, 2, 1)[..., None] * bb_re[:, None] - ak_im.transpose(0, 2, 1)[..., None] * bb_im[:, None]
    ab_im = ak_re.transpose(0, 2, 1)[..., None] * bb_im[:, None] + ak_im.transpose(0, 2, 1)[..., None] * bb_re[:, None]
    kker = (jnp.einsum('gcp,gkpd->gkcd', c_re, ab_re[:, :CHUNK], precision=HI)
            - jnp.einsum('gcp,gkpd->gkcd', c_im, ab_im[:, :CHUNK], precision=HI))
    sub = np.arange(S5_SUB)
    lag = (S5_SUB * np.arange(S5_NSUB)[:, None, None] + sub[None, None, :] - sub[None, :, None])
    blocks = kker[:, np.maximum(lag, 0)]
    blocks = jnp.where((lag >= 0)[None, :, :, :, None, None], blocks, 0.0)
    g = lam_re.shape[0]
    blk = S5_SUB * S5_GROUP
    dblk = blocks.transpose(0, 1, 2, 5, 3, 4).reshape(g, S5_NSUB, blk, blk)
    rev = CHUNK - 1 - np.arange(CHUNK)
    ws = jnp.concatenate([ab_re[:, rev], ab_im[:, rev]], axis=2)
    ws = ws.transpose(0, 1, 3, 2).reshape(g, CHUNK * S5_GROUP, 2 * S5_STATE)
    at_re = ak_re[:, :, 1:]
    at_im = ak_im[:, :, 1:]
    wo_re = c_re.transpose(0, 2, 1)[:, :, None, :] * at_re[..., None] - c_im.transpose(0, 2, 1)[:, :, None, :] * at_im[..., None]
    wo_im = -(c_re.transpose(0, 2, 1)[:, :, None, :] * at_im[..., None] + c_im.transpose(0, 2, 1)[:, :, None, :] * at_re[..., None])
    wo = jnp.concatenate([wo_re, wo_im], axis=1).reshape(g, 2 * S5_STATE, CHUNK * S5_GROUP)
    nsteps = max(1, int(math.ceil(math.log2(nch)))) if nch > 1 else 1
    sr, si = a_pow(CHUNK * (2 ** np.arange(nsteps)))
    sr = sr.transpose(0, 2, 1)
    si = si.transpose(0, 2, 1)
    pa = jnp.concatenate([sr, sr], axis=2)
    pb = jnp.concatenate([-si, si], axis=2)
    return dblk.astype(BF16), ws.astype(BF16), wo.astype(BF16), pa, pb


def _rope_tables(seq, scale):
    inv_freq = 1.0 / (ROPE_THETA ** (jnp.arange(0, QK_ROPE, 2, dtype=F32) / QK_ROPE))
    ang = jnp.arange(seq, dtype=F32)[:, None] * inv_freq[None, :]
    cos = jnp.cos(ang)
    sin = jnp.sin(ang)
    half = QK_ROPE // 2
    zk = jnp.zeros((seq, LANES - QK_ROPE), F32)
    cosk = jnp.concatenate([cos, cos, zk], axis=1)
    sink = jnp.concatenate([-sin, sin, zk], axis=1)
    zq = jnp.zeros((seq, HEAD_PAD - QK_NOPE - QK_ROPE), F32)
    aq = jnp.concatenate([jnp.full((seq, QK_NOPE), scale, F32), scale * cos, scale * cos, zq], axis=1)
    sq = jnp.concatenate([jnp.zeros((seq, QK_NOPE), F32), -scale * sin, scale * sin, zq], axis=1)
    del half
    return cosk, sink, aq, sq


def _swap_halves(w):
    half = w.shape[-1] // 2
    return jnp.concatenate([w[..., half:], w[..., :half]], axis=-1)


def kernel(x, c, w_ada, b_ada, w_in, b_in, s5_lambda_re, s5_lambda_im, s5_log_dt, s5_b_re, s5_b_im, s5_c_re, s5_c_im, s5_d, s5_w_glu, s5_b_glu, mla_q_norm, mla_w_q_up, mla_kv_norm, mla_w_kv_up, sgu_ln_g, sgu_ln_b, sgu_w_s, sgu_b_s, w_branch, w_out, ln1_g, ln1_b, ffn_w_in, ffn_w_out, ln2_g, ln2_b):
    batch, seq, d = x.shape
    depth = w_ada.shape[0]
    t = batch * seq
    ff = ffn_w_out.shape[1]
    alpha = float((2 * depth) ** 0.25)
    tm = 512
    tq = 512
    nch = seq // CHUNK
    assert seq % tm == 0 and seq % tq == 0 and seq % SGU_CHUNK == 0

    c_pad = jnp.pad(c, ((0, 8 - batch % 8 if batch % 8 else 0), (0, 0)))
    ada_all = _ada_call(c_pad, w_ada, b_ada)[:, :batch].reshape(depth, batch, 6, d)

    scale = (QK_NOPE + QK_ROPE) ** -0.5
    cosk, sink, aq, sq = _rope_tables(seq, scale)

    offs = np.cumsum([0, S5_WIDTH, Q_LORA, KV_LORA, QK_ROPE, SGU_WIDTH, SGU_WIDTH])
    o_u5, o_cq, o_ckv, o_kpe, o_usgu, o_vsgu, o_gate = [int(o) for o in offs]

    hw = MLA_HEADS * HEAD_PAD
    pos = np.arange(SGU_CHUNK) // CHUNK
    sgu_mask = jnp.asarray(pos[None, :] <= pos[:, None])
    ones_row = np.zeros((1, hw), np.float32)
    ones_row[0, np.arange(MLA_HEADS) * HEAD_PAD + V_HEAD] = 1.0
    ones_row = jnp.asarray(ones_row)
    place = np.zeros((LANES, hw), np.float32)
    for hh in range(MLA_HEADS):
        place[np.arange(QK_ROPE), hh * HEAD_PAD + QK_NOPE + np.arange(QK_ROPE)] = 1.0
    pkt = jnp.asarray(place.T, BF16)

    x2 = x.reshape(t, d)
    for l in range(depth):
        ada = ada_all[l]
        wl = w_in[l]
        bl = b_in[l]
        kpe_w = wl[:, o_kpe:o_kpe + QK_ROPE]
        kpe_b = bl[o_kpe:o_kpe + QK_ROPE]
        zpad_w = jnp.zeros((d, LANES - 2 * QK_ROPE), F32)
        zpad_b = jnp.zeros((LANES - 2 * QK_ROPE,), F32)
        w_a = jnp.concatenate([wl[:, o_u5:o_kpe], wl[:, o_usgu:o_gate], kpe_w, _swap_halves(kpe_w), zpad_w],
                              axis=1).astype(BF16)
        b_a = jnp.concatenate([bl[o_u5:o_kpe], bl[o_usgu:o_gate], kpe_b, _swap_halves(kpe_b), zpad_b])[None, :]
        ws_m = jnp.where(sgu_mask[None], sgu_w_s[l], 0.0).astype(BF16)
        bs_full = jnp.repeat(sgu_b_s[l].T, SGU_WIDTH // SGU_GROUPS, axis=1)

        h, u5, cqn, ckvn, ysgu, krot = _inproj_call(
            x2, ada, w_a, b_a, mla_q_norm[l][None, :], mla_kv_norm[l][None, :],
            sgu_ln_g[l][None, :], sgu_ln_b[l][None, :], ws_m, bs_full, cosk, sink, seq, tm)

        wq3 = mla_w_q_up[l].reshape(Q_LORA, MLA_HEADS, QK_NOPE + QK_ROPE)
        wq_pe = wq3[:, :, QK_NOPE:]
        wq = jnp.concatenate([wq3, _swap_halves(wq_pe)], axis=2).reshape(Q_LORA, hw).astype(BF16)
        wkv3 = mla_w_kv_up[l].reshape(KV_LORA, MLA_HEADS, QK_NOPE + V_HEAD)
        zhalf = jnp.zeros((KV_LORA, MLA_HEADS, HEAD_PAD - QK_NOPE), F32)
        wk = jnp.concatenate([wkv3[:, :, :QK_NOPE], zhalf], axis=2).reshape(KV_LORA, hw)
        wkt = wk.T.astype(BF16)
        wv = jnp.concatenate([wkv3[:, :, QK_NOPE:], zhalf], axis=2).reshape(KV_LORA, hw).astype(BF16)
        q, kt, v = _mla_proj_call(cqn, ckvn, krot, aq, sq, wq, wkt, pkt, wv, ones_row, batch, seq, tm)
        ymla = _flash_call(q, kt, v, batch, seq, tq)

        dblk, ws5, wo5, pa, pb = _s5_operators(s5_lambda_re[l], s5_lambda_im[l], s5_log_dt[l],
                                               s5_b_re[l], s5_b_im[l], s5_c_re[l], s5_c_im[l], nch)
        u_g = (u5.reshape(batch * nch, CHUNK, S5_GROUPS, S5_GROUP).transpose(2, 0, 1, 3)
               .reshape(S5_GROUPS, batch * nch, CHUNK * S5_GROUP))
        y_g = _s5_call(u_g, dblk, ws5, wo5, pa, pb, nch)
        yssm = (y_g.reshape(S5_GROUPS, batch * nch, CHUNK, S5_GROUP).transpose(1, 2, 0, 3)
                .reshape(t, S5_WIDTH))

        x2 = _merge_call(
            h, yssm, u5, ymla, ysgu, x2, ada,
            wl[:, o_gate:].astype(BF16), bl[o_gate:][None, :], s5_d[l][None, :],
            s5_w_glu[l].astype(BF16), s5_b_glu[l][None, :], w_branch[l].astype(BF16), w_out[l].astype(BF16),
            ln1_g[l][None, :], ln1_b[l][None, :], seq, tm, alpha)

        x2 = _ffn_call(x2, ada, ffn_w_in[l][:, :ff].astype(BF16), ffn_w_in[l][:, ff:].astype(BF16),
                       ffn_w_out[l].astype(BF16), ln2_g[l][None, :], ln2_b[l][None, :], seq, tm, alpha)
    return x2.reshape(batch, seq, d)
```

```python
import functools
import math

import numpy as np
import jax
import jax.numpy as jnp
from jax import lax
from jax.experimental import pallas as pl
from jax.experimental.pallas import tpu as pltpu

F32 = jnp.float32
BF16 = jnp.bfloat16

CHUNK = 64
S5_WIDTH = 512
S5_GROUP = 16
S5_GROUPS = S5_WIDTH // S5_GROUP
S5_STATE = 64
MLA_HEADS = 8
QK_NOPE = 64
QK_ROPE = 32
V_HEAD = 64
Q_LORA = 384
KV_LORA = 256
ROPE_THETA = 10000.0
SGU_WIDTH = 512
SGU_GROUPS = 4
SGU_CHUNK = 128
N_BRANCH = 3
BRANCH_WIDTH = 512
LN_EPS = 1e-5
RMS_EPS = 1e-6
NEG_INF = -1e30

LANES = 128
SUBLANES = 8
HEAD_PAD = 128
S5_SUB = 16
S5_NSUB = CHUNK // S5_SUB
S5_BLK = S5_SUB * S5_GROUP
S5_FLAT = CHUNK * S5_GROUP
S5_GPB = LANES // S5_GROUP
S5_SCAN_COLS = 8

OFF_U5 = 0
OFF_CQ = OFF_U5 + S5_WIDTH
OFF_CKV = OFF_CQ + Q_LORA
OFF_USGU = OFF_CKV + KV_LORA
OFF_VSGU = OFF_USGU + SGU_WIDTH
OFF_KPE = OFF_VSGU + SGU_WIDTH
NA = OFF_KPE + LANES

VMEM_LIMIT = 56 * 1024 * 1024

HI = lax.Precision.HIGHEST
NT_DIMS = (((1,), (1,)), ((), ()))


def _gelu(x):
    return 0.5 * x * (1.0 + jnp.tanh(0.7978845608028654 * (x + 0.044715 * (x * x * x))))


def _sigmoid(x):
    return 0.5 * jnp.tanh(0.5 * x) + 0.5


def _layer_norm(r, g, b):
    mu = jnp.mean(r, axis=-1, keepdims=True)
    rc = r - mu
    var = jnp.mean(rc * rc, axis=-1, keepdims=True)
    return rc * lax.rsqrt(var + LN_EPS) * g + b


def _const_spec(shape):
    nd = len(shape)
    return pl.BlockSpec(shape, lambda *_: (0,) * nd, pipeline_mode=pl.Buffered(1))


def _params(sem):
    return pltpu.CompilerParams(dimension_semantics=sem, vmem_limit_bytes=VMEM_LIMIT)


def _ada_kernel(c_ref, w_ref, b_ref, o_ref):
    c = c_ref[...]
    ca = c * _sigmoid(c)
    o_ref[0] = jnp.dot(ca, w_ref[0], preferred_element_type=F32, precision=HI) + b_ref[0]


def _ada_call(c_pad, w_ada, b_ada):
    depth, d, n6 = w_ada.shape
    tn = n6 // 4
    rows = c_pad.shape[0]
    return pl.pallas_call(
        _ada_kernel,
        grid=(depth, n6 // tn),
        in_specs=[
            pl.BlockSpec((rows, d), lambda l, j: (0, 0)),
            pl.BlockSpec((1, d, tn), lambda l, j: (l, 0, j)),
            pl.BlockSpec((1, 1, tn), lambda l, j: (l, 0, j)),
        ],
        out_specs=pl.BlockSpec((1, rows, tn), lambda l, j: (l, 0, j)),
        out_shape=jax.ShapeDtypeStruct((depth, rows, n6), F32),
        compiler_params=_params(("arbitrary", "arbitrary")),
        name="ada",
    )(c_pad, w_ada, b_ada.reshape(depth, 1, n6))


def _inproj_kernel(x_ref, ada_ref, w_ref, b_ref, qn_ref, kvn_ref, lng_ref, lnb_ref, ws_ref, bs_ref,
                   cosk_ref, sink_ref,
                   h_ref, u5_ref, cq_ref, ckv_ref, ysgu_ref, krot_ref):
    tm = x_ref.shape[0]
    ada = ada_ref[0]
    h = (x_ref[...] * (1.0 + ada[1:2]) + ada[0:1]).astype(BF16)
    h_ref[...] = h
    acc = jnp.dot(h, w_ref[...], preferred_element_type=F32) + b_ref[...]

    u5_ref[...] = acc[:, OFF_U5:OFF_U5 + S5_WIDTH]

    cq = acc[:, OFF_CQ:OFF_CQ + Q_LORA]
    cq_ref[...] = (cq * lax.rsqrt(jnp.mean(cq * cq, axis=-1, keepdims=True) + RMS_EPS) * qn_ref[...]).astype(BF16)
    ckv = acc[:, OFF_CKV:OFF_CKV + KV_LORA]
    ckv_ref[...] = (ckv * lax.rsqrt(jnp.mean(ckv * ckv, axis=-1, keepdims=True) + RMS_EPS)
                    * kvn_ref[...]).astype(BF16)

    kb = acc[:, OFF_KPE:OFF_KPE + LANES]
    krot = kb * cosk_ref[...] + pltpu.roll(kb, LANES - QK_ROPE, 1) * sink_ref[...]
    krot_ref[...] = krot.astype(BF16)

    us = _gelu(acc[:, OFF_USGU:OFF_USGU + SGU_WIDTH])
    vn = _layer_norm(_gelu(acc[:, OFF_VSGU:OFF_VSGU + SGU_WIDTH]), lng_ref[...], lnb_ref[...]).astype(BF16)
    gw = SGU_WIDTH // SGU_GROUPS
    for r in range(tm // SGU_CHUNK):
        rows = slice(r * SGU_CHUNK, (r + 1) * SGU_CHUNK)
        parts = [jnp.dot(ws_ref[g], vn[rows, g * gw:(g + 1) * gw], preferred_element_type=F32)
                 for g in range(SGU_GROUPS)]
        mixed = jnp.concatenate(parts, axis=1) + bs_ref[...]
        ysgu_ref[rows, :] = (us[rows, :] * mixed).astype(BF16)


def _inproj_call(x2, ada, w, b, qn, kvn, lng, lnb, ws, bs, cosk, sink, seq, tm):
    t, d = x2.shape
    tpb = seq // tm
    tok = lambda n: pl.BlockSpec((tm, n), lambda i: (i, 0))
    return pl.pallas_call(
        _inproj_kernel,
        grid=(t // tm,),
        in_specs=[
            tok(d),
            pl.BlockSpec((1, 6, d), lambda i: (i // tpb, 0, 0)),
            _const_spec(w.shape), _const_spec(b.shape), _const_spec(qn.shape), _const_spec(kvn.shape),
            _const_spec(lng.shape), _const_spec(lnb.shape), _const_spec(ws.shape), _const_spec(bs.shape),
            pl.BlockSpec((tm, LANES), lambda i: (i % tpb, 0)),
            pl.BlockSpec((tm, LANES), lambda i: (i % tpb, 0)),
        ],
        out_specs=[tok(d), tok(S5_WIDTH), tok(Q_LORA), tok(KV_LORA), tok(SGU_WIDTH), tok(LANES)],
        out_shape=[
            jax.ShapeDtypeStruct((t, d), BF16),
            jax.ShapeDtypeStruct((t, S5_WIDTH), F32),
            jax.ShapeDtypeStruct((t, Q_LORA), BF16),
            jax.ShapeDtypeStruct((t, KV_LORA), BF16),
            jax.ShapeDtypeStruct((t, SGU_WIDTH), BF16),
            jax.ShapeDtypeStruct((t, LANES), BF16),
        ],
        compiler_params=_params(("arbitrary",)),
        name="inproj",
    )(x2, ada, w, b, qn, kvn, lng, lnb, ws, bs, cosk, sink)


def _mla_proj_kernel(cq_ref, ckv_ref, krot_ref, aq_ref, sq_ref, wq_ref, wkt_ref, pkt_ref, wv_ref, ones_ref,
                     q_ref, kt_ref, v_ref):
    q = jnp.dot(cq_ref[...], wq_ref[...], preferred_element_type=F32)
    aq = aq_ref[...]
    sq = sq_ref[...]
    for h in range(MLA_HEADS):
        cols = slice(h * HEAD_PAD, (h + 1) * HEAD_PAD)
        qh = q[:, cols]
        q_ref[:, cols] = (qh * aq + pltpu.roll(qh, HEAD_PAD - QK_ROPE, 1) * sq).astype(BF16)
    kt = (lax.dot_general(wkt_ref[...], ckv_ref[...], NT_DIMS, preferred_element_type=F32)
          + lax.dot_general(pkt_ref[...], krot_ref[...], NT_DIMS, preferred_element_type=F32))
    kt_ref[0] = kt.astype(BF16)
    v_ref[...] = (jnp.dot(ckv_ref[...], wv_ref[...], preferred_element_type=F32) + ones_ref[...]).astype(BF16)


def _mla_proj_call(cqn, ckvn, krot, aq, sq, wq, wkt, pkt, wv, ones, batch, seq, tm):
    t = cqn.shape[0]
    tpb = seq // tm
    hw = MLA_HEADS * HEAD_PAD
    tok = lambda n: pl.BlockSpec((tm, n), lambda i: (i, 0))
    tab = pl.BlockSpec((tm, LANES), lambda i: (i % tpb, 0))
    return pl.pallas_call(
        _mla_proj_kernel,
        grid=(t // tm,),
        in_specs=[tok(Q_LORA), tok(KV_LORA), tok(LANES), tab, tab,
                  _const_spec(wq.shape), _const_spec(wkt.shape), _const_spec(pkt.shape), _const_spec(wv.shape),
                  _const_spec(ones.shape)],
        out_specs=[tok(hw), pl.BlockSpec((1, hw, tm), lambda i: (i // tpb, 0, i % tpb)), tok(hw)],
        out_shape=[
            jax.ShapeDtypeStruct((t, hw), BF16),
            jax.ShapeDtypeStruct((batch, hw, seq), BF16),
            jax.ShapeDtypeStruct((t, hw), BF16),
        ],
        compiler_params=_params(("arbitrary",)),
        name="mla_proj",
    )(cqn, ckvn, krot, aq, sq, wq, wkt, pkt, wv, ones)


def _flash_kernel(qi_ref, ki_ref, q_ref, kt_ref, v_ref, o_ref, m_sc, acc_sc):
    p_idx = pl.program_id(1)
    qi = qi_ref[p_idx]
    ki = ki_ref[p_idx]
    tq = q_ref.shape[0]
    tk = v_ref.shape[0]

    @pl.when(ki == 0)
    def _init():
        m_sc[...] = jnp.full(m_sc.shape, NEG_INF, F32)
        acc_sc[...] = jnp.zeros(acc_sc.shape, F32)

    def step(masked):
        if masked:
            qc = lax.broadcasted_iota(jnp.int32, (tq, tk), 0) // CHUNK
            kc = lax.broadcasted_iota(jnp.int32, (tq, tk), 1) // CHUNK
            visible = kc <= qc
        for h in range(MLA_HEADS):
            cols = slice(h * HEAD_PAD, (h + 1) * HEAD_PAD)
            s = jnp.dot(q_ref[:, cols], kt_ref[0, cols, :], preferred_element_type=F32)
            if masked:
                s = jnp.where(visible, s, NEG_INF)
            m_prev = m_sc[h]
            m_new = jnp.maximum(m_prev, jnp.max(s, axis=1, keepdims=True))
            alpha = jnp.exp(m_prev - m_new)
            p = jnp.exp(s - jnp.concatenate([m_new] * (tk // LANES), axis=1))
            m_sc[h] = m_new
            acc_sc[h] = alpha * acc_sc[h] + jnp.dot(p.astype(BF16), v_ref[:, cols], preferred_element_type=F32)

    @pl.when(ki < qi)
    def _full():
        step(False)

    @pl.when(ki == qi)
    def _diag():
        step(True)
        outs = []
        for h in range(MLA_HEADS):
            acc = acc_sc[h]
            outs.append(acc[:, :V_HEAD] / acc[:, V_HEAD:V_HEAD + 1])
        o_ref[...] = jnp.concatenate(outs, axis=1).astype(BF16)


def _flash_call(q, kt, v, batch, seq, tq):
    t, hw = q.shape
    nq = seq // tq
    pairs = [(a, b) for a in range(nq) for b in range(a + 1)]
    qi_tab = jnp.asarray([p[0] for p in pairs], jnp.int32)
    ki_tab = jnp.asarray([p[1] for p in pairs], jnp.int32)
    grid_spec = pltpu.PrefetchScalarGridSpec(
        num_scalar_prefetch=2,
        grid=(batch, len(pairs)),
        in_specs=[
            pl.BlockSpec((tq, hw), lambda b, p, qt, kt_: (b * nq + qt[p], 0)),
            pl.BlockSpec((1, hw, tq), lambda b, p, qt, kt_: (b, 0, kt_[p])),
            pl.BlockSpec((tq, hw), lambda b, p, qt, kt_: (b * nq + kt_[p], 0)),
        ],
        out_specs=pl.BlockSpec((tq, MLA_HEADS * V_HEAD), lambda b, p, qt, kt_: (b * nq + qt[p], 0)),
        scratch_shapes=[
            pltpu.VMEM((MLA_HEADS, tq, LANES), F32),
            pltpu.VMEM((MLA_HEADS, tq, HEAD_PAD), F32),
        ],
    )
    return pl.pallas_call(
        _flash_kernel,
        grid_spec=grid_spec,
        out_shape=jax.ShapeDtypeStruct((t, MLA_HEADS * V_HEAD), BF16),
        compiler_params=_params(("arbitrary", "arbitrary")),
        name="flash",
    )(qi_tab, ki_tab, q, kt, v)


def _s5_prep_kernel(lr_ref, li_ref, ldt_ref, bt_ref, btsw_ref, cw1_ref, cw2_ref, dsk_ref,
                    dt_ref, wst_ref, wot_ref, ws_sc):
    pw = 2 * S5_STATE
    lr = lr_ref[0]
    li = li_ref[0]
    dt = jnp.exp(ldt_ref[0])
    lane = lax.broadcasted_iota(jnp.int32, (1, pw), 1)
    sgn = jnp.where(lane < S5_STATE, -1.0, 1.0)
    nk = CHUNK + SUBLANES
    kk = lax.broadcasted_iota(jnp.int32, (nk, pw), 0).astype(F32)
    mag = jnp.exp(kk * (lr * dt))
    ang = kk * (li * dt)
    ar = mag * jnp.cos(ang)
    ai = mag * jnp.sin(ang)
    ais = ai * sgn
    den = lr * lr + li * li
    a1r = ar[1:2]
    a1i = ai[1:2]
    f_re = ((a1r - 1.0) * lr + a1i * li) / den
    f_im = (a1i * lr - (a1r - 1.0) * li) / den
    bt = bt_ref[0]
    btsw = btsw_ref[0]
    bb = f_re * bt + (f_im * sgn) * btsw
    bbsw = f_re * btsw - (f_im * sgn) * bt
    cw1 = cw1_ref[0]
    cw2 = cw2_ref[0]
    for s in range(CHUNK):
        k = CHUNK - 1 - s
        ws_sc[s * S5_GROUP:(s + 1) * S5_GROUP, :] = ar[k:k + 1] * bb + ais[k:k + 1] * bbsw
        wot_ref[0, s * S5_GROUP:(s + 1) * S5_GROUP, :] = (ar[s + 1:s + 2] * cw1 + ai[s + 1:s + 2] * cw2).astype(BF16)
    ws = ws_sc[...]
    wst_ref[0] = ws.T.astype(BF16)
    strip = lax.dot_general(cw1, ws, NT_DIMS, preferred_element_type=F32, precision=HI)
    row = lax.broadcasted_iota(jnp.int32, (S5_GROUP, S5_FLAT), 0)
    col = lax.broadcasted_iota(jnp.int32, (S5_GROUP, S5_FLAT), 1)
    dsk = jnp.concatenate([dsk_ref[0]] * (S5_FLAT // LANES), axis=1)
    strip = strip + jnp.where(col == row + (S5_FLAT - S5_GROUP), dsk, 0.0)
    strip = jnp.concatenate([strip, jnp.zeros((S5_GROUP, S5_BLK), F32)], axis=1)
    for d in range(S5_NSUB):
        for tt in range(S5_SUB):
            i0 = CHUNK - 1 - S5_SUB * d - tt
            dt_ref[0, d, tt * S5_GROUP:(tt + 1) * S5_GROUP, :] = (
                strip[:, S5_GROUP * i0:S5_GROUP * i0 + S5_BLK].astype(BF16))


def _s5_prep_call(lr2, li2, ldt2, bt, btsw, cw1, cw2, dsk):
    n = lr2.shape[0]
    pw = 2 * S5_STATE
    vec = pl.BlockSpec((1, 1, pw), lambda i: (i, 0, 0))
    mat = pl.BlockSpec((1, S5_GROUP, pw), lambda i: (i, 0, 0))
    return pl.pallas_call(
        _s5_prep_kernel,
        grid=(n,),
        in_specs=[vec, vec, vec, mat, mat, mat, mat, mat],
        out_specs=[
            pl.BlockSpec((1, S5_NSUB, S5_BLK, S5_BLK), lambda i: (i, 0, 0, 0)),
            pl.BlockSpec((1, pw, S5_FLAT), lambda i: (i, 0, 0)),
            pl.BlockSpec((1, S5_FLAT, pw), lambda i: (i, 0, 0)),
        ],
        out_shape=[
            jax.ShapeDtypeStruct((n, S5_NSUB, S5_BLK, S5_BLK), BF16),
            jax.ShapeDtypeStruct((n, pw, S5_FLAT), BF16),
            jax.ShapeDtypeStruct((n, S5_FLAT, pw), BF16),
        ],
        scratch_shapes=[pltpu.VMEM((S5_FLAT, pw), F32)],
        compiler_params=_params(("arbitrary",)),
        name="s5_prep",
    )(lr2, li2, ldt2, bt, btsw, cw1, cw2, dsk)


def _s5_kernel(u_ref, dt_ref, wst_ref, wot_ref, pa_ref, pb_ref, y_ref, ut_sc, yt_sc, *, nch):
    gg = pl.program_id(1)
    ncol = ut_sc.shape[2]
    pw = 2 * S5_STATE

    @pl.when(gg == 0)
    def _regroup():
        for s in range(CHUNK):
            ut_sc[s] = u_ref[pl.ds(s, ncol, stride=CHUNK), :].T.astype(BF16)

    row0 = pl.multiple_of(gg * S5_GROUP, S5_GROUP)
    u = ut_sc[:, pl.ds(row0, S5_GROUP), :].reshape(S5_FLAT, ncol)
    h = jnp.dot(wst_ref[0], u, preferred_element_type=F32)
    n_idx = lax.broadcasted_iota(jnp.int32, (pw, ncol), 1) % nch
    pa = pa_ref[0]
    pb = pb_ref[0]
    step = 1
    j = 0
    while step < nch:
        sh = jnp.where(n_idx >= step, pltpu.roll(h, step, 1), 0.0)
        h = h + pa[:, j:j + 1] * sh + pb[:, j:j + 1] * pltpu.roll(sh, S5_STATE, 0)
        step *= 2
        j += 1
    hprev = jnp.where(n_idx >= 1, pltpu.roll(h, 1, 1), 0.0).astype(BF16)
    for jj in range(S5_NSUB):
        acc = jnp.dot(wot_ref[0, jj * S5_BLK:(jj + 1) * S5_BLK, :], hprev, preferred_element_type=F32)
        for ii in range(jj + 1):
            acc = acc + jnp.dot(dt_ref[0, jj - ii], u[ii * S5_BLK:(ii + 1) * S5_BLK, :],
                                preferred_element_type=F32)
        yt_sc[jj * S5_SUB:(jj + 1) * S5_SUB, pl.ds(row0, S5_GROUP), :] = acc.reshape(S5_SUB, S5_GROUP, ncol)

    @pl.when(gg == S5_GPB - 1)
    def _emit():
        for tt in range(CHUNK):
            y_ref[pl.ds(tt, ncol, stride=CHUNK), :] = yt_sc[tt].T


def _s5_call(u5, dtb, wst, wot, pa, pb, layer, nch):
    t, width = u5.shape
    ncol = t // CHUNK
    nblk = width // LANES
    base = layer * S5_GROUPS
    grp = lambda shape: pl.BlockSpec((1,) + shape, lambda j, g: (base + j * S5_GPB + g,) + (0,) * len(shape))
    return pl.pallas_call(
        functools.partial(_s5_kernel, nch=nch),
        grid=(nblk, S5_GPB),
        in_specs=[
            pl.BlockSpec((t, LANES), lambda j, g: (0, j)),
            grp(dtb.shape[1:]), grp(wst.shape[1:]), grp(wot.shape[1:]), grp(pa.shape[1:]), grp(pb.shape[1:]),
        ],
        out_specs=pl.BlockSpec((t, LANES), lambda j, g: (0, j)),
        out_shape=jax.ShapeDtypeStruct((t, width), F32),
        scratch_shapes=[pltpu.VMEM((CHUNK, LANES, ncol), BF16), pltpu.VMEM((CHUNK, LANES, ncol), F32)],
        compiler_params=_params(("arbitrary", "arbitrary")),
        name="s5",
    )(u5, dtb, wst, wot, pa, pb)


def _merge_kernel(h_ref, yssm_ref, ymla_ref, ysgu_ref, x_ref, ada_ref,
                  wg_ref, bg_ref, wglu_ref, bglu_ref, wb_ref, wout_ref, lng_ref, lnb_ref,
                  o_ref, *, alpha):
    d = x_ref.shape[1]
    h = h_ref[...]
    z = _gelu(yssm_ref[...])
    y5 = (z * _sigmoid(jnp.dot(z.astype(BF16), wglu_ref[...], preferred_element_type=F32)
                       + bglu_ref[...])).astype(BF16)
    branches = (y5, ymla_ref[...], ysgu_ref[...])
    merged = None
    for i in range(N_BRANCH):
        cols = slice(i * d, (i + 1) * d)
        gate = _sigmoid(jnp.dot(h, wg_ref[:, cols], preferred_element_type=F32) + bg_ref[:, cols])
        term = gate * jnp.dot(branches[i], wb_ref[i], preferred_element_type=F32)
        merged = term if merged is None else merged + term
    y = jnp.dot(merged.astype(BF16), wout_ref[...], preferred_element_type=F32)
    g1 = ada_ref[0][2:3]
    o_ref[...] = _layer_norm(alpha * x_ref[...] + (1.0 + g1) * y, lng_ref[...], lnb_ref[...])


def _merge_call(h, yssm, ymla, ysgu, x2, ada, wg, bg, wglu, bglu, wb, wout, lng, lnb, seq, tm, alpha):
    t, d = x2.shape
    tpb = seq // tm
    tok = lambda n: pl.BlockSpec((tm, n), lambda i: (i, 0))
    consts = [wg, bg, wglu, bglu, wb, wout, lng, lnb]
    return pl.pallas_call(
        functools.partial(_merge_kernel, alpha=alpha),
        grid=(t // tm,),
        in_specs=[tok(d), tok(S5_WIDTH), tok(BRANCH_WIDTH), tok(BRANCH_WIDTH), tok(d),
                  pl.BlockSpec((1, 6, d), lambda i: (i // tpb, 0, 0))] + [_const_spec(a.shape) for a in consts],
        out_specs=tok(d),
        out_shape=jax.ShapeDtypeStruct((t, d), F32),
        compiler_params=_params(("arbitrary",)),
        name="merge",
    )(h, yssm, ymla, ysgu, x2, ada, *consts)


def _ffn_kernel(x_ref, ada_ref, wa_ref, wb_ref, wo_ref, lng_ref, lnb_ref, o_ref, *, alpha):
    x = x_ref[...]
    ada = ada_ref[0]
    h = (x * (1.0 + ada[4:5]) + ada[3:4]).astype(BF16)
    a = jnp.dot(h, wa_ref[...], preferred_element_type=F32)
    b = jnp.dot(h, wb_ref[...], preferred_element_type=F32)
    act = (a * _sigmoid(a) * b).astype(BF16)
    f = jnp.dot(act, wo_ref[...], preferred_element_type=F32)
    o_ref[...] = _layer_norm(alpha * x + (1.0 + ada[5:6]) * f, lng_ref[...], lnb_ref[...])


def _ffn_call(x2, ada, wa, wb, wo, lng, lnb, seq, tm, alpha):
    t, d = x2.shape
    tpb = seq // tm
    tok = pl.BlockSpec((tm, d), lambda i: (i, 0))
    consts = [wa, wb, wo, lng, lnb]
    return pl.pallas_call(
        functools.partial(_ffn_kernel, alpha=alpha),
        grid=(t // tm,),
        in_specs=[tok, pl.BlockSpec((1, 6, d), lambda i: (i // tpb, 0, 0))] + [_const_spec(a.shape) for a in consts],
        out_specs=tok,
        out_shape=jax.ShapeDtypeStruct((t, d), F32),
        compiler_params=_params(("arbitrary",)),
        name="ffn",
    )(x2, ada, *consts)


def _s5_prep_inputs(lam_re, lam_im, log_dt, b_re, b_im, c_re, c_im, dskip):
    depth, g, p = lam_re.shape
    n = depth * g
    dup = lambda a: jnp.concatenate([a, a], axis=-1).reshape(n, 1, 2 * p)
    lr2 = dup(lam_re)
    li2 = dup(lam_im)
    ldt2 = jnp.broadcast_to(log_dt[..., None], (depth, g, 2 * p)).reshape(n, 1, 2 * p)
    btr = b_re.transpose(0, 1, 3, 2).reshape(n, S5_GROUP, p)
    bti = b_im.transpose(0, 1, 3, 2).reshape(n, S5_GROUP, p)
    bt = jnp.concatenate([btr, bti], axis=-1)
    btsw = jnp.concatenate([bti, btr], axis=-1)
    cr = c_re.reshape(n, S5_GROUP, p)
    ci = c_im.reshape(n, S5_GROUP, p)
    cw1 = jnp.concatenate([cr, -ci], axis=-1)
    cw2 = jnp.concatenate([-ci, -cr], axis=-1)
    dsk = jnp.broadcast_to(dskip.reshape(n, S5_GROUP, 1), (n, S5_GROUP, LANES))
    return lr2, li2, ldt2, bt, btsw, cw1, cw2, dsk


def _s5_scan_multipliers(lam_re, lam_im, log_dt, nch):
    depth, g, p = lam_re.shape
    dt = jnp.exp(log_dt)[..., None, None]
    nsteps = max(1, (nch - 1).bit_length())
    k = jnp.asarray(CHUNK * (2.0 ** np.arange(nsteps)), F32)
    mag = jnp.exp(lam_re[..., None] * dt * k)
    ang = lam_im[..., None] * dt * k
    sr = (mag * jnp.cos(ang)).reshape(depth * g, p, nsteps)
    si = (mag * jnp.sin(ang)).reshape(depth * g, p, nsteps)
    pad = ((0, 0), (0, 0), (0, S5_SCAN_COLS - nsteps))
    pa = jnp.pad(jnp.concatenate([sr, sr], axis=1), pad)
    pb = jnp.pad(jnp.concatenate([-si, si], axis=1), pad)
    return pa, pb


def _rope_tables(seq, scale):
    inv_freq = 1.0 / (ROPE_THETA ** (jnp.arange(0, QK_ROPE, 2, dtype=F32) / QK_ROPE))
    ang = jnp.arange(seq, dtype=F32)[:, None] * inv_freq[None, :]
    cos = jnp.cos(ang)
    sin = jnp.sin(ang)
    zk = jnp.zeros((seq, LANES - QK_ROPE), F32)
    cosk = jnp.concatenate([cos, cos, zk], axis=1)
    sink = jnp.concatenate([-sin, sin, zk], axis=1)
    zq = jnp.zeros((seq, HEAD_PAD - QK_NOPE - QK_ROPE), F32)
    aq = jnp.concatenate([jnp.full((seq, QK_NOPE), scale, F32), scale * cos, scale * cos, zq], axis=1)
    sq = jnp.concatenate([jnp.zeros((seq, QK_NOPE), F32), -scale * sin, scale * sin, zq], axis=1)
    return cosk, sink, aq, sq


def _swap_halves(w):
    half = w.shape[-1] // 2
    return jnp.concatenate([w[..., half:], w[..., :half]], axis=-1)


def kernel(x, c, w_ada, b_ada, w_in, b_in, s5_lambda_re, s5_lambda_im, s5_log_dt, s5_b_re, s5_b_im, s5_c_re, s5_c_im, s5_d, s5_w_glu, s5_b_glu, mla_q_norm, mla_w_q_up, mla_kv_norm, mla_w_kv_up, sgu_ln_g, sgu_ln_b, sgu_w_s, sgu_b_s, w_branch, w_out, ln1_g, ln1_b, ffn_w_in, ffn_w_out, ln2_g, ln2_b):
    batch, seq, d = x.shape
    depth = w_ada.shape[0]
    t = batch * seq
    ff = ffn_w_out.shape[1]
    alpha = float((2 * depth) ** 0.25)
    tm = 512
    tq = 512
    nch = seq // CHUNK
    nbn = batch * nch
    assert seq % tm == 0 and seq % tq == 0 and tm % SGU_CHUNK == 0 and nbn % LANES == 0

    c_pad = jnp.pad(c, ((0, -batch % SUBLANES), (0, 0)))
    ada_all = _ada_call(c_pad, w_ada, b_ada)[:, :batch].reshape(depth, batch, 6, d)

    scale = (QK_NOPE + QK_ROPE) ** -0.5
    cosk, sink, aq, sq = _rope_tables(seq, scale)

    offs = np.cumsum([0, S5_WIDTH, Q_LORA, KV_LORA, QK_ROPE, SGU_WIDTH, SGU_WIDTH])
    o_u5, o_cq, o_ckv, o_kpe, o_usgu, o_vsgu, o_gate = [int(o) for o in offs]

    hw = MLA_HEADS * HEAD_PAD
    pos = np.arange(SGU_CHUNK) // CHUNK
    sgu_mask = jnp.asarray(pos[None, :] <= pos[:, None])
    ones_row = np.zeros((1, hw), np.float32)
    ones_row[0, np.arange(MLA_HEADS) * HEAD_PAD + V_HEAD] = 1.0
    ones_row = jnp.asarray(ones_row)
    place = np.zeros((LANES, hw), np.float32)
    for hh in range(MLA_HEADS):
        place[np.arange(QK_ROPE), hh * HEAD_PAD + QK_NOPE + np.arange(QK_ROPE)] = 1.0
    pkt = jnp.asarray(place.T, BF16)

    dtb, wst, wot = _s5_prep_call(*_s5_prep_inputs(s5_lambda_re, s5_lambda_im, s5_log_dt, s5_b_re, s5_b_im,
                                                   s5_c_re, s5_c_im, s5_d))
    pa, pb = _s5_scan_multipliers(s5_lambda_re, s5_lambda_im, s5_log_dt, nch)

    x2 = x.reshape(t, d)
    for l in range(depth):
        ada = ada_all[l]
        wl = w_in[l]
        bl = b_in[l]
        kpe_w = wl[:, o_kpe:o_kpe + QK_ROPE]
        kpe_b = bl[o_kpe:o_kpe + QK_ROPE]
        zpad_w = jnp.zeros((d, LANES - 2 * QK_ROPE), F32)
        zpad_b = jnp.zeros((LANES - 2 * QK_ROPE,), F32)
        w_a = jnp.concatenate([wl[:, o_u5:o_kpe], wl[:, o_usgu:o_gate], kpe_w, _swap_halves(kpe_w), zpad_w],
                              axis=1).astype(BF16)
        b_a = jnp.concatenate([bl[o_u5:o_kpe], bl[o_usgu:o_gate], kpe_b, _swap_halves(kpe_b), zpad_b])[None, :]
        ws_m = jnp.where(sgu_mask[None], sgu_w_s[l], 0.0).astype(BF16)
        bs_full = jnp.repeat(sgu_b_s[l].T, SGU_WIDTH // SGU_GROUPS, axis=1)

        h, u5, cqn, ckvn, ysgu, krot = _inproj_call(
            x2, ada, w_a, b_a, mla_q_norm[l][None, :], mla_kv_norm[l][None, :],
            sgu_ln_g[l][None, :], sgu_ln_b[l][None, :], ws_m, bs_full, cosk, sink, seq, tm)

        wq3 = mla_w_q_up[l].reshape(Q_LORA, MLA_HEADS, QK_NOPE + QK_ROPE)
        wq_pe = wq3[:, :, QK_NOPE:]
        wq = jnp.concatenate([wq3, _swap_halves(wq_pe)], axis=2).reshape(Q_LORA, hw).astype(BF16)
        wkv3 = mla_w_kv_up[l].reshape(KV_LORA, MLA_HEADS, QK_NOPE + V_HEAD)
        zhalf = jnp.zeros((KV_LORA, MLA_HEADS, HEAD_PAD - QK_NOPE), F32)
        wk = jnp.concatenate([wkv3[:, :, :QK_NOPE], zhalf], axis=2).reshape(KV_LORA, hw)
        wkt = wk.T.astype(BF16)
        wv = jnp.concatenate([wkv3[:, :, QK_NOPE:], zhalf], axis=2).reshape(KV_LORA, hw).astype(BF16)
        q, kt, v = _mla_proj_call(cqn, ckvn, krot, aq, sq, wq, wkt, pkt, wv, ones_row, batch, seq, tm)
        ymla = _flash_call(q, kt, v, batch, seq, tq)

        yssm = _s5_call(u5, dtb, wst, wot, pa, pb, l, nch)

        x2 = _merge_call(
            h, yssm, ymla, ysgu, x2, ada,
            wl[:, o_gate:].astype(BF16), bl[o_gate:][None, :],
            s5_w_glu[l].astype(BF16), s5_b_glu[l][None, :], w_branch[l].astype(BF16), w_out[l].astype(BF16),
            ln1_g[l][None, :], ln1_b[l][None, :], seq, tm, alpha)

        x2 = _ffn_call(x2, ada, ffn_w_in[l][:, :ff].astype(BF16), ffn_w_in[l][:, ff:].astype(BF16),
                       ffn_w_out[l].astype(BF16), ln2_g[l][None, :], ln2_b[l][None, :], seq, tm, alpha)
    return x2.reshape(batch, seq, d)
```

```python
import functools
import math

import numpy as np
import jax
import jax.numpy as jnp
from jax import lax
from jax.experimental import pallas as pl
from jax.experimental.pallas import tpu as pltpu

F32 = jnp.float32
BF16 = jnp.bfloat16

CHUNK = 64
S5_WIDTH = 512
S5_GROUP = 16
S5_GROUPS = S5_WIDTH // S5_GROUP
S5_STATE = 64
MLA_HEADS = 8
QK_NOPE = 64
QK_ROPE = 32
V_HEAD = 64
Q_LORA = 384
KV_LORA = 256
ROPE_THETA = 10000.0
SGU_WIDTH = 512
SGU_GROUPS = 4
SGU_CHUNK = 128
N_BRANCH = 3
BRANCH_WIDTH = 512
LN_EPS = 1e-5
RMS_EPS = 1e-6
NEG_INF = -1e30

LANES = 128
SUBLANES = 8
BF16_ROWS = 16
HEAD_PAD = 128
V_PAD = V_HEAD + BF16_ROWS
S5_SUB = 16
S5_NSUB = CHUNK // S5_SUB
S5_BLK = S5_SUB * S5_GROUP
S5_FLAT = CHUNK * S5_GROUP
S5_GPB = LANES // S5_GROUP
S5_SCAN_COLS = 8

OFF_U5 = 0
OFF_CQ = OFF_U5 + S5_WIDTH
OFF_CKV = OFF_CQ + Q_LORA
OFF_USGU = OFF_CKV + KV_LORA
OFF_VSGU = OFF_USGU + SGU_WIDTH
OFF_KPE = OFF_VSGU + SGU_WIDTH
NA = OFF_KPE + LANES

VMEM_LIMIT = 56 * 1024 * 1024

HI = lax.Precision.HIGHEST
NT_DIMS = (((1,), (1,)), ((), ()))
LOG2E = 1.4426950408889634


def _gelu(x):
    return 0.5 * x * (1.0 + jnp.tanh(0.7978845608028654 * (x + 0.044715 * (x * x * x))))


def _sigmoid(x):
    return 0.5 * jnp.tanh(0.5 * x) + 0.5


def _layer_norm(r, g, b):
    mu = jnp.mean(r, axis=-1, keepdims=True)
    rc = r - mu
    var = jnp.mean(rc * rc, axis=-1, keepdims=True)
    return rc * lax.rsqrt(var + LN_EPS) * g + b


def _const_spec(shape):
    nd = len(shape)
    return pl.BlockSpec(shape, lambda *_: (0,) * nd, pipeline_mode=pl.Buffered(1))


def _layer_spec(arr, layer, col_block=None):
    shape = (1,) + arr.shape[1:]
    idx = (layer,) + (0,) * (arr.ndim - 1)
    if col_block is not None:
        width, j = col_block
        shape = shape[:-1] + (width,)
        idx = idx[:-1] + (j,)
    return pl.BlockSpec(shape, lambda *_: idx, pipeline_mode=pl.Buffered(1))


def _params(sem):
    return pltpu.CompilerParams(dimension_semantics=sem, vmem_limit_bytes=VMEM_LIMIT)


def _ada_kernel(c_ref, w_ref, b_ref, o_ref):
    c = c_ref[...]
    ca = c * _sigmoid(c)
    o_ref[0] = jnp.dot(ca, w_ref[0], preferred_element_type=F32, precision=HI) + b_ref[0]


def _ada_call(c_pad, w_ada, b_ada):
    depth, d, n6 = w_ada.shape
    tn = n6 // 4
    rows = c_pad.shape[0]
    return pl.pallas_call(
        _ada_kernel,
        grid=(depth, n6 // tn),
        in_specs=[
            pl.BlockSpec((rows, d), lambda l, j: (0, 0)),
            pl.BlockSpec((1, d, tn), lambda l, j: (l, 0, j)),
            pl.BlockSpec((1, 1, tn), lambda l, j: (l, 0, j)),
        ],
        out_specs=pl.BlockSpec((1, rows, tn), lambda l, j: (l, 0, j)),
        out_shape=jax.ShapeDtypeStruct((depth, rows, n6), F32),
        compiler_params=_params(("arbitrary", "arbitrary")),
        name="ada",
    )(c_pad, w_ada, b_ada.reshape(depth, 1, n6))


def _inproj_kernel(x_ref, ada_ref, w_ref, b_ref, qn_ref, kvn_ref, lng_ref, lnb_ref, ws_ref, bs_ref,
                   cosk_ref, sink_ref,
                   h_ref, u5_ref, cq_ref, ckv_ref, ysgu_ref, krot_ref):
    tm = x_ref.shape[0]
    ada = ada_ref[0, 0]
    h = (x_ref[...] * (1.0 + ada[1:2]) + ada[0:1]).astype(BF16)
    h_ref[...] = h
    acc = jnp.dot(h, w_ref[0], preferred_element_type=F32) + b_ref[0]

    u5_ref[...] = acc[:, OFF_U5:OFF_U5 + S5_WIDTH]

    cq = acc[:, OFF_CQ:OFF_CQ + Q_LORA]
    cq_ref[...] = (cq * lax.rsqrt(jnp.mean(cq * cq, axis=-1, keepdims=True) + RMS_EPS) * qn_ref[0]).astype(BF16)
    ckv = acc[:, OFF_CKV:OFF_CKV + KV_LORA]
    ckv_ref[...] = (ckv * lax.rsqrt(jnp.mean(ckv * ckv, axis=-1, keepdims=True) + RMS_EPS)
                    * kvn_ref[0]).astype(BF16)

    kb = acc[:, OFF_KPE:OFF_KPE + LANES]
    krot = kb * cosk_ref[...] + pltpu.roll(kb, LANES - QK_ROPE, 1) * sink_ref[...]
    krot_ref[...] = krot.astype(BF16)

    us = _gelu(acc[:, OFF_USGU:OFF_USGU + SGU_WIDTH])
    vn = _layer_norm(_gelu(acc[:, OFF_VSGU:OFF_VSGU + SGU_WIDTH]), lng_ref[0], lnb_ref[0]).astype(BF16)
    gw = SGU_WIDTH // SGU_GROUPS
    for r in range(tm // SGU_CHUNK):
        rows = slice(r * SGU_CHUNK, (r + 1) * SGU_CHUNK)
        parts = [jnp.dot(ws_ref[0, g], vn[rows, g * gw:(g + 1) * gw], preferred_element_type=F32)
                 for g in range(SGU_GROUPS)]
        mixed = jnp.concatenate(parts, axis=1) + bs_ref[0]
        ysgu_ref[rows, :] = (us[rows, :] * mixed).astype(BF16)


def _inproj_call(x2, ada_all, layer, w, b, qn, kvn, lng, lnb, ws, bs, cosk, sink, seq, tm):
    t, d = x2.shape
    tpb = seq // tm
    tok = lambda n: pl.BlockSpec((tm, n), lambda i: (i, 0))
    consts = [w, b, qn, kvn, lng, lnb, ws, bs]
    return pl.pallas_call(
        _inproj_kernel,
        grid=(t // tm,),
        in_specs=[
            tok(d),
            pl.BlockSpec((1, 1, 6, d), lambda i: (layer, i // tpb, 0, 0)),
        ] + [_layer_spec(a, layer) for a in consts] + [
            pl.BlockSpec((tm, LANES), lambda i: (i % tpb, 0)),
            pl.BlockSpec((tm, LANES), lambda i: (i % tpb, 0)),
        ],
        out_specs=[tok(d), tok(S5_WIDTH), tok(Q_LORA), tok(KV_LORA), tok(SGU_WIDTH), tok(LANES)],
        out_shape=[
            jax.ShapeDtypeStruct((t, d), BF16),
            jax.ShapeDtypeStruct((t, S5_WIDTH), F32),
            jax.ShapeDtypeStruct((t, Q_LORA), BF16),
            jax.ShapeDtypeStruct((t, KV_LORA), BF16),
            jax.ShapeDtypeStruct((t, SGU_WIDTH), BF16),
            jax.ShapeDtypeStruct((t, LANES), BF16),
        ],
        compiler_params=_params(("arbitrary",)),
        name="inproj",
    )(x2, ada_all, *consts, cosk, sink)


def _mla_proj_kernel(cq_ref, ckv_ref, krot_ref, aqt_ref, sqt_ref, wqt_ref, wk_ref, pk_ref, wvt_ref, ones_ref,
                     k_ref, qt_ref, vt_ref):
    ckv = ckv_ref[...]
    k_ref[...] = (jnp.dot(ckv, wk_ref[0], preferred_element_type=F32)
                  + jnp.dot(krot_ref[...], pk_ref[...], preferred_element_type=F32)).astype(BF16)
    qt = lax.dot_general(wqt_ref[0], cq_ref[...], NT_DIMS, preferred_element_type=F32)
    aqt = aqt_ref[...]
    sqt = sqt_ref[...]
    for h in range(MLA_HEADS):
        rows = slice(h * HEAD_PAD, (h + 1) * HEAD_PAD)
        qh = qt[rows, :]
        qt_ref[0, rows, :] = (qh * aqt + pltpu.roll(qh, HEAD_PAD - QK_ROPE, 0) * sqt).astype(BF16)
    vt = lax.dot_general(wvt_ref[0], ckv, NT_DIMS, preferred_element_type=F32) + ones_ref[...]
    vt_ref[0] = vt.astype(BF16)


def _mla_proj_call(cqn, ckvn, krot, aqt, sqt, wqt, wk, pk, wvt, ones, layer, batch, seq, tm):
    t = cqn.shape[0]
    tpb = seq // tm
    hw = MLA_HEADS * HEAD_PAD
    vw = MLA_HEADS * V_PAD
    tok = lambda n: pl.BlockSpec((tm, n), lambda i: (i, 0))
    tab = pl.BlockSpec((HEAD_PAD, tm), lambda i: (0, i % tpb))
    return pl.pallas_call(
        _mla_proj_kernel,
        grid=(t // tm,),
        in_specs=[tok(Q_LORA), tok(KV_LORA), tok(LANES), tab, tab,
                  _layer_spec(wqt, layer), _layer_spec(wk, layer), _const_spec(pk.shape), _layer_spec(wvt, layer),
                  _const_spec(ones.shape)],
        out_specs=[tok(hw),
                   pl.BlockSpec((1, hw, tm), lambda i: (i // tpb, 0, i % tpb)),
                   pl.BlockSpec((1, vw, tm), lambda i: (i // tpb, 0, i % tpb))],
        out_shape=[
            jax.ShapeDtypeStruct((t, hw), BF16),
            jax.ShapeDtypeStruct((batch, hw, seq), BF16),
            jax.ShapeDtypeStruct((batch, vw, seq), BF16),
        ],
        compiler_params=_params(("arbitrary",)),
        name="mla_proj",
    )(cqn, ckvn, krot, aqt, sqt, wqt, wk, pk, wvt, ones)


def _flash_kernel(qi_ref, ki_ref, k_ref, qt_ref, vt_ref, o_ref, m_sc, acc_sc):
    p_idx = pl.program_id(1)
    qi = qi_ref[p_idx]
    ki = ki_ref[p_idx]
    tk = k_ref.shape[0]
    tq = qt_ref.shape[2]

    @pl.when(ki == 0)
    def _init():
        m_sc[...] = jnp.full(m_sc.shape, NEG_INF, F32)
        acc_sc[...] = jnp.zeros(acc_sc.shape, F32)

    def step(masked):
        if masked:
            kc = lax.broadcasted_iota(jnp.int32, (tk, tq), 0) // CHUNK
            qc = lax.broadcasted_iota(jnp.int32, (tk, tq), 1) // CHUNK
            visible = kc <= qc

        def scores(h):
            cols = slice(h * HEAD_PAD, (h + 1) * HEAD_PAD)
            st = jnp.dot(k_ref[:, cols], qt_ref[0, cols, :], preferred_element_type=F32)
            if masked:
                st = jnp.where(visible, st, NEG_INF)
            m_prev = m_sc[h]
            m_new = jnp.maximum(m_prev, jnp.max(st, axis=0, keepdims=True))
            m_sc[h] = m_new
            return st, m_new, jnp.exp2(m_prev - m_new)

        def probs(st, m_new):
            return jnp.exp2(st - m_new[0:1]).astype(BF16)

        def accumulate(h, pt, alpha):
            acc_sc[h] = alpha[0:1] * acc_sc[h] + jnp.dot(vt_ref[0, h * V_PAD:(h + 1) * V_PAD, :], pt,
                                                        preferred_element_type=F32)

        nh = MLA_HEADS
        sc = {0: scores(0), 1: scores(1)}
        pr = {0: probs(sc[0][0], sc[0][1])}
        for h in range(nh):
            if h + 2 < nh:
                sc[h + 2] = scores(h + 2)
            if h + 1 < nh:
                pr[h + 1] = probs(sc[h + 1][0], sc[h + 1][1])
            accumulate(h, pr.pop(h), sc.pop(h)[2])

    @pl.when(ki < qi)
    def _full():
        step(False)

    @pl.when(ki == qi)
    def _diag():
        step(True)
        outs = []
        for h in range(MLA_HEADS):
            acc = acc_sc[h]
            outs.append(acc[:V_HEAD] / acc[V_HEAD:V_HEAD + 1])
        o_ref[...] = jnp.concatenate(outs, axis=0).T.astype(BF16)


def _flash_call(k, qt, vt, batch, seq, tq):
    t, hw = k.shape
    vw = vt.shape[1]
    nq = seq // tq
    pairs = [(a, b) for a in range(nq) for b in range(a + 1)]
    qi_tab = jnp.asarray([p[0] for p in pairs], jnp.int32)
    ki_tab = jnp.asarray([p[1] for p in pairs], jnp.int32)
    grid_spec = pltpu.PrefetchScalarGridSpec(
        num_scalar_prefetch=2,
        grid=(batch, len(pairs)),
        in_specs=[
            pl.BlockSpec((tq, hw), lambda b, p, qt_, kt_: (b * nq + kt_[p], 0)),
            pl.BlockSpec((1, hw, tq), lambda b, p, qt_, kt_: (b, 0, qt_[p])),
            pl.BlockSpec((1, vw, tq), lambda b, p, qt_, kt_: (b, 0, kt_[p])),
        ],
        out_specs=pl.BlockSpec((tq, MLA_HEADS * V_HEAD), lambda b, p, qt_, kt_: (b * nq + qt_[p], 0)),
        scratch_shapes=[
            pltpu.VMEM((MLA_HEADS, SUBLANES, tq), F32),
            pltpu.VMEM((MLA_HEADS, V_PAD, tq), F32),
        ],
    )
    return pl.pallas_call(
        _flash_kernel,
        grid_spec=grid_spec,
        out_shape=jax.ShapeDtypeStruct((t, MLA_HEADS * V_HEAD), BF16),
        compiler_params=_params(("arbitrary", "arbitrary")),
        name="flash",
    )(qi_tab, ki_tab, k, qt, vt)


def _s5_prep_kernel(lr_ref, li_ref, ldt_ref, bt_ref, btsw_ref, cw1_ref, cw2_ref, dsk_ref,
                    dt_ref, wst_ref, wot_ref, ws_sc):
    pw = 2 * S5_STATE
    lr = lr_ref[0]
    li = li_ref[0]
    dt = jnp.exp(ldt_ref[0])
    lane = lax.broadcasted_iota(jnp.int32, (1, pw), 1)
    sgn = jnp.where(lane < S5_STATE, -1.0, 1.0)
    nk = CHUNK + SUBLANES
    kk = lax.broadcasted_iota(jnp.int32, (nk, pw), 0).astype(F32)
    mag = jnp.exp(kk * (lr * dt))
    ang = kk * (li * dt)
    ar = mag * jnp.cos(ang)
    ai = mag * jnp.sin(ang)
    ais = ai * sgn
    den = lr * lr + li * li
    a1r = ar[1:2]
    a1i = ai[1:2]
    f_re = ((a1r - 1.0) * lr + a1i * li) / den
    f_im = (a1i * lr - (a1r - 1.0) * li) / den
    bt = bt_ref[0]
    btsw = btsw_ref[0]
    bb = f_re * bt + (f_im * sgn) * btsw
    bbsw = f_re * btsw - (f_im * sgn) * bt
    cw1 = cw1_ref[0]
    cw2 = cw2_ref[0]
    for s in range(CHUNK):
        k = CHUNK - 1 - s
        ws_sc[s * S5_GROUP:(s + 1) * S5_GROUP, :] = ar[k:k + 1] * bb + ais[k:k + 1] * bbsw
        wot_ref[0, s * S5_GROUP:(s + 1) * S5_GROUP, :] = (ar[s + 1:s + 2] * cw1 + ai[s + 1:s + 2] * cw2).astype(BF16)
    ws = ws_sc[...]
    wst_ref[0] = ws.T.astype(BF16)
    strip = lax.dot_general(cw1, ws, NT_DIMS, preferred_element_type=F32, precision=HI)
    row = lax.broadcasted_iota(jnp.int32, (S5_GROUP, S5_FLAT), 0)
    col = lax.broadcasted_iota(jnp.int32, (S5_GROUP, S5_FLAT), 1)
    dsk = jnp.concatenate([dsk_ref[0]] * (S5_FLAT // LANES), axis=1)
    strip = strip + jnp.where(col == row + (S5_FLAT - S5_GROUP), dsk, 0.0)
    strip = jnp.concatenate([strip, jnp.zeros((S5_GROUP, S5_BLK), F32)], axis=1)
    for d in range(S5_NSUB):
        for tt in range(S5_SUB):
            i0 = CHUNK - 1 - S5_SUB * d - tt
            dt_ref[0, d, tt * S5_GROUP:(tt + 1) * S5_GROUP, :] = (
                strip[:, S5_GROUP * i0:S5_GROUP * i0 + S5_BLK].astype(BF16))


def _s5_prep_call(lr2, li2, ldt2, bt, btsw, cw1, cw2, dsk):
    n = lr2.shape[0]
    pw = 2 * S5_STATE
    vec = pl.BlockSpec((1, 1, pw), lambda i: (i, 0, 0))
    mat = pl.BlockSpec((1, S5_GROUP, pw), lambda i: (i, 0, 0))
    return pl.pallas_call(
        _s5_prep_kernel,
        grid=(n,),
        in_specs=[vec, vec, vec, mat, mat, mat, mat, mat],
        out_specs=[
            pl.BlockSpec((1, S5_NSUB, S5_BLK, S5_BLK), lambda i: (i, 0, 0, 0)),
            pl.BlockSpec((1, pw, S5_FLAT), lambda i: (i, 0, 0)),
            pl.BlockSpec((1, S5_FLAT, pw), lambda i: (i, 0, 0)),
        ],
        out_shape=[
            jax.ShapeDtypeStruct((n, S5_NSUB, S5_BLK, S5_BLK), BF16),
            jax.ShapeDtypeStruct((n, pw, S5_FLAT), BF16),
            jax.ShapeDtypeStruct((n, S5_FLAT, pw), BF16),
        ],
        scratch_shapes=[pltpu.VMEM((S5_FLAT, pw), F32)],
        compiler_params=_params(("arbitrary",)),
        name="s5_prep",
    )(lr2, li2, ldt2, bt, btsw, cw1, cw2, dsk)


def _s5_kernel(u_ref, dt_ref, wst_ref, wot_ref, pa_ref, pb_ref, y_ref, ut_sc, yt_sc, *, nch):
    gg = pl.program_id(1)
    ncol = ut_sc.shape[2]
    pw = 2 * S5_STATE

    @pl.when(gg == 0)
    def _regroup():
        for s in range(CHUNK):
            ut_sc[s] = u_ref[pl.ds(s, ncol, stride=CHUNK), :].T.astype(BF16)

    row0 = pl.multiple_of(gg * S5_GROUP, S5_GROUP)
    u = ut_sc[:, pl.ds(row0, S5_GROUP), :].reshape(S5_FLAT, ncol)
    h = jnp.dot(wst_ref[0], u, preferred_element_type=F32)
    n_idx = lax.broadcasted_iota(jnp.int32, (pw, ncol), 1) % nch
    pa = pa_ref[0]
    pb = pb_ref[0]
    step = 1
    j = 0
    while step < nch:
        sh = jnp.where(n_idx >= step, pltpu.roll(h, step, 1), 0.0)
        h = h + pa[:, j:j + 1] * sh + pb[:, j:j + 1] * pltpu.roll(sh, S5_STATE, 0)
        step *= 2
        j += 1
    hprev = jnp.where(n_idx >= 1, pltpu.roll(h, 1, 1), 0.0).astype(BF16)
    for jj in range(S5_NSUB):
        acc = jnp.dot(wot_ref[0, jj * S5_BLK:(jj + 1) * S5_BLK, :], hprev, preferred_element_type=F32)
        for ii in range(jj + 1):
            acc = acc + jnp.dot(dt_ref[0, jj - ii], u[ii * S5_BLK:(ii + 1) * S5_BLK, :],
                                preferred_element_type=F32)
        yt_sc[jj * S5_SUB:(jj + 1) * S5_SUB, pl.ds(row0, S5_GROUP), :] = acc.reshape(S5_SUB, S5_GROUP, ncol)

    @pl.when(gg == S5_GPB - 1)
    def _emit():
        for tt in range(CHUNK):
            y_ref[pl.ds(tt, ncol, stride=CHUNK), :] = yt_sc[tt].T


def _s5_call(u5, dtb, wst, wot, pa, pb, layer, nch):
    t, width = u5.shape
    ncol = t // CHUNK
    nblk = width // LANES
    base = layer * S5_GROUPS
    grp = lambda shape: pl.BlockSpec((1,) + shape, lambda j, g: (base + j * S5_GPB + g,) + (0,) * len(shape))
    return pl.pallas_call(
        functools.partial(_s5_kernel, nch=nch),
        grid=(nblk, S5_GPB),
        in_specs=[
            pl.BlockSpec((t, LANES), lambda j, g: (0, j)),
            grp(dtb.shape[1:]), grp(wst.shape[1:]), grp(wot.shape[1:]), grp(pa.shape[1:]), grp(pb.shape[1:]),
        ],
        out_specs=pl.BlockSpec((t, LANES), lambda j, g: (0, j)),
        out_shape=jax.ShapeDtypeStruct((t, width), F32),
        scratch_shapes=[pltpu.VMEM((CHUNK, LANES, ncol), BF16), pltpu.VMEM((CHUNK, LANES, ncol), F32)],
        compiler_params=_params(("arbitrary", "arbitrary")),
        name="s5",
    )(u5, dtb, wst, wot, pa, pb)


def _merge_kernel(h_ref, yssm_ref, ymla_ref, ysgu_ref, x_ref, ada_ref,
                  wg_ref, bg_ref, wglu_ref, bglu_ref, wb_ref, wout_ref, lng_ref, lnb_ref,
                  o_ref, *, alpha):
    d = x_ref.shape[1]
    h = h_ref[...]
    z = _gelu(yssm_ref[...])
    y5 = (z * _sigmoid(jnp.dot(z.astype(BF16), wglu_ref[0], preferred_element_type=F32)
                       + bglu_ref[0])).astype(BF16)
    branches = (y5, ymla_ref[...], ysgu_ref[...])
    merged = None
    for i in range(N_BRANCH):
        cols = slice(i * d, (i + 1) * d)
        gate = _sigmoid(jnp.dot(h, wg_ref[0, :, cols], preferred_element_type=F32) + bg_ref[0, :, cols])
        term = gate * jnp.dot(branches[i], wb_ref[0, i], preferred_element_type=F32)
        merged = term if merged is None else merged + term
    y = jnp.dot(merged.astype(BF16), wout_ref[0], preferred_element_type=F32)
    g1 = ada_ref[0, 0][2:3]
    o_ref[...] = _layer_norm(alpha * x_ref[...] + (1.0 + g1) * y, lng_ref[0], lnb_ref[0])


def _merge_call(h, yssm, ymla, ysgu, x2, ada_all, layer, wg, bg, wglu, bglu, wb, wout, lng, lnb, seq, tm, alpha):
    t, d = x2.shape
    tpb = seq // tm
    tok = lambda n: pl.BlockSpec((tm, n), lambda i: (i, 0))
    consts = [wg, bg, wglu, bglu, wb, wout, lng, lnb]
    return pl.pallas_call(
        functools.partial(_merge_kernel, alpha=alpha),
        grid=(t // tm,),
        in_specs=[tok(d), tok(S5_WIDTH), tok(BRANCH_WIDTH), tok(BRANCH_WIDTH), tok(d),
                  pl.BlockSpec((1, 1, 6, d), lambda i: (layer, i // tpb, 0, 0))]
                 + [_layer_spec(a, layer) for a in consts],
        out_specs=tok(d),
        out_shape=jax.ShapeDtypeStruct((t, d), F32),
        compiler_params=_params(("arbitrary",)),
        name="merge",
    )(h, yssm, ymla, ysgu, x2, ada_all, *consts)


def _ffn_kernel(x_ref, ada_ref, wa_ref, wb_ref, wo_ref, lng_ref, lnb_ref, o_ref, *, alpha):
    x = x_ref[...]
    ada = ada_ref[0, 0]
    h = (x * (1.0 + ada[4:5]) + ada[3:4]).astype(BF16)
    a = jnp.dot(h, wa_ref[0], preferred_element_type=F32)
    b = jnp.dot(h, wb_ref[0], preferred_element_type=F32)
    act = (a * _sigmoid(a) * b).astype(BF16)
    f = jnp.dot(act, wo_ref[0], preferred_element_type=F32)
    o_ref[...] = _layer_norm(alpha * x + (1.0 + ada[5:6]) * f, lng_ref[0], lnb_ref[0])


def _ffn_call(x2, ada_all, layer, w_in, w_out, lng, lnb, seq, tm, alpha):
    t, d = x2.shape
    ff = w_out.shape[1]
    tpb = seq // tm
    tok = pl.BlockSpec((tm, d), lambda i: (i, 0))
    return pl.pallas_call(
        functools.partial(_ffn_kernel, alpha=alpha),
        grid=(t // tm,),
        in_specs=[tok, pl.BlockSpec((1, 1, 6, d), lambda i: (layer, i // tpb, 0, 0)),
                  _layer_spec(w_in, layer, (ff, 0)), _layer_spec(w_in, layer, (ff, 1)), _layer_spec(w_out, layer),
                  _layer_spec(lng, layer), _layer_spec(lnb, layer)],
        out_specs=tok,
        out_shape=jax.ShapeDtypeStruct((t, d), F32),
        compiler_params=_params(("arbitrary",)),
        name="ffn",
    )(x2, ada_all, w_in, w_in, w_out, lng, lnb)


def _s5_prep_inputs(lam_re, lam_im, log_dt, b_re, b_im, c_re, c_im, dskip):
    depth, g, p = lam_re.shape
    n = depth * g
    dup = lambda a: jnp.concatenate([a, a], axis=-1).reshape(n, 1, 2 * p)
    lr2 = dup(lam_re)
    li2 = dup(lam_im)
    ldt2 = jnp.broadcast_to(log_dt[..., None], (depth, g, 2 * p)).reshape(n, 1, 2 * p)
    btr = b_re.transpose(0, 1, 3, 2).reshape(n, S5_GROUP, p)
    bti = b_im.transpose(0, 1, 3, 2).reshape(n, S5_GROUP, p)
    bt = jnp.concatenate([btr, bti], axis=-1)
    btsw = jnp.concatenate([bti, btr], axis=-1)
    cr = c_re.reshape(n, S5_GROUP, p)
    ci = c_im.reshape(n, S5_GROUP, p)
    cw1 = jnp.concatenate([cr, -ci], axis=-1)
    cw2 = jnp.concatenate([-ci, -cr], axis=-1)
    dsk = jnp.broadcast_to(dskip.reshape(n, S5_GROUP, 1), (n, S5_GROUP, LANES))
    return lr2, li2, ldt2, bt, btsw, cw1, cw2, dsk


def _s5_scan_multipliers(lam_re, lam_im, log_dt, nch):
    depth, g, p = lam_re.shape
    dt = jnp.exp(log_dt)[..., None, None]
    nsteps = max(1, (nch - 1).bit_length())
    k = jnp.asarray(CHUNK * (2.0 ** np.arange(nsteps)), F32)
    mag = jnp.exp(lam_re[..., None] * dt * k)
    ang = lam_im[..., None] * dt * k
    sr = (mag * jnp.cos(ang)).reshape(depth * g, p, nsteps)
    si = (mag * jnp.sin(ang)).reshape(depth * g, p, nsteps)
    pad = ((0, 0), (0, 0), (0, S5_SCAN_COLS - nsteps))
    pa = jnp.pad(jnp.concatenate([sr, sr], axis=1), pad)
    pb = jnp.pad(jnp.concatenate([-si, si], axis=1), pad)
    return pa, pb


def _rope_tables(seq, scale):
    inv_freq = 1.0 / (ROPE_THETA ** (jnp.arange(0, QK_ROPE, 2, dtype=F32) / QK_ROPE))
    ang = jnp.arange(seq, dtype=F32)[:, None] * inv_freq[None, :]
    cos = jnp.cos(ang)
    sin = jnp.sin(ang)
    zk = jnp.zeros((seq, LANES - QK_ROPE), F32)
    cosk = jnp.concatenate([cos, cos, zk], axis=1)
    sink = jnp.concatenate([-sin, sin, zk], axis=1)
    zq = jnp.zeros((seq, HEAD_PAD - QK_NOPE - QK_ROPE), F32)
    aq = jnp.concatenate([jnp.full((seq, QK_NOPE), scale, F32), scale * cos, scale * cos, zq], axis=1)
    sq = jnp.concatenate([jnp.zeros((seq, QK_NOPE), F32), -scale * sin, scale * sin, zq], axis=1)
    return cosk, sink, aq.T, sq.T


def _swap_halves(w):
    half = w.shape[-1] // 2
    return jnp.concatenate([w[..., half:], w[..., :half]], axis=-1)


def kernel(x, c, w_ada, b_ada, w_in, b_in, s5_lambda_re, s5_lambda_im, s5_log_dt, s5_b_re, s5_b_im, s5_c_re, s5_c_im, s5_d, s5_w_glu, s5_b_glu, mla_q_norm, mla_w_q_up, mla_kv_norm, mla_w_kv_up, sgu_ln_g, sgu_ln_b, sgu_w_s, sgu_b_s, w_branch, w_out, ln1_g, ln1_b, ffn_w_in, ffn_w_out, ln2_g, ln2_b):
    batch, seq, d = x.shape
    depth = w_ada.shape[0]
    t = batch * seq
    alpha = float((2 * depth) ** 0.25)
    tm = 512
    tq = 512
    nch = seq // CHUNK
    assert seq % tm == 0 and seq % tq == 0 and tm % SGU_CHUNK == 0 and (batch * nch) % LANES == 0

    c_pad = jnp.pad(c, ((0, -batch % SUBLANES), (0, 0)))
    ada_all = _ada_call(c_pad, w_ada, b_ada)[:, :batch].reshape(depth, batch, 6, d)

    scale = LOG2E * (QK_NOPE + QK_ROPE) ** -0.5
    cosk, sink, aqt, sqt = _rope_tables(seq, scale)

    offs = np.cumsum([0, S5_WIDTH, Q_LORA, KV_LORA, QK_ROPE, SGU_WIDTH, SGU_WIDTH])
    o_u5, o_cq, o_ckv, o_kpe, o_usgu, o_vsgu, o_gate = [int(o) for o in offs]
    row = lambda a: a[:, None, :]

    kpe_w = w_in[:, :, o_kpe:o_kpe + QK_ROPE]
    kpe_b = b_in[:, o_kpe:o_kpe + QK_ROPE]
    w_a = jnp.concatenate([w_in[:, :, o_u5:o_kpe], w_in[:, :, o_usgu:o_gate], kpe_w, _swap_halves(kpe_w),
                           jnp.zeros((depth, d, LANES - 2 * QK_ROPE), F32)], axis=2).astype(BF16)
    b_a = row(jnp.concatenate([b_in[:, o_u5:o_kpe], b_in[:, o_usgu:o_gate], kpe_b, _swap_halves(kpe_b),
                               jnp.zeros((depth, LANES - 2 * QK_ROPE), F32)], axis=1))
    w_g = w_in[:, :, o_gate:].astype(BF16)
    b_g = row(b_in[:, o_gate:])
    pos = np.arange(SGU_CHUNK) // CHUNK
    sgu_mask = jnp.asarray(pos[None, :] <= pos[:, None])
    ws_m = jnp.where(sgu_mask[None, None], sgu_w_s, 0.0).astype(BF16)
    bs_full = jnp.repeat(sgu_b_s.transpose(0, 2, 1), SGU_WIDTH // SGU_GROUPS, axis=2)

    hw = MLA_HEADS * HEAD_PAD
    vw = MLA_HEADS * V_PAD
    wq4 = mla_w_q_up.reshape(depth, Q_LORA, MLA_HEADS, QK_NOPE + QK_ROPE)
    wqt = (jnp.concatenate([wq4, _swap_halves(wq4[..., QK_NOPE:])], axis=3)
           .reshape(depth, Q_LORA, hw).transpose(0, 2, 1).astype(BF16))
    wkv4 = mla_w_kv_up.reshape(depth, KV_LORA, MLA_HEADS, QK_NOPE + V_HEAD)
    wk = jnp.concatenate([wkv4[..., :QK_NOPE], jnp.zeros((depth, KV_LORA, MLA_HEADS, HEAD_PAD - QK_NOPE), F32)],
                         axis=3).reshape(depth, KV_LORA, hw).astype(BF16)
    wvt = (jnp.concatenate([wkv4[..., QK_NOPE:], jnp.zeros((depth, KV_LORA, MLA_HEADS, V_PAD - V_HEAD), F32)],
                           axis=3).reshape(depth, KV_LORA, vw).transpose(0, 2, 1).astype(BF16))
    place = np.zeros((LANES, hw), np.float32)
    for hh in range(MLA_HEADS):
        place[np.arange(QK_ROPE), hh * HEAD_PAD + QK_NOPE + np.arange(QK_ROPE)] = 1.0
    pk = jnp.asarray(place, BF16)
    ones_col = np.zeros((vw, 1), np.float32)
    ones_col[np.arange(MLA_HEADS) * V_PAD + V_HEAD, 0] = 1.0
    ones_col = jnp.asarray(ones_col)

    wglu = s5_w_glu.astype(BF16)
    wbr = w_branch.astype(BF16)
    wout = w_out.astype(BF16)
    ffn_in = ffn_w_in.astype(BF16)
    ffn_out = ffn_w_out.astype(BF16)

    dtb, wst, wot = _s5_prep_call(*_s5_prep_inputs(s5_lambda_re, s5_lambda_im, s5_log_dt, s5_b_re, s5_b_im,
                                                   s5_c_re, s5_c_im, s5_d))
    pa, pb = _s5_scan_multipliers(s5_lambda_re, s5_lambda_im, s5_log_dt, nch)

    x2 = x.reshape(t, d)
    for l in range(depth):
        h, u5, cqn, ckvn, ysgu, krot = _inproj_call(
            x2, ada_all, l, w_a, b_a, row(mla_q_norm), row(mla_kv_norm), row(sgu_ln_g), row(sgu_ln_b),
            ws_m, bs_full, cosk, sink, seq, tm)
        kmat, qt, vt = _mla_proj_call(cqn, ckvn, krot, aqt, sqt, wqt, wk, pk, wvt, ones_col, l, batch, seq, tm)
        ymla = _flash_call(kmat, qt, vt, batch, seq, tq)
        yssm = _s5_call(u5, dtb, wst, wot, pa, pb, l, nch)
        x2 = _merge_call(h, yssm, ymla, ysgu, x2, ada_all, l, w_g, b_g, wglu, row(s5_b_glu), wbr, wout,
                         row(ln1_g), row(ln1_b), seq, tm, alpha)
        x2 = _ffn_call(x2, ada_all, l, ffn_in, ffn_out, row(ln2_g), row(ln2_b), seq, tm, alpha)
    return x2.reshape(batch, seq, d)
```

```python
import functools
import math

import numpy as np
import jax
import jax.numpy as jnp
from jax import lax
from jax.experimental import pallas as pl
from jax.experimental.pallas import tpu as pltpu

F32 = jnp.float32
BF16 = jnp.bfloat16

CHUNK = 64
S5_WIDTH = 512
S5_GROUP = 16
S5_GROUPS = S5_WIDTH // S5_GROUP
S5_STATE = 64
MLA_HEADS = 8
QK_NOPE = 64
QK_ROPE = 32
V_HEAD = 64
Q_LORA = 384
KV_LORA = 256
ROPE_THETA = 10000.0
SGU_WIDTH = 512
SGU_GROUPS = 4
SGU_CHUNK = 128
N_BRANCH = 3
BRANCH_WIDTH = 512
LN_EPS = 1e-5
RMS_EPS = 1e-6
NEG_INF = -1e30

LANES = 128
SUBLANES = 8
BF16_ROWS = 16
HEAD_PAD = 128
V_PAD = V_HEAD + BF16_ROWS
S5_SUB = 16
S5_NSUB = CHUNK // S5_SUB
S5_BLK = S5_SUB * S5_GROUP
S5_FLAT = CHUNK * S5_GROUP
S5_GPB = LANES // S5_GROUP
S5_SCAN_COLS = 8
INPROJ_SUB = 256
MERGE_SUB = 256
FFN_SUB = 256
FLASH_SUB = 256
FLASH_DEPTH = 6

OFF_U5 = 0
OFF_CQ = OFF_U5 + S5_WIDTH
OFF_CKV = OFF_CQ + Q_LORA
OFF_USGU = OFF_CKV + KV_LORA
OFF_VSGU = OFF_USGU + SGU_WIDTH
OFF_KPE = OFF_VSGU + SGU_WIDTH
NA = OFF_KPE + LANES

VMEM_LIMIT = 56 * 1024 * 1024

HI = lax.Precision.HIGHEST
NT_DIMS = (((1,), (1,)), ((), ()))
LOG2E = 1.4426950408889634


def _gelu(x):
    return 0.5 * x * (1.0 + jnp.tanh(0.7978845608028654 * (x + 0.044715 * (x * x * x))))


def _sigmoid(x):
    return 0.5 * jnp.tanh(0.5 * x) + 0.5


def _layer_norm(r, g, b):
    mu = jnp.mean(r, axis=-1, keepdims=True)
    rc = r - mu
    var = jnp.mean(rc * rc, axis=-1, keepdims=True)
    return rc * lax.rsqrt(var + LN_EPS) * g + b


def _const_spec(shape):
    nd = len(shape)
    return pl.BlockSpec(shape, lambda *_: (0,) * nd, pipeline_mode=pl.Buffered(1))


def _layer_spec(arr, layer, col_block=None):
    shape = (1,) + arr.shape[1:]
    idx = (layer,) + (0,) * (arr.ndim - 1)
    if col_block is not None:
        width, j = col_block
        shape = shape[:-1] + (width,)
        idx = idx[:-1] + (j,)
    return pl.BlockSpec(shape, lambda *_: idx, pipeline_mode=pl.Buffered(1))


def _params(sem):
    return pltpu.CompilerParams(dimension_semantics=sem, vmem_limit_bytes=VMEM_LIMIT)


def _ada_kernel(c_ref, w_ref, b_ref, o_ref):
    c = c_ref[...]
    ca = c * _sigmoid(c)
    o_ref[0] = jnp.dot(ca, w_ref[0], preferred_element_type=F32, precision=HI) + b_ref[0]


def _ada_call(c_pad, w_ada, b_ada):
    depth, d, n6 = w_ada.shape
    tn = n6 // 4
    rows = c_pad.shape[0]
    return pl.pallas_call(
        _ada_kernel,
        grid=(depth, n6 // tn),
        in_specs=[
            pl.BlockSpec((rows, d), lambda l, j: (0, 0)),
            pl.BlockSpec((1, d, tn), lambda l, j: (l, 0, j)),
            pl.BlockSpec((1, 1, tn), lambda l, j: (l, 0, j)),
        ],
        out_specs=pl.BlockSpec((1, rows, tn), lambda l, j: (l, 0, j)),
        out_shape=jax.ShapeDtypeStruct((depth, rows, n6), F32),
        compiler_params=_params(("arbitrary", "arbitrary")),
        name="ada",
    )(c_pad, w_ada, b_ada.reshape(depth, 1, n6))


def _inproj_kernel(x_ref, ada_ref, w_ref, b_ref, qn_ref, kvn_ref, lng_ref, lnb_ref, ws_ref, bs_ref,
                   cosk_ref, sink_ref,
                   h_ref, u5_ref, cq_ref, ckv_ref, ysgu_ref, krot_ref):
    tm = x_ref.shape[0]
    ada = ada_ref[0, 0]
    sub = INPROJ_SUB
    gw = SGU_WIDTH // SGU_GROUPS

    def project(r):
        rows = slice(r * sub, (r + 1) * sub)
        h = (x_ref[rows, :] * (1.0 + ada[1:2]) + ada[0:1]).astype(BF16)
        h_ref[rows, :] = h
        return jnp.dot(h, w_ref[0], preferred_element_type=F32) + b_ref[0]

    def epilogue(r, acc):
        rows = slice(r * sub, (r + 1) * sub)
        u5_ref[rows, :] = acc[:, OFF_U5:OFF_U5 + S5_WIDTH]

        cq = acc[:, OFF_CQ:OFF_CQ + Q_LORA]
        cq_ref[rows, :] = (cq * lax.rsqrt(jnp.mean(cq * cq, axis=-1, keepdims=True) + RMS_EPS)
                           * qn_ref[0]).astype(BF16)
        ckv = acc[:, OFF_CKV:OFF_CKV + KV_LORA]
        ckv_ref[rows, :] = (ckv * lax.rsqrt(jnp.mean(ckv * ckv, axis=-1, keepdims=True) + RMS_EPS)
                            * kvn_ref[0]).astype(BF16)

        kb = acc[:, OFF_KPE:OFF_KPE + LANES]
        krot = kb * cosk_ref[rows, :] + pltpu.roll(kb, LANES - QK_ROPE, 1) * sink_ref[rows, :]
        krot_ref[rows, :] = krot.astype(BF16)

        us = _gelu(acc[:, OFF_USGU:OFF_USGU + SGU_WIDTH])
        vn = _layer_norm(_gelu(acc[:, OFF_VSGU:OFF_VSGU + SGU_WIDTH]), lng_ref[0], lnb_ref[0]).astype(BF16)
        for c in range(sub // SGU_CHUNK):
            crow = slice(c * SGU_CHUNK, (c + 1) * SGU_CHUNK)
            parts = [jnp.dot(ws_ref[0, g], vn[crow, g * gw:(g + 1) * gw], preferred_element_type=F32)
                     for g in range(SGU_GROUPS)]
            mixed = jnp.concatenate(parts, axis=1) + bs_ref[0]
            ysgu_ref[r * sub + c * SGU_CHUNK:r * sub + (c + 1) * SGU_CHUNK, :] = (us[crow, :] * mixed).astype(BF16)

    nsub = tm // sub
    acc = project(0)
    for r in range(nsub):
        nxt = project(r + 1) if r + 1 < nsub else None
        epilogue(r, acc)
        acc = nxt


def _inproj_call(x2, ada_all, layer, w, b, qn, kvn, lng, lnb, ws, bs, cosk, sink, seq, tm):
    t, d = x2.shape
    tpb = seq // tm
    tok = lambda n: pl.BlockSpec((tm, n), lambda i: (i, 0))
    consts = [w, b, qn, kvn, lng, lnb, ws, bs]
    return pl.pallas_call(
        _inproj_kernel,
        grid=(t // tm,),
        in_specs=[
            tok(d),
            pl.BlockSpec((1, 1, 6, d), lambda i: (layer, i // tpb, 0, 0)),
        ] + [_layer_spec(a, layer) for a in consts] + [
            pl.BlockSpec((tm, LANES), lambda i: (i % tpb, 0)),
            pl.BlockSpec((tm, LANES), lambda i: (i % tpb, 0)),
        ],
        out_specs=[tok(d), tok(S5_WIDTH), tok(Q_LORA), tok(KV_LORA), tok(SGU_WIDTH), tok(LANES)],
        out_shape=[
            jax.ShapeDtypeStruct((t, d), BF16),
            jax.ShapeDtypeStruct((t, S5_WIDTH), F32),
            jax.ShapeDtypeStruct((t, Q_LORA), BF16),
            jax.ShapeDtypeStruct((t, KV_LORA), BF16),
            jax.ShapeDtypeStruct((t, SGU_WIDTH), BF16),
            jax.ShapeDtypeStruct((t, LANES), BF16),
        ],
        compiler_params=_params(("arbitrary",)),
        name="inproj",
    )(x2, ada_all, *consts, cosk, sink)


def _mla_proj_kernel(cq_ref, ckv_ref, krot_ref, aqt_ref, sqt_ref, wqt_ref, wk_ref, pk_ref, wvt_ref, ones_ref,
                     k_ref, qt_ref, vt_ref):
    ckv = ckv_ref[...]
    k_ref[...] = (jnp.dot(ckv, wk_ref[0], preferred_element_type=F32)
                  + jnp.dot(krot_ref[...], pk_ref[...], preferred_element_type=F32)).astype(BF16)
    qt = lax.dot_general(wqt_ref[0], cq_ref[...], NT_DIMS, preferred_element_type=F32)
    aqt = aqt_ref[...]
    sqt = sqt_ref[...]
    for h in range(MLA_HEADS):
        rows = slice(h * HEAD_PAD, (h + 1) * HEAD_PAD)
        qh = qt[rows, :]
        qt_ref[0, rows, :] = (qh * aqt + pltpu.roll(qh, HEAD_PAD - QK_ROPE, 0) * sqt).astype(BF16)
    vt = lax.dot_general(wvt_ref[0], ckv, NT_DIMS, preferred_element_type=F32) + ones_ref[...]
    vt_ref[0] = vt.astype(BF16)


def _mla_proj_call(cqn, ckvn, krot, aqt, sqt, wqt, wk, pk, wvt, ones, layer, batch, seq, tm):
    t = cqn.shape[0]
    tpb = seq // tm
    hw = MLA_HEADS * HEAD_PAD
    vw = MLA_HEADS * V_PAD
    tok = lambda n: pl.BlockSpec((tm, n), lambda i: (i, 0))
    tab = pl.BlockSpec((HEAD_PAD, tm), lambda i: (0, i % tpb))
    return pl.pallas_call(
        _mla_proj_kernel,
        grid=(t // tm,),
        in_specs=[tok(Q_LORA), tok(KV_LORA), tok(LANES), tab, tab,
                  _layer_spec(wqt, layer), _layer_spec(wk, layer), _const_spec(pk.shape), _layer_spec(wvt, layer),
                  _const_spec(ones.shape)],
        out_specs=[tok(hw),
                   pl.BlockSpec((1, hw, tm), lambda i: (i, 0, 0)),
                   pl.BlockSpec((1, vw, tm), lambda i: (i, 0, 0))],
        out_shape=[
            jax.ShapeDtypeStruct((t, hw), BF16),
            jax.ShapeDtypeStruct((t // tm, hw, tm), BF16),
            jax.ShapeDtypeStruct((t // tm, vw, tm), BF16),
        ],
        compiler_params=_params(("arbitrary",)),
        name="mla_proj",
    )(cqn, ckvn, krot, aqt, sqt, wqt, wk, pk, wvt, ones)


def _flash_kernel(qi_ref, ki_ref, k_ref, qt_ref, vt_ref, o_ref, m_sc, acc_sc):
    p_idx = pl.program_id(1)
    qi = qi_ref[p_idx]
    ki = ki_ref[p_idx]
    tk = k_ref.shape[0]
    tq = qt_ref.shape[2]

    @pl.when(ki == 0)
    def _init():
        m_sc[...] = jnp.full(m_sc.shape, NEG_INF, F32)
        acc_sc[...] = jnp.zeros(acc_sc.shape, F32)

    def step(masked):
        sub = FLASH_SUB
        nkb, nqb = tk // sub, tq // sub
        state = [(h, qb) for h in range(MLA_HEADS) for qb in range(nqb)]
        m_all = {(h, qb): m_sc[h, :, qb * sub:(qb + 1) * sub] for h, qb in state}
        acc_all = {(h, qb): acc_sc[h, :, qb * sub:(qb + 1) * sub] for h, qb in state}
        blocks = [(h, qb, kb) for kb in range(nkb) for h, qb in state]
        if masked:
            blocks = [(h, qb, kb) for h, qb, kb in blocks if kb * sub // CHUNK <= ((qb + 1) * sub - 1) // CHUNK]

        def scores(h, qb, kb):
            cols = slice(h * HEAD_PAD, (h + 1) * HEAD_PAD)
            st = jnp.dot(k_ref[kb * sub:(kb + 1) * sub, cols], qt_ref[0, cols, qb * sub:(qb + 1) * sub],
                         preferred_element_type=F32)
            if masked and ((kb + 1) * sub - 1) // CHUNK > qb * sub // CHUNK:
                kc = (lax.broadcasted_iota(jnp.int32, (sub, sub), 0) + kb * sub) // CHUNK
                qc = (lax.broadcasted_iota(jnp.int32, (sub, sub), 1) + qb * sub) // CHUNK
                st = jnp.where(kc <= qc, st, NEG_INF)
            return st

        def absorb(h, qb, kb, st):
            m_prev = m_all[(h, qb)]
            m_new = jnp.maximum(m_prev, jnp.max(st, axis=0, keepdims=True))
            m_all[(h, qb)] = m_new
            pt = jnp.exp2(st - m_new[0:1]).astype(BF16)
            acc_all[(h, qb)] = (jnp.exp2(m_prev - m_new)[0:1] * acc_all[(h, qb)]
                                + jnp.dot(vt_ref[0, h * V_PAD:(h + 1) * V_PAD, kb * sub:(kb + 1) * sub], pt,
                                          preferred_element_type=F32))

        nblk = len(blocks)
        pending = {i: scores(*blocks[i]) for i in range(min(FLASH_DEPTH, nblk))}
        for i in range(nblk):
            if i + FLASH_DEPTH < nblk:
                pending[i + FLASH_DEPTH] = scores(*blocks[i + FLASH_DEPTH])
            absorb(*blocks[i], pending.pop(i))
        for h, qb in state:
            m_sc[h, :, qb * sub:(qb + 1) * sub] = m_all[(h, qb)]
            acc_sc[h, :, qb * sub:(qb + 1) * sub] = acc_all[(h, qb)]

    @pl.when(ki < qi)
    def _full():
        step(False)

    @pl.when(ki == qi)
    def _diag():
        step(True)
        outs = []
        for h in range(MLA_HEADS):
            acc = acc_sc[h]
            outs.append(acc[:V_HEAD] / acc[V_HEAD:V_HEAD + 1])
        o_ref[...] = jnp.concatenate(outs, axis=0).T.astype(BF16)


def _flash_call(k, qt, vt, batch, seq, tq):
    t, hw = k.shape
    vw = vt.shape[1]
    nq = seq // tq
    pairs = [(a, b) for a in range(nq) for b in range(a + 1)]
    qi_tab = jnp.asarray([p[0] for p in pairs], jnp.int32)
    ki_tab = jnp.asarray([p[1] for p in pairs], jnp.int32)
    grid_spec = pltpu.PrefetchScalarGridSpec(
        num_scalar_prefetch=2,
        grid=(batch, len(pairs)),
        in_specs=[
            pl.BlockSpec((tq, hw), lambda b, p, qt_, kt_: (b * nq + kt_[p], 0)),
            pl.BlockSpec((1, hw, tq), lambda b, p, qt_, kt_: (b * nq + qt_[p], 0, 0)),
            pl.BlockSpec((1, vw, tq), lambda b, p, qt_, kt_: (b * nq + kt_[p], 0, 0)),
        ],
        out_specs=pl.BlockSpec((tq, MLA_HEADS * V_HEAD), lambda b, p, qt_, kt_: (b * nq + qt_[p], 0)),
        scratch_shapes=[
            pltpu.VMEM((MLA_HEADS, SUBLANES, tq), F32),
            pltpu.VMEM((MLA_HEADS, V_PAD, tq), F32),
        ],
    )
    return pl.pallas_call(
        _flash_kernel,
        grid_spec=grid_spec,
        out_shape=jax.ShapeDtypeStruct((t, MLA_HEADS * V_HEAD), BF16),
        compiler_params=_params(("arbitrary", "arbitrary")),
        name="flash",
    )(qi_tab, ki_tab, k, qt, vt)


def _s5_prep_kernel(lr_ref, li_ref, ldt_ref, bt_ref, btsw_ref, cw1_ref, cw2_ref, dsk_ref,
                    dt_ref, wst_ref, wot_ref, ws_sc):
    pw = 2 * S5_STATE
    lr = lr_ref[0]
    li = li_ref[0]
    dt = jnp.exp(ldt_ref[0])
    lane = lax.broadcasted_iota(jnp.int32, (1, pw), 1)
    sgn = jnp.where(lane < S5_STATE, -1.0, 1.0)
    nk = CHUNK + SUBLANES
    kk = lax.broadcasted_iota(jnp.int32, (nk, pw), 0).astype(F32)
    mag = jnp.exp(kk * (lr * dt))
    ang = kk * (li * dt)
    ar = mag * jnp.cos(ang)
    ai = mag * jnp.sin(ang)
    ais = ai * sgn
    den = lr * lr + li * li
    a1r = ar[1:2]
    a1i = ai[1:2]
    f_re = ((a1r - 1.0) * lr + a1i * li) / den
    f_im = (a1i * lr - (a1r - 1.0) * li) / den
    bt = bt_ref[0]
    btsw = btsw_ref[0]
    bb = f_re * bt + (f_im * sgn) * btsw
    bbsw = f_re * btsw - (f_im * sgn) * bt
    cw1 = cw1_ref[0]
    cw2 = cw2_ref[0]
    for s in range(CHUNK):
        k = CHUNK - 1 - s
        ws_sc[s * S5_GROUP:(s + 1) * S5_GROUP, :] = ar[k:k + 1] * bb + ais[k:k + 1] * bbsw
        wot_ref[0, s * S5_GROUP:(s + 1) * S5_GROUP, :] = (ar[s + 1:s + 2] * cw1 + ai[s + 1:s + 2] * cw2).astype(BF16)
    ws = ws_sc[...]
    wst_ref[0] = ws.T.astype(BF16)
    strip = lax.dot_general(cw1, ws, NT_DIMS, preferred_element_type=F32, precision=HI)
    row = lax.broadcasted_iota(jnp.int32, (S5_GROUP, S5_FLAT), 0)
    col = lax.broadcasted_iota(jnp.int32, (S5_GROUP, S5_FLAT), 1)
    dsk = jnp.concatenate([dsk_ref[0]] * (S5_FLAT // LANES), axis=1)
    strip = strip + jnp.where(col == row + (S5_FLAT - S5_GROUP), dsk, 0.0)
    strip = jnp.concatenate([strip, jnp.zeros((S5_GROUP, S5_BLK), F32)], axis=1)
    for d in range(S5_NSUB):
        for tt in range(S5_SUB):
            i0 = CHUNK - 1 - S5_SUB * d - tt
            dt_ref[0, d, tt * S5_GROUP:(tt + 1) * S5_GROUP, :] = (
                strip[:, S5_GROUP * i0:S5_GROUP * i0 + S5_BLK].astype(BF16))


def _s5_prep_call(lr2, li2, ldt2, bt, btsw, cw1, cw2, dsk):
    n = lr2.shape[0]
    pw = 2 * S5_STATE
    vec = pl.BlockSpec((1, 1, pw), lambda i: (i, 0, 0))
    mat = pl.BlockSpec((1, S5_GROUP, pw), lambda i: (i, 0, 0))
    return pl.pallas_call(
        _s5_prep_kernel,
        grid=(n,),
        in_specs=[vec, vec, vec, mat, mat, mat, mat, mat],
        out_specs=[
            pl.BlockSpec((1, S5_NSUB, S5_BLK, S5_BLK), lambda i: (i, 0, 0, 0)),
            pl.BlockSpec((1, pw, S5_FLAT), lambda i: (i, 0, 0)),
            pl.BlockSpec((1, S5_FLAT, pw), lambda i: (i, 0, 0)),
        ],
        out_shape=[
            jax.ShapeDtypeStruct((n, S5_NSUB, S5_BLK, S5_BLK), BF16),
            jax.ShapeDtypeStruct((n, pw, S5_FLAT), BF16),
            jax.ShapeDtypeStruct((n, S5_FLAT, pw), BF16),
        ],
        scratch_shapes=[pltpu.VMEM((S5_FLAT, pw), F32)],
        compiler_params=_params(("arbitrary",)),
        name="s5_prep",
    )(lr2, li2, ldt2, bt, btsw, cw1, cw2, dsk)


def _s5_kernel(u_ref, dt_ref, wst_ref, wot_ref, pa_ref, pb_ref, y_ref, ut_sc, yt_sc, *, nch):
    gg = pl.program_id(1)
    ncol = ut_sc.shape[2]
    pw = 2 * S5_STATE

    @pl.when(gg == 0)
    def _regroup():
        for s in range(CHUNK):
            ut_sc[s] = u_ref[pl.ds(s, ncol, stride=CHUNK), :].T.astype(BF16)

    row0 = pl.multiple_of(gg * S5_GROUP, S5_GROUP)
    u = ut_sc[:, pl.ds(row0, S5_GROUP), :].reshape(S5_FLAT, ncol)
    h = jnp.dot(wst_ref[0], u, preferred_element_type=F32)
    n_idx = lax.broadcasted_iota(jnp.int32, (pw, ncol), 1) % nch
    pa = pa_ref[0]
    pb = pb_ref[0]
    step = 1
    j = 0
    while step < nch:
        sh = jnp.where(n_idx >= step, pltpu.roll(h, step, 1), 0.0)
        h = h + pa[:, j:j + 1] * sh + pb[:, j:j + 1] * pltpu.roll(sh, S5_STATE, 0)
        step *= 2
        j += 1
    hprev = jnp.where(n_idx >= 1, pltpu.roll(h, 1, 1), 0.0).astype(BF16)
    for jj in range(S5_NSUB):
        acc = jnp.dot(wot_ref[0, jj * S5_BLK:(jj + 1) * S5_BLK, :], hprev, preferred_element_type=F32)
        for ii in range(jj + 1):
            acc = acc + jnp.dot(dt_ref[0, jj - ii], u[ii * S5_BLK:(ii + 1) * S5_BLK, :],
                                preferred_element_type=F32)
        yt_sc[jj * S5_SUB:(jj + 1) * S5_SUB, pl.ds(row0, S5_GROUP), :] = acc.reshape(S5_SUB, S5_GROUP, ncol)

    @pl.when(gg == S5_GPB - 1)
    def _emit():
        for tt in range(CHUNK):
            y_ref[pl.ds(tt, ncol, stride=CHUNK), :] = yt_sc[tt].T


def _s5_call(u5, dtb, wst, wot, pa, pb, layer, nch):
    t, width = u5.shape
    ncol = t // CHUNK
    nblk = width // LANES
    base = layer * S5_GROUPS
    grp = lambda shape: pl.BlockSpec((1,) + shape, lambda j, g: (base + j * S5_GPB + g,) + (0,) * len(shape))
    return pl.pallas_call(
        functools.partial(_s5_kernel, nch=nch),
        grid=(nblk, S5_GPB),
        in_specs=[
            pl.BlockSpec((t, LANES), lambda j, g: (0, j)),
            grp(dtb.shape[1:]), grp(wst.shape[1:]), grp(wot.shape[1:]), grp(pa.shape[1:]), grp(pb.shape[1:]),
        ],
        out_specs=pl.BlockSpec((t, LANES), lambda j, g: (0, j)),
        out_shape=jax.ShapeDtypeStruct((t, width), F32),
        scratch_shapes=[pltpu.VMEM((CHUNK, LANES, ncol), BF16), pltpu.VMEM((CHUNK, LANES, ncol), F32)],
        compiler_params=_params(("arbitrary", "arbitrary")),
        name="s5",
    )(u5, dtb, wst, wot, pa, pb)


def _merge_kernel(h_ref, yssm_ref, ymla_ref, ysgu_ref, x_ref, ada_ref,
                  wg_ref, bg_ref, wglu_ref, bglu_ref, wb_ref, wout_ref, lng_ref, lnb_ref,
                  o_ref, *, alpha):
    tm, d = x_ref.shape
    sub = MERGE_SUB
    g1 = ada_ref[0, 0][2:3]

    def merge(r):
        rows = slice(r * sub, (r + 1) * sub)
        h = h_ref[rows, :]
        z = _gelu(yssm_ref[rows, :])
        y5 = (z * _sigmoid(jnp.dot(z.astype(BF16), wglu_ref[0], preferred_element_type=F32)
                           + bglu_ref[0])).astype(BF16)
        branches = (y5, ymla_ref[rows, :], ysgu_ref[rows, :])
        merged = None
        for i in range(N_BRANCH):
            cols = slice(i * d, (i + 1) * d)
            gate = _sigmoid(jnp.dot(h, wg_ref[0, :, cols], preferred_element_type=F32) + bg_ref[0, :, cols])
            term = gate * jnp.dot(branches[i], wb_ref[0, i], preferred_element_type=F32)
            merged = term if merged is None else merged + term
        return merged.astype(BF16)

    def project(r, merged):
        rows = slice(r * sub, (r + 1) * sub)
        y = jnp.dot(merged, wout_ref[0], preferred_element_type=F32)
        o_ref[rows, :] = _layer_norm(alpha * x_ref[rows, :] + (1.0 + g1) * y, lng_ref[0], lnb_ref[0])

    nsub = tm // sub
    merged = merge(0)
    for r in range(nsub):
        nxt = merge(r + 1) if r + 1 < nsub else None
        project(r, merged)
        merged = nxt


def _merge_call(h, yssm, ymla, ysgu, x2, ada_all, layer, wg, bg, wglu, bglu, wb, wout, lng, lnb, seq, tm, alpha):
    t, d = x2.shape
    tpb = seq // tm
    tok = lambda n: pl.BlockSpec((tm, n), lambda i: (i, 0))
    consts = [wg, bg, wglu, bglu, wb, wout, lng, lnb]
    return pl.pallas_call(
        functools.partial(_merge_kernel, alpha=alpha),
        grid=(t // tm,),
        in_specs=[tok(d), tok(S5_WIDTH), tok(BRANCH_WIDTH), tok(BRANCH_WIDTH), tok(d),
                  pl.BlockSpec((1, 1, 6, d), lambda i: (layer, i // tpb, 0, 0))]
                 + [_layer_spec(a, layer) for a in consts],
        out_specs=tok(d),
        out_shape=jax.ShapeDtypeStruct((t, d), F32),
        compiler_params=_params(("arbitrary",)),
        name="merge",
    )(h, yssm, ymla, ysgu, x2, ada_all, *consts)


def _ffn_kernel(x_ref, ada_ref, wa_ref, wb_ref, wo_ref, lng_ref, lnb_ref, o_ref, *, alpha):
    tm = x_ref.shape[0]
    sub = FFN_SUB
    ada = ada_ref[0, 0]

    def hidden(r):
        rows = slice(r * sub, (r + 1) * sub)
        h = (x_ref[rows, :] * (1.0 + ada[4:5]) + ada[3:4]).astype(BF16)
        a = jnp.dot(h, wa_ref[0], preferred_element_type=F32)
        b = jnp.dot(h, wb_ref[0], preferred_element_type=F32)
        return (a * _sigmoid(a) * b).astype(BF16)

    def project(r, act):
        rows = slice(r * sub, (r + 1) * sub)
        f = jnp.dot(act, wo_ref[0], preferred_element_type=F32)
        o_ref[rows, :] = _layer_norm(alpha * x_ref[rows, :] + (1.0 + ada[5:6]) * f, lng_ref[0], lnb_ref[0])

    nsub = tm // sub
    act = hidden(0)
    for r in range(nsub):
        nxt = hidden(r + 1) if r + 1 < nsub else None
        project(r, act)
        act = nxt


def _ffn_call(x2, ada_all, layer, w_in, w_out, lng, lnb, seq, tm, alpha):
    t, d = x2.shape
    ff = w_out.shape[1]
    tpb = seq // tm
    tok = pl.BlockSpec((tm, d), lambda i: (i, 0))
    return pl.pallas_call(
        functools.partial(_ffn_kernel, alpha=alpha),
        grid=(t // tm,),
        in_specs=[tok, pl.BlockSpec((1, 1, 6, d), lambda i: (layer, i // tpb, 0, 0)),
                  _layer_spec(w_in, layer, (ff, 0)), _layer_spec(w_in, layer, (ff, 1)), _layer_spec(w_out, layer),
                  _layer_spec(lng, layer), _layer_spec(lnb, layer)],
        out_specs=tok,
        out_shape=jax.ShapeDtypeStruct((t, d), F32),
        compiler_params=_params(("arbitrary",)),
        name="ffn",
    )(x2, ada_all, w_in, w_in, w_out, lng, lnb)


def _s5_prep_inputs(lam_re, lam_im, log_dt, b_re, b_im, c_re, c_im, dskip):
    depth, g, p = lam_re.shape
    n = depth * g
    dup = lambda a: jnp.concatenate([a, a], axis=-1).reshape(n, 1, 2 * p)
    lr2 = dup(lam_re)
    li2 = dup(lam_im)
    ldt2 = jnp.broadcast_to(log_dt[..., None], (depth, g, 2 * p)).reshape(n, 1, 2 * p)
    btr = b_re.transpose(0, 1, 3, 2).reshape(n, S5_GROUP, p)
    bti = b_im.transpose(0, 1, 3, 2).reshape(n, S5_GROUP, p)
    bt = jnp.concatenate([btr, bti], axis=-1)
    btsw = jnp.concatenate([bti, btr], axis=-1)
    cr = c_re.reshape(n, S5_GROUP, p)
    ci = c_im.reshape(n, S5_GROUP, p)
    cw1 = jnp.concatenate([cr, -ci], axis=-1)
    cw2 = jnp.concatenate([-ci, -cr], axis=-1)
    dsk = jnp.broadcast_to(dskip.reshape(n, S5_GROUP, 1), (n, S5_GROUP, LANES))
    return lr2, li2, ldt2, bt, btsw, cw1, cw2, dsk


def _s5_scan_multipliers(lam_re, lam_im, log_dt, nch):
    depth, g, p = lam_re.shape
    dt = jnp.exp(log_dt)[..., None, None]
    nsteps = max(1, (nch - 1).bit_length())
    k = jnp.asarray(CHUNK * (2.0 ** np.arange(nsteps)), F32)
    mag = jnp.exp(lam_re[..., None] * dt * k)
    ang = lam_im[..., None] * dt * k
    sr = (mag * jnp.cos(ang)).reshape(depth * g, p, nsteps)
    si = (mag * jnp.sin(ang)).reshape(depth * g, p, nsteps)
    pad = ((0, 0), (0, 0), (0, S5_SCAN_COLS - nsteps))
    pa = jnp.pad(jnp.concatenate([sr, sr], axis=1), pad)
    pb = jnp.pad(jnp.concatenate([-si, si], axis=1), pad)
    return pa, pb


def _rope_tables(seq, scale):
    inv_freq = 1.0 / (ROPE_THETA ** (jnp.arange(0, QK_ROPE, 2, dtype=F32) / QK_ROPE))
    ang = jnp.arange(seq, dtype=F32)[:, None] * inv_freq[None, :]
    cos = jnp.cos(ang)
    sin = jnp.sin(ang)
    zk = jnp.zeros((seq, LANES - QK_ROPE), F32)
    cosk = jnp.concatenate([cos, cos, zk], axis=1)
    sink = jnp.concatenate([-sin, sin, zk], axis=1)
    zq = jnp.zeros((seq, HEAD_PAD - QK_NOPE - QK_ROPE), F32)
    aq = jnp.concatenate([jnp.full((seq, QK_NOPE), scale, F32), scale * cos, scale * cos, zq], axis=1)
    sq = jnp.concatenate([jnp.zeros((seq, QK_NOPE), F32), -scale * sin, scale * sin, zq], axis=1)
    return cosk, sink, aq.T, sq.T


def _swap_halves(w):
    half = w.shape[-1] // 2
    return jnp.concatenate([w[..., half:], w[..., :half]], axis=-1)


def kernel(x, c, w_ada, b_ada, w_in, b_in, s5_lambda_re, s5_lambda_im, s5_log_dt, s5_b_re, s5_b_im, s5_c_re, s5_c_im, s5_d, s5_w_glu, s5_b_glu, mla_q_norm, mla_w_q_up, mla_kv_norm, mla_w_kv_up, sgu_ln_g, sgu_ln_b, sgu_w_s, sgu_b_s, w_branch, w_out, ln1_g, ln1_b, ffn_w_in, ffn_w_out, ln2_g, ln2_b):
    batch, seq, d = x.shape
    depth = w_ada.shape[0]
    t = batch * seq
    alpha = float((2 * depth) ** 0.25)
    tm = 512
    tq = 512
    nch = seq // CHUNK
    assert seq % tm == 0 and tm == tq and tm % SGU_CHUNK == 0 and (batch * nch) % LANES == 0

    c_pad = jnp.pad(c, ((0, -batch % SUBLANES), (0, 0)))
    ada_all = _ada_call(c_pad, w_ada, b_ada)[:, :batch].reshape(depth, batch, 6, d)

    scale = LOG2E * (QK_NOPE + QK_ROPE) ** -0.5
    cosk, sink, aqt, sqt = _rope_tables(seq, scale)

    offs = np.cumsum([0, S5_WIDTH, Q_LORA, KV_LORA, QK_ROPE, SGU_WIDTH, SGU_WIDTH])
    o_u5, o_cq, o_ckv, o_kpe, o_usgu, o_vsgu, o_gate = [int(o) for o in offs]
    row = lambda a: a[:, None, :]

    kpe_w = w_in[:, :, o_kpe:o_kpe + QK_ROPE]
    kpe_b = b_in[:, o_kpe:o_kpe + QK_ROPE]
    w_a = jnp.concatenate([w_in[:, :, o_u5:o_kpe], w_in[:, :, o_usgu:o_gate], kpe_w, _swap_halves(kpe_w),
                           jnp.zeros((depth, d, LANES - 2 * QK_ROPE), F32)], axis=2).astype(BF16)
    b_a = row(jnp.concatenate([b_in[:, o_u5:o_kpe], b_in[:, o_usgu:o_gate], kpe_b, _swap_halves(kpe_b),
                               jnp.zeros((depth, LANES - 2 * QK_ROPE), F32)], axis=1))
    w_g = w_in[:, :, o_gate:].astype(BF16)
    b_g = row(b_in[:, o_gate:])
    pos = np.arange(SGU_CHUNK) // CHUNK
    sgu_mask = jnp.asarray(pos[None, :] <= pos[:, None])
    ws_m = jnp.where(sgu_mask[None, None], sgu_w_s, 0.0).astype(BF16)
    bs_full = jnp.repeat(sgu_b_s.transpose(0, 2, 1), SGU_WIDTH // SGU_GROUPS, axis=2)

    hw = MLA_HEADS * HEAD_PAD
    vw = MLA_HEADS * V_PAD
    wq4 = mla_w_q_up.reshape(depth, Q_LORA, MLA_HEADS, QK_NOPE + QK_ROPE)
    wqt = (jnp.concatenate([wq4, _swap_halves(wq4[..., QK_NOPE:])], axis=3)
           .reshape(depth, Q_LORA, hw).transpose(0, 2, 1).astype(BF16))
    wkv4 = mla_w_kv_up.reshape(depth, KV_LORA, MLA_HEADS, QK_NOPE + V_HEAD)
    wk = jnp.concatenate([wkv4[..., :QK_NOPE], jnp.zeros((depth, KV_LORA, MLA_HEADS, HEAD_PAD - QK_NOPE), F32)],
                         axis=3).reshape(depth, KV_LORA, hw).astype(BF16)
    wvt = (jnp.concatenate([wkv4[..., QK_NOPE:], jnp.zeros((depth, KV_LORA, MLA_HEADS, V_PAD - V_HEAD), F32)],
                           axis=3).reshape(depth, KV_LORA, vw).transpose(0, 2, 1).astype(BF16))
    place = np.zeros((LANES, hw), np.float32)
    for hh in range(MLA_HEADS):
        place[np.arange(QK_ROPE), hh * HEAD_PAD + QK_NOPE + np.arange(QK_ROPE)] = 1.0
    pk = jnp.asarray(place, BF16)
    ones_col = np.zeros((vw, 1), np.float32)
    ones_col[np.arange(MLA_HEADS) * V_PAD + V_HEAD, 0] = 1.0
    ones_col = jnp.asarray(ones_col)

    wglu = s5_w_glu.astype(BF16)
    wbr = w_branch.astype(BF16)
    wout = w_out.astype(BF16)
    ffn_in = ffn_w_in.astype(BF16)
    ffn_out = ffn_w_out.astype(BF16)

    dtb, wst, wot = _s5_prep_call(*_s5_prep_inputs(s5_lambda_re, s5_lambda_im, s5_log_dt, s5_b_re, s5_b_im,
                                                   s5_c_re, s5_c_im, s5_d))
    pa, pb = _s5_scan_multipliers(s5_lambda_re, s5_lambda_im, s5_log_dt, nch)

    x2 = x.reshape(t, d)
    for l in range(depth):
        h, u5, cqn, ckvn, ysgu, krot = _inproj_call(
            x2, ada_all, l, w_a, b_a, row(mla_q_norm), row(mla_kv_norm), row(sgu_ln_g), row(sgu_ln_b),
            ws_m, bs_full, cosk, sink, seq, tm)
        kmat, qt, vt = _mla_proj_call(cqn, ckvn, krot, aqt, sqt, wqt, wk, pk, wvt, ones_col, l, batch, seq, tm)
        ymla = _flash_call(kmat, qt, vt, batch, seq, tq)
        yssm = _s5_call(u5, dtb, wst, wot, pa, pb, l, nch)
        x2 = _merge_call(h, yssm, ymla, ysgu, x2, ada_all, l, w_g, b_g, wglu, row(s5_b_glu), wbr, wout,
                         row(ln1_g), row(ln1_b), seq, tm, alpha)
        x2 = _ffn_call(x2, ada_all, l, ffn_in, ffn_out, row(ln2_g), row(ln2_b), seq, tm, alpha)
    return x2.reshape(batch, seq, d)
```

```python
import functools
import math

import numpy as np
import jax
import jax.numpy as jnp
from jax import lax
from jax.experimental import pallas as pl
from jax.experimental.pallas import tpu as pltpu

F32 = jnp.float32
BF16 = jnp.bfloat16

CHUNK = 64
S5_WIDTH = 512
S5_GROUP = 16
S5_GROUPS = S5_WIDTH // S5_GROUP
S5_STATE = 64
MLA_HEADS = 8
QK_NOPE = 64
QK_ROPE = 32
V_HEAD = 64
Q_LORA = 384
KV_LORA = 256
ROPE_THETA = 10000.0
SGU_WIDTH = 512
SGU_GROUPS = 4
SGU_CHUNK = 128
N_BRANCH = 3
BRANCH_WIDTH = 512
LN_EPS = 1e-5
RMS_EPS = 1e-6
NEG_INF = -1e30

LANES = 128
SUBLANES = 8
BF16_ROWS = 16
HEAD_PAD = 128
V_PAD = V_HEAD + BF16_ROWS
S5_SUB = 16
S5_NSUB = CHUNK // S5_SUB
S5_BLK = S5_SUB * S5_GROUP
S5_FLAT = CHUNK * S5_GROUP
S5_GPB = LANES // S5_GROUP
S5_SCAN_COLS = 8
INPROJ_SUB = 256
MERGE_SUB = 256
FFN_SUB = 256
S5_PREP_GROUPS = 4
FLASH_SUB = 256
FLASH_DEPTH = 6

OFF_U5 = 0
OFF_CQ = OFF_U5 + S5_WIDTH
OFF_CKV = OFF_CQ + Q_LORA
OFF_USGU = OFF_CKV + KV_LORA
OFF_VSGU = OFF_USGU + SGU_WIDTH
OFF_KPE = OFF_VSGU + SGU_WIDTH
NA = OFF_KPE + LANES

VMEM_LIMIT = 56 * 1024 * 1024

HI = lax.Precision.HIGHEST
NT_DIMS = (((1,), (1,)), ((), ()))
LOG2E = 1.4426950408889634


def _gelu(x):
    return 0.5 * x * (1.0 + jnp.tanh(0.7978845608028654 * (x + 0.044715 * (x * x * x))))


def _sigmoid(x):
    return 0.5 * jnp.tanh(0.5 * x) + 0.5


def _layer_norm(r, g, b):
    mu = jnp.mean(r, axis=-1, keepdims=True)
    rc = r - mu
    var = jnp.mean(rc * rc, axis=-1, keepdims=True)
    return rc * lax.rsqrt(var + LN_EPS) * g + b


def _const_spec(shape):
    nd = len(shape)
    return pl.BlockSpec(shape, lambda *_: (0,) * nd, pipeline_mode=pl.Buffered(1))


def _layer_spec(arr, layer, col_block=None):
    shape = (1,) + arr.shape[1:]
    idx = (layer,) + (0,) * (arr.ndim - 1)
    if col_block is not None:
        width, j = col_block
        shape = shape[:-1] + (width,)
        idx = idx[:-1] + (j,)
    return pl.BlockSpec(shape, lambda *_: idx, pipeline_mode=pl.Buffered(1))


def _params(sem):
    return pltpu.CompilerParams(dimension_semantics=sem, vmem_limit_bytes=VMEM_LIMIT)


def _ada_kernel(c_ref, w_ref, b_ref, o_ref):
    c = c_ref[...]
    ca = c * _sigmoid(c)
    o_ref[0] = jnp.dot(ca, w_ref[0], preferred_element_type=F32, precision=HI) + b_ref[0]


def _ada_call(c_pad, w_ada, b_ada):
    depth, d, n6 = w_ada.shape
    tn = n6 // 4
    rows = c_pad.shape[0]
    return pl.pallas_call(
        _ada_kernel,
        grid=(depth, n6 // tn),
        in_specs=[
            pl.BlockSpec((rows, d), lambda l, j: (0, 0)),
            pl.BlockSpec((1, d, tn), lambda l, j: (l, 0, j)),
            pl.BlockSpec((1, 1, tn), lambda l, j: (l, 0, j)),
        ],
        out_specs=pl.BlockSpec((1, rows, tn), lambda l, j: (l, 0, j)),
        out_shape=jax.ShapeDtypeStruct((depth, rows, n6), F32),
        compiler_params=_params(("arbitrary", "arbitrary")),
        name="ada",
    )(c_pad, w_ada, b_ada.reshape(depth, 1, n6))


def _win_prep_kernel(w_ref, wa_ref, wg_ref, *, o_kpe, o_usgu, o_gate):
    w = w_ref[0]
    half = QK_ROPE // 2
    kpe = w[:, o_kpe:o_kpe + QK_ROPE]
    last = jnp.concatenate([kpe, kpe[:, half:], kpe[:, :half],
                            jnp.zeros((w.shape[0], LANES - 2 * QK_ROPE), F32)], axis=1)
    wa_ref[0, :, 0:o_kpe] = w[:, 0:o_kpe].astype(BF16)
    wa_ref[0, :, o_kpe:OFF_KPE] = w[:, o_usgu:o_gate].astype(BF16)
    wa_ref[0, :, OFF_KPE:NA] = last.astype(BF16)
    wg_ref[0] = w[:, o_gate:].astype(BF16)


def _win_prep_call(w_in, o_kpe, o_usgu, o_gate):
    depth, d, n_in = w_in.shape
    tr = 256
    assert o_kpe == OFF_USGU and o_kpe + (o_gate - o_usgu) == OFF_KPE
    return pl.pallas_call(
        functools.partial(_win_prep_kernel, o_kpe=o_kpe, o_usgu=o_usgu, o_gate=o_gate),
        grid=(depth, d // tr),
        in_specs=[pl.BlockSpec((1, tr, n_in), lambda l, i: (l, i, 0))],
        out_specs=[pl.BlockSpec((1, tr, NA), lambda l, i: (l, i, 0)),
                   pl.BlockSpec((1, tr, n_in - o_gate), lambda l, i: (l, i, 0))],
        out_shape=[jax.ShapeDtypeStruct((depth, d, NA), BF16),
                   jax.ShapeDtypeStruct((depth, d, n_in - o_gate), BF16)],
        compiler_params=_params(("arbitrary", "arbitrary")),
        name="win_prep",
    )(w_in)


def _inproj_kernel(x_ref, ada_ref, w_ref, b_ref, qn_ref, kvn_ref, lng_ref, lnb_ref, ws_ref, bs_ref,
                   wqt_ref, wk_ref, wvt_ref, pk_ref, ones_ref, cosk_ref, sink_ref, aqt_ref, sqt_ref,
                   h_ref, u5_ref, ysgu_ref, k_ref, qt_ref, vt_ref):
    tm = x_ref.shape[0]
    ada = ada_ref[0, 0]
    sub = INPROJ_SUB
    gw = SGU_WIDTH // SGU_GROUPS

    def project(r):
        rows = slice(r * sub, (r + 1) * sub)
        h = (x_ref[rows, :] * (1.0 + ada[1:2]) + ada[0:1]).astype(BF16)
        h_ref[rows, :] = h
        return jnp.dot(h, w_ref[0], preferred_element_type=F32) + b_ref[0]

    def epilogue(r, acc):
        rows = slice(r * sub, (r + 1) * sub)
        u5_ref[rows, :] = acc[:, OFF_U5:OFF_U5 + S5_WIDTH]

        cq = acc[:, OFF_CQ:OFF_CQ + Q_LORA]
        cqn = (cq * lax.rsqrt(jnp.mean(cq * cq, axis=-1, keepdims=True) + RMS_EPS) * qn_ref[0]).astype(BF16)
        ckv = acc[:, OFF_CKV:OFF_CKV + KV_LORA]
        ckvn = (ckv * lax.rsqrt(jnp.mean(ckv * ckv, axis=-1, keepdims=True) + RMS_EPS) * kvn_ref[0]).astype(BF16)

        kb = acc[:, OFF_KPE:OFF_KPE + LANES]
        krot = (kb * cosk_ref[rows, :] + pltpu.roll(kb, LANES - QK_ROPE, 1) * sink_ref[rows, :]).astype(BF16)

        k_ref[rows, :] = (jnp.dot(ckvn, wk_ref[0], preferred_element_type=F32)
                          + jnp.dot(krot, pk_ref[...], preferred_element_type=F32)).astype(BF16)
        qt = lax.dot_general(wqt_ref[0], cqn, NT_DIMS, preferred_element_type=F32)
        aqt = aqt_ref[:, rows]
        sqt = sqt_ref[:, rows]
        for h in range(MLA_HEADS):
            hrow = slice(h * HEAD_PAD, (h + 1) * HEAD_PAD)
            qh = qt[hrow, :]
            qt_ref[0, hrow, rows] = (qh * aqt + pltpu.roll(qh, HEAD_PAD - QK_ROPE, 0) * sqt).astype(BF16)
        vt = lax.dot_general(wvt_ref[0], ckvn, NT_DIMS, preferred_element_type=F32) + ones_ref[...]
        vt_ref[0, :, rows] = vt.astype(BF16)

        us = _gelu(acc[:, OFF_USGU:OFF_USGU + SGU_WIDTH])
        vn = _layer_norm(_gelu(acc[:, OFF_VSGU:OFF_VSGU + SGU_WIDTH]), lng_ref[0], lnb_ref[0]).astype(BF16)
        for c in range(sub // SGU_CHUNK):
            crow = slice(c * SGU_CHUNK, (c + 1) * SGU_CHUNK)
            parts = [jnp.dot(ws_ref[0, g], vn[crow, g * gw:(g + 1) * gw], preferred_element_type=F32)
                     for g in range(SGU_GROUPS)]
            mixed = jnp.concatenate(parts, axis=1) + bs_ref[0]
            ysgu_ref[r * sub + c * SGU_CHUNK:r * sub + (c + 1) * SGU_CHUNK, :] = (us[crow, :] * mixed).astype(BF16)

    nsub = tm // sub
    acc = project(0)
    for r in range(nsub):
        nxt = project(r + 1) if r + 1 < nsub else None
        epilogue(r, acc)
        acc = nxt


def _inproj_call(x2, ada_all, layer, w, b, qn, kvn, lng, lnb, ws, bs, wqt, wk, wvt, pk, ones,
                 cosk, sink, aqt, sqt, seq, tm):
    t, d = x2.shape
    tpb = seq // tm
    hw = MLA_HEADS * HEAD_PAD
    vw = MLA_HEADS * V_PAD
    tok = lambda n: pl.BlockSpec((tm, n), lambda i: (i, 0))
    ktab = pl.BlockSpec((tm, LANES), lambda i: (i % tpb, 0))
    qtab = pl.BlockSpec((HEAD_PAD, tm), lambda i: (0, i % tpb))
    consts = [w, b, qn, kvn, lng, lnb, ws, bs, wqt, wk, wvt]
    return pl.pallas_call(
        _inproj_kernel,
        grid=(t // tm,),
        in_specs=[tok(d), pl.BlockSpec((1, 1, 6, d), lambda i: (layer, i // tpb, 0, 0))]
                 + [_layer_spec(a, layer) for a in consts]
                 + [_const_spec(pk.shape), _const_spec(ones.shape), ktab, ktab, qtab, qtab],
        out_specs=[tok(d), tok(S5_WIDTH), tok(SGU_WIDTH), tok(hw),
                   pl.BlockSpec((1, hw, tm), lambda i: (i, 0, 0)),
                   pl.BlockSpec((1, vw, tm), lambda i: (i, 0, 0))],
        out_shape=[
            jax.ShapeDtypeStruct((t, d), BF16),
            jax.ShapeDtypeStruct((t, S5_WIDTH), F32),
            jax.ShapeDtypeStruct((t, SGU_WIDTH), BF16),
            jax.ShapeDtypeStruct((t, hw), BF16),
            jax.ShapeDtypeStruct((t // tm, hw, tm), BF16),
            jax.ShapeDtypeStruct((t // tm, vw, tm), BF16),
        ],
        compiler_params=_params(("arbitrary",)),
        name="inproj",
    )(x2, ada_all, *consts, pk, ones, cosk, sink, aqt, sqt)


def _flash_kernel(qi_ref, ki_ref, k_ref, qt_ref, vt_ref, o_ref, m_sc, acc_sc):
    p_idx = pl.program_id(1)
    qi = qi_ref[p_idx]
    ki = ki_ref[p_idx]
    tk = k_ref.shape[0]
    tq = qt_ref.shape[2]

    @pl.when(ki == 0)
    def _init():
        m_sc[...] = jnp.full(m_sc.shape, NEG_INF, F32)
        acc_sc[...] = jnp.zeros(acc_sc.shape, F32)

    def step(masked):
        sub = FLASH_SUB
        nkb, nqb = tk // sub, tq // sub
        state = [(h, qb) for h in range(MLA_HEADS) for qb in range(nqb)]
        m_all = {(h, qb): m_sc[h, :, qb * sub:(qb + 1) * sub] for h, qb in state}
        acc_all = {(h, qb): acc_sc[h, :, qb * sub:(qb + 1) * sub] for h, qb in state}
        blocks = [(h, qb, kb) for kb in range(nkb) for h, qb in state]
        if masked:
            blocks = [(h, qb, kb) for h, qb, kb in blocks if kb * sub // CHUNK <= ((qb + 1) * sub - 1) // CHUNK]

        def scores(h, qb, kb):
            cols = slice(h * HEAD_PAD, (h + 1) * HEAD_PAD)
            st = jnp.dot(k_ref[kb * sub:(kb + 1) * sub, cols], qt_ref[0, cols, qb * sub:(qb + 1) * sub],
                         preferred_element_type=F32)
            if masked and ((kb + 1) * sub - 1) // CHUNK > qb * sub // CHUNK:
                kc = (lax.broadcasted_iota(jnp.int32, (sub, sub), 0) + kb * sub) // CHUNK
                qc = (lax.broadcasted_iota(jnp.int32, (sub, sub), 1) + qb * sub) // CHUNK
                st = jnp.where(kc <= qc, st, NEG_INF)
            return st

        def absorb(h, qb, kb, st):
            m_prev = m_all[(h, qb)]
            m_new = jnp.maximum(m_prev, jnp.max(st, axis=0, keepdims=True))
            m_all[(h, qb)] = m_new
            pt = jnp.exp2(st - m_new[0:1]).astype(BF16)
            acc_all[(h, qb)] = (jnp.exp2(m_prev - m_new)[0:1] * acc_all[(h, qb)]
                                + jnp.dot(vt_ref[0, h * V_PAD:(h + 1) * V_PAD, kb * sub:(kb + 1) * sub], pt,
                                          preferred_element_type=F32))

        nblk = len(blocks)
        pending = {i: scores(*blocks[i]) for i in range(min(FLASH_DEPTH, nblk))}
        for i in range(nblk):
            if i + FLASH_DEPTH < nblk:
                pending[i + FLASH_DEPTH] = scores(*blocks[i + FLASH_DEPTH])
            absorb(*blocks[i], pending.pop(i))
        for h, qb in state:
            m_sc[h, :, qb * sub:(qb + 1) * sub] = m_all[(h, qb)]
            acc_sc[h, :, qb * sub:(qb + 1) * sub] = acc_all[(h, qb)]

    @pl.when(ki < qi)
    def _full():
        step(False)

    @pl.when(ki == qi)
    def _diag():
        step(True)
        outs = []
        for h in range(MLA_HEADS):
            acc = acc_sc[h]
            outs.append(acc[:V_HEAD] / acc[V_HEAD:V_HEAD + 1])
        o_ref[...] = jnp.concatenate(outs, axis=0).T.astype(BF16)


def _flash_call(k, qt, vt, batch, seq, tq):
    t, hw = k.shape
    vw = vt.shape[1]
    nq = seq // tq
    pairs = [(a, b) for a in range(nq) for b in range(a + 1)]
    qi_tab = jnp.asarray([p[0] for p in pairs], jnp.int32)
    ki_tab = jnp.asarray([p[1] for p in pairs], jnp.int32)
    grid_spec = pltpu.PrefetchScalarGridSpec(
        num_scalar_prefetch=2,
        grid=(batch, len(pairs)),
        in_specs=[
            pl.BlockSpec((tq, hw), lambda b, p, qt_, kt_: (b * nq + kt_[p], 0)),
            pl.BlockSpec((1, hw, tq), lambda b, p, qt_, kt_: (b * nq + qt_[p], 0, 0)),
            pl.BlockSpec((1, vw, tq), lambda b, p, qt_, kt_: (b * nq + kt_[p], 0, 0)),
        ],
        out_specs=pl.BlockSpec((tq, MLA_HEADS * V_HEAD), lambda b, p, qt_, kt_: (b * nq + qt_[p], 0)),
        scratch_shapes=[
            pltpu.VMEM((MLA_HEADS, SUBLANES, tq), F32),
            pltpu.VMEM((MLA_HEADS, V_PAD, tq), F32),
        ],
    )
    return pl.pallas_call(
        _flash_kernel,
        grid_spec=grid_spec,
        out_shape=jax.ShapeDtypeStruct((t, MLA_HEADS * V_HEAD), BF16),
        compiler_params=_params(("arbitrary", "arbitrary")),
        name="flash",
    )(qi_tab, ki_tab, k, qt, vt)


def _s5_prep_kernel(lr_ref, li_ref, ldt_ref, bt_ref, btsw_ref, cw1_ref, cw2_ref, dsk_ref,
                    dt_ref, wst_ref, wot_ref, ws_sc):
    for q in range(S5_PREP_GROUPS):
        _s5_prep_group(lr_ref.at[q], li_ref.at[q], ldt_ref.at[q], bt_ref.at[q], btsw_ref.at[q], cw1_ref.at[q],
                       cw2_ref.at[q], dsk_ref.at[q], dt_ref.at[q], wst_ref.at[q], wot_ref.at[q], ws_sc)


def _s5_prep_group(lr_ref, li_ref, ldt_ref, bt_ref, btsw_ref, cw1_ref, cw2_ref, dsk_ref,
                   dt_ref, wst_ref, wot_ref, ws_sc):
    pw = 2 * S5_STATE
    lr = lr_ref[...]
    li = li_ref[...]
    dt = jnp.exp(ldt_ref[...])
    lane = lax.broadcasted_iota(jnp.int32, (1, pw), 1)
    sgn = jnp.where(lane < S5_STATE, -1.0, 1.0)
    nk = CHUNK + SUBLANES
    kk = lax.broadcasted_iota(jnp.int32, (nk, pw), 0).astype(F32)
    mag = jnp.exp(kk * (lr * dt))
    ang = kk * (li * dt)
    ar = mag * jnp.cos(ang)
    ai = mag * jnp.sin(ang)
    ais = ai * sgn
    den = lr * lr + li * li
    a1r = ar[1:2]
    a1i = ai[1:2]
    f_re = ((a1r - 1.0) * lr + a1i * li) / den
    f_im = (a1i * lr - (a1r - 1.0) * li) / den
    bt = bt_ref[...]
    btsw = btsw_ref[...]
    bb = f_re * bt + (f_im * sgn) * btsw
    bbsw = f_re * btsw - (f_im * sgn) * bt
    cw1 = cw1_ref[...]
    cw2 = cw2_ref[...]
    for s in range(CHUNK):
        k = CHUNK - 1 - s
        ws_sc[s * S5_GROUP:(s + 1) * S5_GROUP, :] = ar[k:k + 1] * bb + ais[k:k + 1] * bbsw
        wot_ref[s * S5_GROUP:(s + 1) * S5_GROUP, :] = (ar[s + 1:s + 2] * cw1 + ai[s + 1:s + 2] * cw2).astype(BF16)
    ws = ws_sc[...]
    wst_ref[...] = ws.T.astype(BF16)
    strip = lax.dot_general(cw1, ws, NT_DIMS, preferred_element_type=F32, precision=HI)
    row = lax.broadcasted_iota(jnp.int32, (S5_GROUP, S5_FLAT), 0)
    col = lax.broadcasted_iota(jnp.int32, (S5_GROUP, S5_FLAT), 1)
    dsk = jnp.concatenate([dsk_ref[...]] * (S5_FLAT // LANES), axis=1)
    strip = strip + jnp.where(col == row + (S5_FLAT - S5_GROUP), dsk, 0.0)
    strip = jnp.concatenate([strip, jnp.zeros((S5_GROUP, S5_BLK), F32)], axis=1)
    for d in range(S5_NSUB):
        for tt in range(S5_SUB):
            i0 = CHUNK - 1 - S5_SUB * d - tt
            dt_ref[d, tt * S5_GROUP:(tt + 1) * S5_GROUP, :] = (
                strip[:, S5_GROUP * i0:S5_GROUP * i0 + S5_BLK].astype(BF16))


def _s5_prep_call(lr2, li2, ldt2, bt, btsw, cw1, cw2, dsk):
    n = lr2.shape[0]
    pw = 2 * S5_STATE
    ng = S5_PREP_GROUPS
    vec = pl.BlockSpec((ng, 1, pw), lambda i: (i, 0, 0))
    mat = pl.BlockSpec((ng, S5_GROUP, pw), lambda i: (i, 0, 0))
    return pl.pallas_call(
        _s5_prep_kernel,
        grid=(n // ng,),
        in_specs=[vec, vec, vec, mat, mat, mat, mat, mat],
        out_specs=[
            pl.BlockSpec((ng, S5_NSUB, S5_BLK, S5_BLK), lambda i: (i, 0, 0, 0)),
            pl.BlockSpec((ng, pw, S5_FLAT), lambda i: (i, 0, 0)),
            pl.BlockSpec((ng, S5_FLAT, pw), lambda i: (i, 0, 0)),
        ],
        out_shape=[
            jax.ShapeDtypeStruct((n, S5_NSUB, S5_BLK, S5_BLK), BF16),
            jax.ShapeDtypeStruct((n, pw, S5_FLAT), BF16),
            jax.ShapeDtypeStruct((n, S5_FLAT, pw), BF16),
        ],
        scratch_shapes=[pltpu.VMEM((S5_FLAT, pw), F32)],
        compiler_params=_params(("arbitrary",)),
        name="s5_prep",
    )(lr2, li2, ldt2, bt, btsw, cw1, cw2, dsk)


def _s5_kernel(u_ref, dt_ref, wst_ref, wot_ref, pa_ref, pb_ref, y_ref, ut_sc, yt_sc, *, nch):
    gg = pl.program_id(1)
    ncol = ut_sc.shape[2]
    pw = 2 * S5_STATE

    @pl.when(gg == 0)
    def _regroup():
        for s in range(CHUNK):
            ut_sc[s] = u_ref[pl.ds(s, ncol, stride=CHUNK), :].T.astype(BF16)

    row0 = pl.multiple_of(gg * S5_GROUP, S5_GROUP)
    u = ut_sc[:, pl.ds(row0, S5_GROUP), :].reshape(S5_FLAT, ncol)
    h = jnp.dot(wst_ref[0], u, preferred_element_type=F32)
    n_idx = lax.broadcasted_iota(jnp.int32, (pw, ncol), 1) % nch
    pa = pa_ref[0]
    pb = pb_ref[0]
    step = 1
    j = 0
    while step < nch:
        sh = jnp.where(n_idx >= step, pltpu.roll(h, step, 1), 0.0)
        h = h + pa[:, j:j + 1] * sh + pb[:, j:j + 1] * pltpu.roll(sh, S5_STATE, 0)
        step *= 2
        j += 1
    hprev = jnp.where(n_idx >= 1, pltpu.roll(h, 1, 1), 0.0).astype(BF16)
    for jj in range(S5_NSUB):
        acc = jnp.dot(wot_ref[0, jj * S5_BLK:(jj + 1) * S5_BLK, :], hprev, preferred_element_type=F32)
        for ii in range(jj + 1):
            acc = acc + jnp.dot(dt_ref[0, jj - ii], u[ii * S5_BLK:(ii + 1) * S5_BLK, :],
                                preferred_element_type=F32)
        yt_sc[jj * S5_SUB:(jj + 1) * S5_SUB, pl.ds(row0, S5_GROUP), :] = acc.reshape(S5_SUB, S5_GROUP, ncol)

    @pl.when(gg == S5_GPB - 1)
    def _emit():
        for tt in range(CHUNK):
            y_ref[pl.ds(tt, ncol, stride=CHUNK), :] = yt_sc[tt].T


def _s5_call(u5, dtb, wst, wot, pa, pb, layer, nch):
    t, width = u5.shape
    ncol = t // CHUNK
    nblk = width // LANES
    base = layer * S5_GROUPS
    grp = lambda shape: pl.BlockSpec((1,) + shape, lambda j, g: (base + j * S5_GPB + g,) + (0,) * len(shape))
    return pl.pallas_call(
        functools.partial(_s5_kernel, nch=nch),
        grid=(nblk, S5_GPB),
        in_specs=[
            pl.BlockSpec((t, LANES), lambda j, g: (0, j)),
            grp(dtb.shape[1:]), grp(wst.shape[1:]), grp(wot.shape[1:]), grp(pa.shape[1:]), grp(pb.shape[1:]),
        ],
        out_specs=pl.BlockSpec((t, LANES), lambda j, g: (0, j)),
        out_shape=jax.ShapeDtypeStruct((t, width), F32),
        scratch_shapes=[pltpu.VMEM((CHUNK, LANES, ncol), BF16), pltpu.VMEM((CHUNK, LANES, ncol), F32)],
        compiler_params=_params(("arbitrary", "arbitrary")),
        name="s5",
    )(u5, dtb, wst, wot, pa, pb)


def _merge_kernel(h_ref, yssm_ref, ymla_ref, ysgu_ref, x_ref, ada_ref,
                  wg_ref, bg_ref, wglu_ref, bglu_ref, wb_ref, wout_ref, lng_ref, lnb_ref,
                  o_ref, *, alpha):
    tm, d = x_ref.shape
    sub = MERGE_SUB
    g1 = ada_ref[0, 0][2:3]

    def merge(r):
        rows = slice(r * sub, (r + 1) * sub)
        h = h_ref[rows, :]
        z = _gelu(yssm_ref[rows, :])
        y5 = (z * _sigmoid(jnp.dot(z.astype(BF16), wglu_ref[0], preferred_element_type=F32)
                           + bglu_ref[0])).astype(BF16)
        branches = (y5, ymla_ref[rows, :], ysgu_ref[rows, :])
        merged = None
        for i in range(N_BRANCH):
            cols = slice(i * d, (i + 1) * d)
            gate = _sigmoid(jnp.dot(h, wg_ref[0, :, cols], preferred_element_type=F32) + bg_ref[0, :, cols])
            term = gate * jnp.dot(branches[i], wb_ref[0, i], preferred_element_type=F32)
            merged = term if merged is None else merged + term
        return merged.astype(BF16)

    def project(r, merged):
        rows = slice(r * sub, (r + 1) * sub)
        y = jnp.dot(merged, wout_ref[0], preferred_element_type=F32)
        o_ref[rows, :] = _layer_norm(alpha * x_ref[rows, :] + (1.0 + g1) * y, lng_ref[0], lnb_ref[0])

    nsub = tm // sub
    merged = merge(0)
    for r in range(nsub):
        nxt = merge(r + 1) if r + 1 < nsub else None
        project(r, merged)
        merged = nxt


def _merge_call(h, yssm, ymla, ysgu, x2, ada_all, layer, wg, bg, wglu, bglu, wb, wout, lng, lnb, seq, tm, alpha):
    t, d = x2.shape
    tpb = seq // tm
    tok = lambda n: pl.BlockSpec((tm, n), lambda i: (i, 0))
    consts = [wg, bg, wglu, bglu, wb, wout, lng, lnb]
    return pl.pallas_call(
        functools.partial(_merge_kernel, alpha=alpha),
        grid=(t // tm,),
        in_specs=[tok(d), tok(S5_WIDTH), tok(BRANCH_WIDTH), tok(BRANCH_WIDTH), tok(d),
                  pl.BlockSpec((1, 1, 6, d), lambda i: (layer, i // tpb, 0, 0))]
                 + [_layer_spec(a, layer) for a in consts],
        out_specs=tok(d),
        out_shape=jax.ShapeDtypeStruct((t, d), F32),
        compiler_params=_params(("arbitrary",)),
        name="merge",
    )(h, yssm, ymla, ysgu, x2, ada_all, *consts)


def _ffn_kernel(x_ref, ada_ref, wa_ref, wb_ref, wo_ref, lng_ref, lnb_ref, o_ref, *, alpha):
    tm = x_ref.shape[0]
    sub = FFN_SUB
    ada = ada_ref[0, 0]

    def hidden(r):
        rows = slice(r * sub, (r + 1) * sub)
        h = (x_ref[rows, :] * (1.0 + ada[4:5]) + ada[3:4]).astype(BF16)
        a = jnp.dot(h, wa_ref[0], preferred_element_type=F32)
        b = jnp.dot(h, wb_ref[0], preferred_element_type=F32)
        return (a * _sigmoid(a) * b).astype(BF16)

    def project(r, act):
        rows = slice(r * sub, (r + 1) * sub)
        f = jnp.dot(act, wo_ref[0], preferred_element_type=F32)
        o_ref[rows, :] = _layer_norm(alpha * x_ref[rows, :] + (1.0 + ada[5:6]) * f, lng_ref[0], lnb_ref[0])

    nsub = tm // sub
    act = hidden(0)
    for r in range(nsub):
        nxt = hidden(r + 1) if r + 1 < nsub else None
        project(r, act)
        act = nxt


def _ffn_call(x2, ada_all, layer, w_in, w_out, lng, lnb, seq, tm, alpha):
    t, d = x2.shape
    ff = w_out.shape[1]
    tpb = seq // tm
    tok = pl.BlockSpec((tm, d), lambda i: (i, 0))
    return pl.pallas_call(
        functools.partial(_ffn_kernel, alpha=alpha),
        grid=(t // tm,),
        in_specs=[tok, pl.BlockSpec((1, 1, 6, d), lambda i: (layer, i // tpb, 0, 0)),
                  _layer_spec(w_in, layer, (ff, 0)), _layer_spec(w_in, layer, (ff, 1)), _layer_spec(w_out, layer),
                  _layer_spec(lng, layer), _layer_spec(lnb, layer)],
        out_specs=tok,
        out_shape=jax.ShapeDtypeStruct((t, d), F32),
        compiler_params=_params(("arbitrary",)),
        name="ffn",
    )(x2, ada_all, w_in, w_in, w_out, lng, lnb)


def _s5_prep_inputs(lam_re, lam_im, log_dt, b_re, b_im, c_re, c_im, dskip):
    depth, g, p = lam_re.shape
    n = depth * g
    dup = lambda a: jnp.concatenate([a, a], axis=-1).reshape(n, 1, 2 * p)
    lr2 = dup(lam_re)
    li2 = dup(lam_im)
    ldt2 = jnp.broadcast_to(log_dt[..., None], (depth, g, 2 * p)).reshape(n, 1, 2 * p)
    btr = b_re.transpose(0, 1, 3, 2).reshape(n, S5_GROUP, p)
    bti = b_im.transpose(0, 1, 3, 2).reshape(n, S5_GROUP, p)
    bt = jnp.concatenate([btr, bti], axis=-1)
    btsw = jnp.concatenate([bti, btr], axis=-1)
    cr = c_re.reshape(n, S5_GROUP, p)
    ci = c_im.reshape(n, S5_GROUP, p)
    cw1 = jnp.concatenate([cr, -ci], axis=-1)
    cw2 = jnp.concatenate([-ci, -cr], axis=-1)
    dsk = jnp.broadcast_to(dskip.reshape(n, S5_GROUP, 1), (n, S5_GROUP, LANES))
    return lr2, li2, ldt2, bt, btsw, cw1, cw2, dsk


def _s5_scan_multipliers(lam_re, lam_im, log_dt, nch):
    depth, g, p = lam_re.shape
    dt = jnp.exp(log_dt)[..., None, None]
    nsteps = max(1, (nch - 1).bit_length())
    k = jnp.asarray(CHUNK * (2.0 ** np.arange(nsteps)), F32)
    mag = jnp.exp(lam_re[..., None] * dt * k)
    ang = lam_im[..., None] * dt * k
    sr = (mag * jnp.cos(ang)).reshape(depth * g, p, nsteps)
    si = (mag * jnp.sin(ang)).reshape(depth * g, p, nsteps)
    pad = ((0, 0), (0, 0), (0, S5_SCAN_COLS - nsteps))
    pa = jnp.pad(jnp.concatenate([sr, sr], axis=1), pad)
    pb = jnp.pad(jnp.concatenate([-si, si], axis=1), pad)
    return pa, pb


def _rope_tables(seq, scale):
    inv_freq = 1.0 / (ROPE_THETA ** (jnp.arange(0, QK_ROPE, 2, dtype=F32) / QK_ROPE))
    ang = jnp.arange(seq, dtype=F32)[:, None] * inv_freq[None, :]
    cos = jnp.cos(ang)
    sin = jnp.sin(ang)
    zk = jnp.zeros((seq, LANES - QK_ROPE), F32)
    cosk = jnp.concatenate([cos, cos, zk], axis=1)
    sink = jnp.concatenate([-sin, sin, zk], axis=1)
    zq = jnp.zeros((seq, HEAD_PAD - QK_NOPE - QK_ROPE), F32)
    aq = jnp.concatenate([jnp.full((seq, QK_NOPE), scale, F32), scale * cos, scale * cos, zq], axis=1)
    sq = jnp.concatenate([jnp.zeros((seq, QK_NOPE), F32), -scale * sin, scale * sin, zq], axis=1)
    return cosk, sink, aq.T, sq.T


def _swap_halves(w):
    half = w.shape[-1] // 2
    return jnp.concatenate([w[..., half:], w[..., :half]], axis=-1)


def kernel(x, c, w_ada, b_ada, w_in, b_in, s5_lambda_re, s5_lambda_im, s5_log_dt, s5_b_re, s5_b_im, s5_c_re, s5_c_im, s5_d, s5_w_glu, s5_b_glu, mla_q_norm, mla_w_q_up, mla_kv_norm, mla_w_kv_up, sgu_ln_g, sgu_ln_b, sgu_w_s, sgu_b_s, w_branch, w_out, ln1_g, ln1_b, ffn_w_in, ffn_w_out, ln2_g, ln2_b):
    batch, seq, d = x.shape
    depth = w_ada.shape[0]
    t = batch * seq
    alpha = float((2 * depth) ** 0.25)
    tm = 512
    tq = 512
    nch = seq // CHUNK
    assert seq % tm == 0 and tm == tq and tm % SGU_CHUNK == 0 and (batch * nch) % LANES == 0

    c_pad = jnp.pad(c, ((0, -batch % SUBLANES), (0, 0)))
    ada_all = _ada_call(c_pad, w_ada, b_ada)[:, :batch].reshape(depth, batch, 6, d)

    scale = LOG2E * (QK_NOPE + QK_ROPE) ** -0.5
    cosk, sink, aqt, sqt = _rope_tables(seq, scale)

    offs = np.cumsum([0, S5_WIDTH, Q_LORA, KV_LORA, QK_ROPE, SGU_WIDTH, SGU_WIDTH])
    o_u5, o_cq, o_ckv, o_kpe, o_usgu, o_vsgu, o_gate = [int(o) for o in offs]
    row = lambda a: a[:, None, :]

    kpe_b = b_in[:, o_kpe:o_kpe + QK_ROPE]
    w_a, w_g = _win_prep_call(w_in, o_kpe, o_usgu, o_gate)
    b_a = row(jnp.concatenate([b_in[:, o_u5:o_kpe], b_in[:, o_usgu:o_gate], kpe_b, _swap_halves(kpe_b),
                               jnp.zeros((depth, LANES - 2 * QK_ROPE), F32)], axis=1))
    b_g = row(b_in[:, o_gate:])
    pos = np.arange(SGU_CHUNK) // CHUNK
    sgu_mask = jnp.asarray(pos[None, :] <= pos[:, None])
    ws_m = jnp.where(sgu_mask[None, None], sgu_w_s, 0.0).astype(BF16)
    bs_full = jnp.repeat(sgu_b_s.transpose(0, 2, 1), SGU_WIDTH // SGU_GROUPS, axis=2)

    hw = MLA_HEADS * HEAD_PAD
    vw = MLA_HEADS * V_PAD
    wq4 = mla_w_q_up.reshape(depth, Q_LORA, MLA_HEADS, QK_NOPE + QK_ROPE)
    wqt = (jnp.concatenate([wq4, _swap_halves(wq4[..., QK_NOPE:])], axis=3)
           .reshape(depth, Q_LORA, hw).transpose(0, 2, 1).astype(BF16))
    wkv4 = mla_w_kv_up.reshape(depth, KV_LORA, MLA_HEADS, QK_NOPE + V_HEAD)
    wk = jnp.concatenate([wkv4[..., :QK_NOPE], jnp.zeros((depth, KV_LORA, MLA_HEADS, HEAD_PAD - QK_NOPE), F32)],
                         axis=3).reshape(depth, KV_LORA, hw).astype(BF16)
    wvt = (jnp.concatenate([wkv4[..., QK_NOPE:], jnp.zeros((depth, KV_LORA, MLA_HEADS, V_PAD - V_HEAD), F32)],
                           axis=3).reshape(depth, KV_LORA, vw).transpose(0, 2, 1).astype(BF16))
    place = np.zeros((LANES, hw), np.float32)
    for hh in range(MLA_HEADS):
        place[np.arange(QK_ROPE), hh * HEAD_PAD + QK_NOPE + np.arange(QK_ROPE)] = 1.0
    pk = jnp.asarray(place, BF16)
    ones_col = np.zeros((vw, 1), np.float32)
    ones_col[np.arange(MLA_HEADS) * V_PAD + V_HEAD, 0] = 1.0
    ones_col = jnp.asarray(ones_col)

    wglu = s5_w_glu.astype(BF16)
    wbr = w_branch.astype(BF16)
    wout = w_out.astype(BF16)
    ffn_in = ffn_w_in.astype(BF16)
    ffn_out = ffn_w_out.astype(BF16)

    dtb, wst, wot = _s5_prep_call(*_s5_prep_inputs(s5_lambda_re, s5_lambda_im, s5_log_dt, s5_b_re, s5_b_im,
                                                   s5_c_re, s5_c_im, s5_d))
    pa, pb = _s5_scan_multipliers(s5_lambda_re, s5_lambda_im, s5_log_dt, nch)

    x2 = x.reshape(t, d)
    for l in range(depth):
        h, u5, ysgu, kmat, qt, vt = _inproj_call(
            x2, ada_all, l, w_a, b_a, row(mla_q_norm), row(mla_kv_norm), row(sgu_ln_g), row(sgu_ln_b),
            ws_m, bs_full, wqt, wk, wvt, pk, ones_col, cosk, sink, aqt, sqt, seq, tm)
        ymla = _flash_call(kmat, qt, vt, batch, seq, tq)
        yssm = _s5_call(u5, dtb, wst, wot, pa, pb, l, nch)
        x2 = _merge_call(h, yssm, ymla, ysgu, x2, ada_all, l, w_g, b_g, wglu, row(s5_b_glu), wbr, wout,
                         row(ln1_g), row(ln1_b), seq, tm, alpha)
        x2 = _ffn_call(x2, ada_all, l, ffn_in, ffn_out, row(ln2_g), row(ln2_b), seq, tm, alpha)
    return x2.reshape(batch, seq, d)
```

```python
import functools
import math

import numpy as np
import jax
import jax.numpy as jnp
from jax import lax
from jax.experimental import pallas as pl
from jax.experimental.pallas import tpu as pltpu

F32 = jnp.float32
BF16 = jnp.bfloat16

CHUNK = 64
S5_WIDTH = 512
S5_GROUP = 16
S5_GROUPS = S5_WIDTH // S5_GROUP
S5_STATE = 64
MLA_HEADS = 8
QK_NOPE = 64
QK_ROPE = 32
V_HEAD = 64
Q_LORA = 384
KV_LORA = 256
ROPE_THETA = 10000.0
SGU_WIDTH = 512
SGU_GROUPS = 4
SGU_CHUNK = 128
N_BRANCH = 3
BRANCH_WIDTH = 512
LN_EPS = 1e-5
RMS_EPS = 1e-6
NEG_INF = -1e30

LANES = 128
SUBLANES = 8
BF16_ROWS = 16
HEAD_PAD = 128
V_PAD = V_HEAD + BF16_ROWS
S5_SUB = 16
S5_NSUB = CHUNK // S5_SUB
S5_BLK = S5_SUB * S5_GROUP
S5_FLAT = CHUNK * S5_GROUP
S5_GPB = LANES // S5_GROUP
S5_SCAN_COLS = 8
INPROJ_SUB = 256
MERGE_SUB = 256
FFN_SUB = 256
S5_PREP_GROUPS = 4
FLASH_SUB = 256
FLASH_TK = 1024
FLASH_DEPTH = 6

OFF_U5 = 0
OFF_CQ = OFF_U5 + S5_WIDTH
OFF_CKV = OFF_CQ + Q_LORA
OFF_USGU = OFF_CKV + KV_LORA
OFF_VSGU = OFF_USGU + SGU_WIDTH
OFF_KPE = OFF_VSGU + SGU_WIDTH
NA = OFF_KPE + LANES

VMEM_LIMIT = 56 * 1024 * 1024

HI = lax.Precision.HIGHEST
NT_DIMS = (((1,), (1,)), ((), ()))
LOG2E = 1.4426950408889634


def _gelu(x):
    return 0.5 * x * (1.0 + jnp.tanh(0.7978845608028654 * (x + 0.044715 * (x * x * x))))


def _sigmoid(x):
    return 0.5 * jnp.tanh(0.5 * x) + 0.5


def _layer_norm(r, g, b):
    mu = jnp.mean(r, axis=-1, keepdims=True)
    rc = r - mu
    var = jnp.mean(rc * rc, axis=-1, keepdims=True)
    return rc * lax.rsqrt(var + LN_EPS) * g + b


def _const_spec(shape):
    nd = len(shape)
    return pl.BlockSpec(shape, lambda *_: (0,) * nd, pipeline_mode=pl.Buffered(1))


def _layer_spec(arr, layer, col_block=None):
    shape = (1,) + arr.shape[1:]
    idx = (layer,) + (0,) * (arr.ndim - 1)
    if col_block is not None:
        width, j = col_block
        shape = shape[:-1] + (width,)
        idx = idx[:-1] + (j,)
    return pl.BlockSpec(shape, lambda *_: idx, pipeline_mode=pl.Buffered(1))


def _params(sem):
    return pltpu.CompilerParams(dimension_semantics=sem, vmem_limit_bytes=VMEM_LIMIT)


def _ada_kernel(c_ref, w_ref, b_ref, o_ref):
    c = c_ref[...]
    ca = c * _sigmoid(c)
    o_ref[0] = jnp.dot(ca, w_ref[0], preferred_element_type=F32, precision=HI) + b_ref[0]


def _ada_call(c_pad, w_ada, b_ada):
    depth, d, n6 = w_ada.shape
    tn = n6 // 4
    rows = c_pad.shape[0]
    return pl.pallas_call(
        _ada_kernel,
        grid=(depth, n6 // tn),
        in_specs=[
            pl.BlockSpec((rows, d), lambda l, j: (0, 0)),
            pl.BlockSpec((1, d, tn), lambda l, j: (l, 0, j)),
            pl.BlockSpec((1, 1, tn), lambda l, j: (l, 0, j)),
        ],
        out_specs=pl.BlockSpec((1, rows, tn), lambda l, j: (l, 0, j)),
        out_shape=jax.ShapeDtypeStruct((depth, rows, n6), F32),
        compiler_params=_params(("arbitrary", "arbitrary")),
        name="ada",
    )(c_pad, w_ada, b_ada.reshape(depth, 1, n6))


def _win_prep_kernel(w_ref, wa_ref, wg_ref, *, o_kpe, o_usgu, o_gate):
    w = w_ref[0]
    half = QK_ROPE // 2
    kpe = w[:, o_kpe:o_kpe + QK_ROPE]
    last = jnp.concatenate([kpe, kpe[:, half:], kpe[:, :half],
                            jnp.zeros((w.shape[0], LANES - 2 * QK_ROPE), F32)], axis=1)
    wa_ref[0, :, 0:o_kpe] = w[:, 0:o_kpe].astype(BF16)
    wa_ref[0, :, o_kpe:OFF_KPE] = w[:, o_usgu:o_gate].astype(BF16)
    wa_ref[0, :, OFF_KPE:NA] = last.astype(BF16)
    wg_ref[0] = w[:, o_gate:].astype(BF16)


def _win_prep_call(w_in, o_kpe, o_usgu, o_gate):
    depth, d, n_in = w_in.shape
    tr = 256
    assert o_kpe == OFF_USGU and o_kpe + (o_gate - o_usgu) == OFF_KPE
    return pl.pallas_call(
        functools.partial(_win_prep_kernel, o_kpe=o_kpe, o_usgu=o_usgu, o_gate=o_gate),
        grid=(depth, d // tr),
        in_specs=[pl.BlockSpec((1, tr, n_in), lambda l, i: (l, i, 0))],
        out_specs=[pl.BlockSpec((1, tr, NA), lambda l, i: (l, i, 0)),
                   pl.BlockSpec((1, tr, n_in - o_gate), lambda l, i: (l, i, 0))],
        out_shape=[jax.ShapeDtypeStruct((depth, d, NA), BF16),
                   jax.ShapeDtypeStruct((depth, d, n_in - o_gate), BF16)],
        compiler_params=_params(("arbitrary", "arbitrary")),
        name="win_prep",
    )(w_in)


def _inproj_kernel(x_ref, ada_ref, w_ref, b_ref, qn_ref, kvn_ref, lng_ref, lnb_ref, ws_ref, bs_ref,
                   wqt_ref, wk_ref, wvt_ref, pk_ref, ones_ref, cosk_ref, sink_ref, aqt_ref, sqt_ref,
                   h_ref, u5_ref, ysgu_ref, k_ref, qt_ref, vt_ref):
    tm = x_ref.shape[0]
    ada = ada_ref[0, 0]
    sub = INPROJ_SUB
    gw = SGU_WIDTH // SGU_GROUPS

    def project(r):
        rows = slice(r * sub, (r + 1) * sub)
        h = (x_ref[rows, :] * (1.0 + ada[1:2]) + ada[0:1]).astype(BF16)
        h_ref[rows, :] = h
        return jnp.dot(h, w_ref[0], preferred_element_type=F32) + b_ref[0]

    def epilogue(r, acc):
        rows = slice(r * sub, (r + 1) * sub)
        u5_ref[rows, :] = acc[:, OFF_U5:OFF_U5 + S5_WIDTH]

        cq = acc[:, OFF_CQ:OFF_CQ + Q_LORA]
        cqn = (cq * lax.rsqrt(jnp.mean(cq * cq, axis=-1, keepdims=True) + RMS_EPS) * qn_ref[0]).astype(BF16)
        ckv = acc[:, OFF_CKV:OFF_CKV + KV_LORA]
        ckvn = (ckv * lax.rsqrt(jnp.mean(ckv * ckv, axis=-1, keepdims=True) + RMS_EPS) * kvn_ref[0]).astype(BF16)

        kb = acc[:, OFF_KPE:OFF_KPE + LANES]
        krot = (kb * cosk_ref[rows, :] + pltpu.roll(kb, LANES - QK_ROPE, 1) * sink_ref[rows, :]).astype(BF16)

        k_ref[rows, :] = (jnp.dot(ckvn, wk_ref[0], preferred_element_type=F32)
                          + jnp.dot(krot, pk_ref[...], preferred_element_type=F32)).astype(BF16)
        qt = lax.dot_general(wqt_ref[0], cqn, NT_DIMS, preferred_element_type=F32)
        aqt = aqt_ref[:, rows]
        sqt = sqt_ref[:, rows]
        for h in range(MLA_HEADS):
            hrow = slice(h * HEAD_PAD, (h + 1) * HEAD_PAD)
            qh = qt[hrow, :]
            qt_ref[0, hrow, rows] = (qh * aqt + pltpu.roll(qh, HEAD_PAD - QK_ROPE, 0) * sqt).astype(BF16)
        vt = lax.dot_general(wvt_ref[0], ckvn, NT_DIMS, preferred_element_type=F32) + ones_ref[...]
        vt_ref[0, :, rows] = vt.astype(BF16)

        us = _gelu(acc[:, OFF_USGU:OFF_USGU + SGU_WIDTH])
        vn = _layer_norm(_gelu(acc[:, OFF_VSGU:OFF_VSGU + SGU_WIDTH]), lng_ref[0], lnb_ref[0]).astype(BF16)
        for c in range(sub // SGU_CHUNK):
            crow = slice(c * SGU_CHUNK, (c + 1) * SGU_CHUNK)
            parts = [jnp.dot(ws_ref[0, g], vn[crow, g * gw:(g + 1) * gw], preferred_element_type=F32)
                     for g in range(SGU_GROUPS)]
            mixed = jnp.concatenate(parts, axis=1) + bs_ref[0]
            ysgu_ref[r * sub + c * SGU_CHUNK:r * sub + (c + 1) * SGU_CHUNK, :] = (us[crow, :] * mixed).astype(BF16)

    nsub = tm // sub
    acc = project(0)
    for r in range(nsub):
        nxt = project(r + 1) if r + 1 < nsub else None
        epilogue(r, acc)
        acc = nxt


def _inproj_call(x2, ada_all, layer, w, b, qn, kvn, lng, lnb, ws, bs, wqt, wk, wvt, pk, ones,
                 cosk, sink, aqt, sqt, seq, tm):
    t, d = x2.shape
    tpb = seq // tm
    hw = MLA_HEADS * HEAD_PAD
    vw = MLA_HEADS * V_PAD
    tok = lambda n: pl.BlockSpec((tm, n), lambda i: (i, 0))
    ktab = pl.BlockSpec((tm, LANES), lambda i: (i % tpb, 0))
    qtab = pl.BlockSpec((HEAD_PAD, tm), lambda i: (0, i % tpb))
    consts = [w, b, qn, kvn, lng, lnb, ws, bs, wqt, wk, wvt]
    return pl.pallas_call(
        _inproj_kernel,
        grid=(t // tm,),
        in_specs=[tok(d), pl.BlockSpec((1, 1, 6, d), lambda i: (layer, i // tpb, 0, 0))]
                 + [_layer_spec(a, layer) for a in consts]
                 + [_const_spec(pk.shape), _const_spec(ones.shape), ktab, ktab, qtab, qtab],
        out_specs=[tok(d), tok(S5_WIDTH), tok(SGU_WIDTH), tok(hw),
                   pl.BlockSpec((1, hw, tm), lambda i: (i, 0, 0)),
                   pl.BlockSpec((1, vw, tm), lambda i: (i, 0, 0))],
        out_shape=[
            jax.ShapeDtypeStruct((t, d), BF16),
            jax.ShapeDtypeStruct((t, S5_WIDTH), F32),
            jax.ShapeDtypeStruct((t, SGU_WIDTH), BF16),
            jax.ShapeDtypeStruct((t, hw), BF16),
            jax.ShapeDtypeStruct((t // tm, hw, tm), BF16),
            jax.ShapeDtypeStruct((t // tm, vw, tm), BF16),
        ],
        compiler_params=_params(("arbitrary",)),
        name="inproj",
    )(x2, ada_all, *consts, pk, ones, cosk, sink, aqt, sqt)


def _flash_kernel(qi_ref, ki_ref, kind_ref, k_ref, qt_ref, vt_ref, o_ref, m_sc, acc_sc):
    p_idx = pl.program_id(1)
    ki = ki_ref[p_idx]
    kind = kind_ref[p_idx]
    tk = k_ref.shape[0]
    tq = qt_ref.shape[2]
    kslabs = tk // tq

    @pl.when(ki == 0)
    def _init():
        m_sc[...] = jnp.full(m_sc.shape, NEG_INF, F32)
        acc_sc[...] = jnp.zeros(acc_sc.shape, F32)

    def step(q_off):
        sub = FLASH_SUB
        nkb, nqb = tk // sub, tq // sub
        state = [(h, qb) for h in range(MLA_HEADS) for qb in range(nqb)]
        m_all = {(h, qb): m_sc[h, :, qb * sub:(qb + 1) * sub] for h, qb in state}
        acc_all = {(h, qb): acc_sc[h, :, qb * sub:(qb + 1) * sub] for h, qb in state}
        blocks = [(h, qb, kb) for kb in range(nkb) for h, qb in state]
        if q_off is not None:
            blocks = [(h, qb, kb) for h, qb, kb in blocks
                      if kb * sub // CHUNK <= (q_off + (qb + 1) * sub - 1) // CHUNK]

        def scores(h, qb, kb):
            cols = slice(h * HEAD_PAD, (h + 1) * HEAD_PAD)
            st = jnp.dot(k_ref[kb * sub:(kb + 1) * sub, cols], qt_ref[0, cols, qb * sub:(qb + 1) * sub],
                         preferred_element_type=F32)
            if q_off is not None and ((kb + 1) * sub - 1) // CHUNK > (q_off + qb * sub) // CHUNK:
                kc = (lax.broadcasted_iota(jnp.int32, (sub, sub), 0) + kb * sub) // CHUNK
                qc = (lax.broadcasted_iota(jnp.int32, (sub, sub), 1) + q_off + qb * sub) // CHUNK
                st = jnp.where(kc <= qc, st, NEG_INF)
            return st

        def absorb(h, qb, kb, st):
            m_prev = m_all[(h, qb)]
            m_new = jnp.maximum(m_prev, jnp.max(st, axis=0, keepdims=True))
            m_all[(h, qb)] = m_new
            pt = jnp.exp2(st - m_new[0:1]).astype(BF16)
            slab, col = divmod(kb * sub, tq)
            acc_all[(h, qb)] = (jnp.exp2(m_prev - m_new)[0:1] * acc_all[(h, qb)]
                                + jnp.dot(vt_ref[slab, h * V_PAD:(h + 1) * V_PAD, col:col + sub], pt,
                                          preferred_element_type=F32))

        nblk = len(blocks)
        pending = {i: scores(*blocks[i]) for i in range(min(FLASH_DEPTH, nblk))}
        for i in range(nblk):
            if i + FLASH_DEPTH < nblk:
                pending[i + FLASH_DEPTH] = scores(*blocks[i + FLASH_DEPTH])
            absorb(*blocks[i], pending.pop(i))
        for h, qb in state:
            m_sc[h, :, qb * sub:(qb + 1) * sub] = m_all[(h, qb)]
            acc_sc[h, :, qb * sub:(qb + 1) * sub] = acc_all[(h, qb)]

    def finish():
        outs = []
        for h in range(MLA_HEADS):
            acc = acc_sc[h]
            outs.append(acc[:V_HEAD] / acc[V_HEAD:V_HEAD + 1])
        o_ref[...] = jnp.concatenate(outs, axis=0).T.astype(BF16)

    @pl.when(kind == 0)
    def _below():
        step(None)

    for slab in range(kslabs):
        @pl.when(kind == slab + 1)
        def _diag(slab=slab):
            step(slab * tq)
            finish()


def _flash_call(k, qt, vt, batch, seq, tq, tk):
    t, hw = k.shape
    vw = vt.shape[1]
    nq = seq // tq
    nk = seq // tk
    kslabs = tk // tq
    steps = [(a, b, 0 if b < a // kslabs else a % kslabs + 1) for a in range(nq) for b in range(a // kslabs + 1)]
    qi_tab, ki_tab, kind_tab = (jnp.asarray([s[i] for s in steps], jnp.int32) for i in range(3))
    grid_spec = pltpu.PrefetchScalarGridSpec(
        num_scalar_prefetch=3,
        grid=(batch, len(steps)),
        in_specs=[
            pl.BlockSpec((tk, hw), lambda b, p, qt_, kt_, kd_: (b * nk + kt_[p], 0)),
            pl.BlockSpec((1, hw, tq), lambda b, p, qt_, kt_, kd_: (b * nq + qt_[p], 0, 0)),
            pl.BlockSpec((kslabs, vw, tq), lambda b, p, qt_, kt_, kd_: (b * nk + kt_[p], 0, 0)),
        ],
        out_specs=pl.BlockSpec((tq, MLA_HEADS * V_HEAD), lambda b, p, qt_, kt_, kd_: (b * nq + qt_[p], 0)),
        scratch_shapes=[
            pltpu.VMEM((MLA_HEADS, SUBLANES, tq), F32),
            pltpu.VMEM((MLA_HEADS, V_PAD, tq), F32),
        ],
    )
    return pl.pallas_call(
        _flash_kernel,
        grid_spec=grid_spec,
        out_shape=jax.ShapeDtypeStruct((t, MLA_HEADS * V_HEAD), BF16),
        compiler_params=_params(("arbitrary", "arbitrary")),
        name="flash",
    )(qi_tab, ki_tab, kind_tab, k, qt, vt)


def _s5_prep_kernel(lr_ref, li_ref, ldt_ref, bt_ref, btsw_ref, cw1_ref, cw2_ref, dsk_ref,
                    dt_ref, wst_ref, wot_ref, ws_sc):
    for q in range(S5_PREP_GROUPS):
        _s5_prep_group(lr_ref.at[q], li_ref.at[q], ldt_ref.at[q], bt_ref.at[q], btsw_ref.at[q], cw1_ref.at[q],
                       cw2_ref.at[q], dsk_ref.at[q], dt_ref.at[q], wst_ref.at[q], wot_ref.at[q], ws_sc)


def _s5_prep_group(lr_ref, li_ref, ldt_ref, bt_ref, btsw_ref, cw1_ref, cw2_ref, dsk_ref,
                   dt_ref, wst_ref, wot_ref, ws_sc):
    pw = 2 * S5_STATE
    lr = lr_ref[...]
    li = li_ref[...]
    dt = jnp.exp(ldt_ref[...])
    lane = lax.broadcasted_iota(jnp.int32, (1, pw), 1)
    sgn = jnp.where(lane < S5_STATE, -1.0, 1.0)
    nk = CHUNK + SUBLANES
    kk = lax.broadcasted_iota(jnp.int32, (nk, pw), 0).astype(F32)
    mag = jnp.exp(kk * (lr * dt))
    ang = kk * (li * dt)
    ar = mag * jnp.cos(ang)
    ai = mag * jnp.sin(ang)
    ais = ai * sgn
    den = lr * lr + li * li
    a1r = ar[1:2]
    a1i = ai[1:2]
    f_re = ((a1r - 1.0) * lr + a1i * li) / den
    f_im = (a1i * lr - (a1r - 1.0) * li) / den
    bt = bt_ref[...]
    btsw = btsw_ref[...]
    bb = f_re * bt + (f_im * sgn) * btsw
    bbsw = f_re * btsw - (f_im * sgn) * bt
    cw1 = cw1_ref[...]
    cw2 = cw2_ref[...]
    for s in range(CHUNK):
        k = CHUNK - 1 - s
        ws_sc[s * S5_GROUP:(s + 1) * S5_GROUP, :] = ar[k:k + 1] * bb + ais[k:k + 1] * bbsw
        wot_ref[s * S5_GROUP:(s + 1) * S5_GROUP, :] = (ar[s + 1:s + 2] * cw1 + ai[s + 1:s + 2] * cw2).astype(BF16)
    ws = ws_sc[...]
    wst_ref[...] = ws.T.astype(BF16)
    strip = lax.dot_general(cw1, ws, NT_DIMS, preferred_element_type=F32, precision=HI)
    row = lax.broadcasted_iota(jnp.int32, (S5_GROUP, S5_FLAT), 0)
    col = lax.broadcasted_iota(jnp.int32, (S5_GROUP, S5_FLAT), 1)
    dsk = jnp.concatenate([dsk_ref[...]] * (S5_FLAT // LANES), axis=1)
    strip = strip + jnp.where(col == row + (S5_FLAT - S5_GROUP), dsk, 0.0)
    strip = jnp.concatenate([strip, jnp.zeros((S5_GROUP, S5_BLK), F32)], axis=1)
    for d in range(S5_NSUB):
        for tt in range(S5_SUB):
            i0 = CHUNK - 1 - S5_SUB * d - tt
            dt_ref[d, tt * S5_GROUP:(tt + 1) * S5_GROUP, :] = (
                strip[:, S5_GROUP * i0:S5_GROUP * i0 + S5_BLK].astype(BF16))


def _s5_prep_call(lr2, li2, ldt2, bt, btsw, cw1, cw2, dsk):
    n = lr2.shape[0]
    pw = 2 * S5_STATE
    ng = S5_PREP_GROUPS
    vec = pl.BlockSpec((ng, 1, pw), lambda i: (i, 0, 0))
    mat = pl.BlockSpec((ng, S5_GROUP, pw), lambda i: (i, 0, 0))
    return pl.pallas_call(
        _s5_prep_kernel,
        grid=(n // ng,),
        in_specs=[vec, vec, vec, mat, mat, mat, mat, mat],
        out_specs=[
            pl.BlockSpec((ng, S5_NSUB, S5_BLK, S5_BLK), lambda i: (i, 0, 0, 0)),
            pl.BlockSpec((ng, pw, S5_FLAT), lambda i: (i, 0, 0)),
            pl.BlockSpec((ng, S5_FLAT, pw), lambda i: (i, 0, 0)),
        ],
        out_shape=[
            jax.ShapeDtypeStruct((n, S5_NSUB, S5_BLK, S5_BLK), BF16),
            jax.ShapeDtypeStruct((n, pw, S5_FLAT), BF16),
            jax.ShapeDtypeStruct((n, S5_FLAT, pw), BF16),
        ],
        scratch_shapes=[pltpu.VMEM((S5_FLAT, pw), F32)],
        compiler_params=_params(("arbitrary",)),
        name="s5_prep",
    )(lr2, li2, ldt2, bt, btsw, cw1, cw2, dsk)


def _s5_kernel(u_ref, dt_ref, wst_ref, wot_ref, pa_ref, pb_ref, y_ref, ut_sc, yt_sc, *, nch):
    gg = pl.program_id(1)
    ncol = ut_sc.shape[2]
    pw = 2 * S5_STATE

    @pl.when(gg == 0)
    def _regroup():
        for s in range(CHUNK):
            ut_sc[s] = u_ref[pl.ds(s, ncol, stride=CHUNK), :].T.astype(BF16)

    row0 = pl.multiple_of(gg * S5_GROUP, S5_GROUP)
    u = ut_sc[:, pl.ds(row0, S5_GROUP), :].reshape(S5_FLAT, ncol)
    h = jnp.dot(wst_ref[0], u, preferred_element_type=F32)
    n_idx = lax.broadcasted_iota(jnp.int32, (pw, ncol), 1) % nch
    pa = pa_ref[0]
    pb = pb_ref[0]
    step = 1
    j = 0
    while step < nch:
        sh = jnp.where(n_idx >= step, pltpu.roll(h, step, 1), 0.0)
        h = h + pa[:, j:j + 1] * sh + pb[:, j:j + 1] * pltpu.roll(sh, S5_STATE, 0)
        step *= 2
        j += 1
    hprev = jnp.where(n_idx >= 1, pltpu.roll(h, 1, 1), 0.0).astype(BF16)
    for jj in range(S5_NSUB):
        acc = jnp.dot(wot_ref[0, jj * S5_BLK:(jj + 1) * S5_BLK, :], hprev, preferred_element_type=F32)
        for ii in range(jj + 1):
            acc = acc + jnp.dot(dt_ref[0, jj - ii], u[ii * S5_BLK:(ii + 1) * S5_BLK, :],
                                preferred_element_type=F32)
        yt_sc[jj * S5_SUB:(jj + 1) * S5_SUB, pl.ds(row0, S5_GROUP), :] = acc.reshape(S5_SUB, S5_GROUP, ncol)

    @pl.when(gg == S5_GPB - 1)
    def _emit():
        for tt in range(CHUNK):
            y_ref[pl.ds(tt, ncol, stride=CHUNK), :] = yt_sc[tt].T


def _s5_call(u5, dtb, wst, wot, pa, pb, layer, nch):
    t, width = u5.shape
    ncol = t // CHUNK
    nblk = width // LANES
    base = layer * S5_GROUPS
    grp = lambda shape: pl.BlockSpec((1,) + shape, lambda j, g: (base + j * S5_GPB + g,) + (0,) * len(shape))
    return pl.pallas_call(
        functools.partial(_s5_kernel, nch=nch),
        grid=(nblk, S5_GPB),
        in_specs=[
            pl.BlockSpec((t, LANES), lambda j, g: (0, j)),
            grp(dtb.shape[1:]), grp(wst.shape[1:]), grp(wot.shape[1:]), grp(pa.shape[1:]), grp(pb.shape[1:]),
        ],
        out_specs=pl.BlockSpec((t, LANES), lambda j, g: (0, j)),
        out_shape=jax.ShapeDtypeStruct((t, width), F32),
        scratch_shapes=[pltpu.VMEM((CHUNK, LANES, ncol), BF16), pltpu.VMEM((CHUNK, LANES, ncol), F32)],
        compiler_params=_params(("arbitrary", "arbitrary")),
        name="s5",
    )(u5, dtb, wst, wot, pa, pb)


def _merge_kernel(h_ref, yssm_ref, ymla_ref, ysgu_ref, x_ref, ada_ref,
                  wg_ref, bg_ref, wglu_ref, bglu_ref, wb_ref, wout_ref, lng_ref, lnb_ref,
                  o_ref, *, alpha):
    tm, d = x_ref.shape
    sub = MERGE_SUB
    g1 = ada_ref[0, 0][2:3]

    def merge(r):
        rows = slice(r * sub, (r + 1) * sub)
        h = h_ref[rows, :]
        z = _gelu(yssm_ref[rows, :])
        y5 = (z * _sigmoid(jnp.dot(z.astype(BF16), wglu_ref[0], preferred_element_type=F32)
                           + bglu_ref[0])).astype(BF16)
        branches = (y5, ymla_ref[rows, :], ysgu_ref[rows, :])
        merged = None
        for i in range(N_BRANCH):
            cols = slice(i * d, (i + 1) * d)
            gate = _sigmoid(jnp.dot(h, wg_ref[0, :, cols], preferred_element_type=F32) + bg_ref[0, :, cols])
            term = gate * jnp.dot(branches[i], wb_ref[0, i], preferred_element_type=F32)
            merged = term if merged is None else merged + term
        return merged.astype(BF16)

    def project(r, merged):
        rows = slice(r * sub, (r + 1) * sub)
        y = jnp.dot(merged, wout_ref[0], preferred_element_type=F32)
        o_ref[rows, :] = _layer_norm(alpha * x_ref[rows, :] + (1.0 + g1) * y, lng_ref[0], lnb_ref[0])

    nsub = tm // sub
    merged = merge(0)
    for r in range(nsub):
        nxt = merge(r + 1) if r + 1 < nsub else None
        project(r, merged)
        merged = nxt


def _merge_call(h, yssm, ymla, ysgu, x2, ada_all, layer, wg, bg, wglu, bglu, wb, wout, lng, lnb, seq, tm, alpha):
    t, d = x2.shape
    tpb = seq // tm
    tok = lambda n: pl.BlockSpec((tm, n), lambda i: (i, 0))
    consts = [wg, bg, wglu, bglu, wb, wout, lng, lnb]
    return pl.pallas_call(
        functools.partial(_merge_kernel, alpha=alpha),
        grid=(t // tm,),
        in_specs=[tok(d), tok(S5_WIDTH), tok(BRANCH_WIDTH), tok(BRANCH_WIDTH), tok(d),
                  pl.BlockSpec((1, 1, 6, d), lambda i: (layer, i // tpb, 0, 0))]
                 + [_layer_spec(a, layer) for a in consts],
        out_specs=tok(d),
        out_shape=jax.ShapeDtypeStruct((t, d), F32),
        compiler_params=_params(("arbitrary",)),
        name="merge",
    )(h, yssm, ymla, ysgu, x2, ada_all, *consts)


def _ffn_kernel(x_ref, ada_ref, wa_ref, wb_ref, wo_ref, lng_ref, lnb_ref, o_ref, *, alpha):
    tm = x_ref.shape[0]
    sub = FFN_SUB
    ada = ada_ref[0, 0]

    def hidden(r):
        rows = slice(r * sub, (r + 1) * sub)
        h = (x_ref[rows, :] * (1.0 + ada[4:5]) + ada[3:4]).astype(BF16)
        a = jnp.dot(h, wa_ref[0], preferred_element_type=F32)
        b = jnp.dot(h, wb_ref[0], preferred_element_type=F32)
        return (a * _sigmoid(a) * b).astype(BF16)

    def project(r, act):
        rows = slice(r * sub, (r + 1) * sub)
        f = jnp.dot(act, wo_ref[0], preferred_element_type=F32)
        o_ref[rows, :] = _layer_norm(alpha * x_ref[rows, :] + (1.0 + ada[5:6]) * f, lng_ref[0], lnb_ref[0])

    nsub = tm // sub
    act = hidden(0)
    for r in range(nsub):
        nxt = hidden(r + 1) if r + 1 < nsub else None
        project(r, act)
        act = nxt


def _ffn_call(x2, ada_all, layer, w_in, w_out, lng, lnb, seq, tm, alpha):
    t, d = x2.shape
    ff = w_out.shape[1]
    tpb = seq // tm
    tok = pl.BlockSpec((tm, d), lambda i: (i, 0))
    return pl.pallas_call(
        functools.partial(_ffn_kernel, alpha=alpha),
        grid=(t // tm,),
        in_specs=[tok, pl.BlockSpec((1, 1, 6, d), lambda i: (layer, i // tpb, 0, 0)),
                  _layer_spec(w_in, layer, (ff, 0)), _layer_spec(w_in, layer, (ff, 1)), _layer_spec(w_out, layer),
                  _layer_spec(lng, layer), _layer_spec(lnb, layer)],
        out_specs=tok,
        out_shape=jax.ShapeDtypeStruct((t, d), F32),
        compiler_params=_params(("arbitrary",)),
        name="ffn",
    )(x2, ada_all, w_in, w_in, w_out, lng, lnb)


def _s5_prep_inputs(lam_re, lam_im, log_dt, b_re, b_im, c_re, c_im, dskip):
    depth, g, p = lam_re.shape
    n = depth * g
    dup = lambda a: jnp.concatenate([a, a], axis=-1).reshape(n, 1, 2 * p)
    lr2 = dup(lam_re)
    li2 = dup(lam_im)
    ldt2 = jnp.broadcast_to(log_dt[..., None], (depth, g, 2 * p)).reshape(n, 1, 2 * p)
    btr = b_re.transpose(0, 1, 3, 2).reshape(n, S5_GROUP, p)
    bti = b_im.transpose(0, 1, 3, 2).reshape(n, S5_GROUP, p)
    bt = jnp.concatenate([btr, bti], axis=-1)
    btsw = jnp.concatenate([bti, btr], axis=-1)
    cr = c_re.reshape(n, S5_GROUP, p)
    ci = c_im.reshape(n, S5_GROUP, p)
    cw1 = jnp.concatenate([cr, -ci], axis=-1)
    cw2 = jnp.concatenate([-ci, -cr], axis=-1)
    dsk = jnp.broadcast_to(dskip.reshape(n, S5_GROUP, 1), (n, S5_GROUP, LANES))
    return lr2, li2, ldt2, bt, btsw, cw1, cw2, dsk


def _s5_scan_multipliers(lam_re, lam_im, log_dt, nch):
    depth, g, p = lam_re.shape
    dt = jnp.exp(log_dt)[..., None, None]
    nsteps = max(1, (nch - 1).bit_length())
    k = jnp.asarray(CHUNK * (2.0 ** np.arange(nsteps)), F32)
    mag = jnp.exp(lam_re[..., None] * dt * k)
    ang = lam_im[..., None] * dt * k
    sr = (mag * jnp.cos(ang)).reshape(depth * g, p, nsteps)
    si = (mag * jnp.sin(ang)).reshape(depth * g, p, nsteps)
    pad = ((0, 0), (0, 0), (0, S5_SCAN_COLS - nsteps))
    pa = jnp.pad(jnp.concatenate([sr, sr], axis=1), pad)
    pb = jnp.pad(jnp.concatenate([-si, si], axis=1), pad)
    return pa, pb


def _rope_tables(seq, scale):
    inv_freq = 1.0 / (ROPE_THETA ** (jnp.arange(0, QK_ROPE, 2, dtype=F32) / QK_ROPE))
    ang = jnp.arange(seq, dtype=F32)[:, None] * inv_freq[None, :]
    cos = jnp.cos(ang)
    sin = jnp.sin(ang)
    zk = jnp.zeros((seq, LANES - QK_ROPE), F32)
    cosk = jnp.concatenate([cos, cos, zk], axis=1)
    sink = jnp.concatenate([-sin, sin, zk], axis=1)
    zq = jnp.zeros((seq, HEAD_PAD - QK_NOPE - QK_ROPE), F32)
    aq = jnp.concatenate([jnp.full((seq, QK_NOPE), scale, F32), scale * cos, scale * cos, zq], axis=1)
    sq = jnp.concatenate([jnp.zeros((seq, QK_NOPE), F32), -scale * sin, scale * sin, zq], axis=1)
    return cosk, sink, aq.T, sq.T


def _swap_halves(w):
    half = w.shape[-1] // 2
    return jnp.concatenate([w[..., half:], w[..., :half]], axis=-1)


def kernel(x, c, w_ada, b_ada, w_in, b_in, s5_lambda_re, s5_lambda_im, s5_log_dt, s5_b_re, s5_b_im, s5_c_re, s5_c_im, s5_d, s5_w_glu, s5_b_glu, mla_q_norm, mla_w_q_up, mla_kv_norm, mla_w_kv_up, sgu_ln_g, sgu_ln_b, sgu_w_s, sgu_b_s, w_branch, w_out, ln1_g, ln1_b, ffn_w_in, ffn_w_out, ln2_g, ln2_b):
    batch, seq, d = x.shape
    depth = w_ada.shape[0]
    t = batch * seq
    alpha = float((2 * depth) ** 0.25)
    tm = 512
    tq = 512
    nch = seq // CHUNK
    assert seq % FLASH_TK == 0 and FLASH_TK % tm == 0 and tm == tq and tm % SGU_CHUNK == 0
    assert (batch * nch) % LANES == 0

    c_pad = jnp.pad(c, ((0, -batch % SUBLANES), (0, 0)))
    ada_all = _ada_call(c_pad, w_ada, b_ada)[:, :batch].reshape(depth, batch, 6, d)

    scale = LOG2E * (QK_NOPE + QK_ROPE) ** -0.5
    cosk, sink, aqt, sqt = _rope_tables(seq, scale)

    offs = np.cumsum([0, S5_WIDTH, Q_LORA, KV_LORA, QK_ROPE, SGU_WIDTH, SGU_WIDTH])
    o_u5, o_cq, o_ckv, o_kpe, o_usgu, o_vsgu, o_gate = [int(o) for o in offs]
    row = lambda a: a[:, None, :]

    kpe_b = b_in[:, o_kpe:o_kpe + QK_ROPE]
    w_a, w_g = _win_prep_call(w_in, o_kpe, o_usgu, o_gate)
    b_a = row(jnp.concatenate([b_in[:, o_u5:o_kpe], b_in[:, o_usgu:o_gate], kpe_b, _swap_halves(kpe_b),
                               jnp.zeros((depth, LANES - 2 * QK_ROPE), F32)], axis=1))
    b_g = row(b_in[:, o_gate:])
    pos = np.arange(SGU_CHUNK) // CHUNK
    sgu_mask = jnp.asarray(pos[None, :] <= pos[:, None])
    ws_m = jnp.where(sgu_mask[None, None], sgu_w_s, 0.0).astype(BF16)
    bs_full = jnp.repeat(sgu_b_s.transpose(0, 2, 1), SGU_WIDTH // SGU_GROUPS, axis=2)

    hw = MLA_HEADS * HEAD_PAD
    vw = MLA_HEADS * V_PAD
    wq4 = mla_w_q_up.reshape(depth, Q_LORA, MLA_HEADS, QK_NOPE + QK_ROPE)
    wqt = (jnp.concatenate([wq4, _swap_halves(wq4[..., QK_NOPE:])], axis=3)
           .reshape(depth, Q_LORA, hw).transpose(0, 2, 1).astype(BF16))
    wkv4 = mla_w_kv_up.reshape(depth, KV_LORA, MLA_HEADS, QK_NOPE + V_HEAD)
    wk = jnp.concatenate([wkv4[..., :QK_NOPE], jnp.zeros((depth, KV_LORA, MLA_HEADS, HEAD_PAD - QK_NOPE), F32)],
                         axis=3).reshape(depth, KV_LORA, hw).astype(BF16)
    wvt = (jnp.concatenate([wkv4[..., QK_NOPE:], jnp.zeros((depth, KV_LORA, MLA_HEADS, V_PAD - V_HEAD), F32)],
                           axis=3).reshape(depth, KV_LORA, vw).transpose(0, 2, 1).astype(BF16))
    place = np.zeros((LANES, hw), np.float32)
    for hh in range(MLA_HEADS):
        place[np.arange(QK_ROPE), hh * HEAD_PAD + QK_NOPE + np.arange(QK_ROPE)] = 1.0
    pk = jnp.asarray(place, BF16)
    ones_col = np.zeros((vw, 1), np.float32)
    ones_col[np.arange(MLA_HEADS) * V_PAD + V_HEAD, 0] = 1.0
    ones_col = jnp.asarray(ones_col)

    wglu = s5_w_glu.astype(BF16)
    wbr = w_branch.astype(BF16)
    wout = w_out.astype(BF16)
    ffn_in = ffn_w_in.astype(BF16)
    ffn_out = ffn_w_out.astype(BF16)

    dtb, wst, wot = _s5_prep_call(*_s5_prep_inputs(s5_lambda_re, s5_lambda_im, s5_log_dt, s5_b_re, s5_b_im,
                                                   s5_c_re, s5_c_im, s5_d))
    pa, pb = _s5_scan_multipliers(s5_lambda_re, s5_lambda_im, s5_log_dt, nch)

    x2 = x.reshape(t, d)
    for l in range(depth):
        h, u5, ysgu, kmat, qt, vt = _inproj_call(
            x2, ada_all, l, w_a, b_a, row(mla_q_norm), row(mla_kv_norm), row(sgu_ln_g), row(sgu_ln_b),
            ws_m, bs_full, wqt, wk, wvt, pk, ones_col, cosk, sink, aqt, sqt, seq, tm)
        ymla = _flash_call(kmat, qt, vt, batch, seq, tq, FLASH_TK)
        yssm = _s5_call(u5, dtb, wst, wot, pa, pb, l, nch)
        x2 = _merge_call(h, yssm, ymla, ysgu, x2, ada_all, l, w_g, b_g, wglu, row(s5_b_glu), wbr, wout,
                         row(ln1_g), row(ln1_b), seq, tm, alpha)
        x2 = _ffn_call(x2, ada_all, l, ffn_in, ffn_out, row(ln2_g), row(ln2_b), seq, tm, alpha)
    return x2.reshape(batch, seq, d)
```

```python
import functools
import math

import numpy as np
import jax
import jax.numpy as jnp
from jax import lax
from jax.experimental import pallas as pl
from jax.experimental.pallas import tpu as pltpu

F32 = jnp.float32
BF16 = jnp.bfloat16

CHUNK = 64
S5_WIDTH = 512
S5_GROUP = 16
S5_GROUPS = S5_WIDTH // S5_GROUP
S5_STATE = 64
MLA_HEADS = 8
QK_NOPE = 64
QK_ROPE = 32
V_HEAD = 64
Q_LORA = 384
KV_LORA = 256
ROPE_THETA = 10000.0
SGU_WIDTH = 512
SGU_GROUPS = 4
SGU_CHUNK = 128
N_BRANCH = 3
BRANCH_WIDTH = 512
LN_EPS = 1e-5
RMS_EPS = 1e-6
NEG_INF = -1e30

LANES = 128
SUBLANES = 8
BF16_ROWS = 16
HEAD_PAD = 128
V_PAD = V_HEAD + BF16_ROWS
S5_SUB = 16
S5_NSUB = CHUNK // S5_SUB
S5_BLK = S5_SUB * S5_GROUP
S5_FLAT = CHUNK * S5_GROUP
S5_GPB = LANES // S5_GROUP
S5_SCAN_COLS = 8
S5_REGROUP_SPLIT = 2
INPROJ_SUB = 256
MERGE_SUB = 256
FFN_SUB = 256
S5_PREP_GROUPS = 8
FLASH_SUB = 256
FLASH_TK = 1024
FLASH_DEPTH = 6

OFF_U5 = 0
OFF_CQ = OFF_U5 + S5_WIDTH
OFF_CKV = OFF_CQ + Q_LORA
OFF_USGU = OFF_CKV + KV_LORA
OFF_VSGU = OFF_USGU + SGU_WIDTH
OFF_KPE = OFF_VSGU + SGU_WIDTH
NA = OFF_KPE + LANES

VMEM_LIMIT = 56 * 1024 * 1024

HI = lax.Precision.HIGHEST
NT_DIMS = (((1,), (1,)), ((), ()))
LOG2E = 1.4426950408889634


def _gelu(x):
    return 0.5 * x * (1.0 + jnp.tanh(0.7978845608028654 * (x + 0.044715 * (x * x * x))))


def _sigmoid(x):
    return 0.5 * jnp.tanh(0.5 * x) + 0.5


def _layer_norm(r, g, b):
    mu = jnp.mean(r, axis=-1, keepdims=True)
    rc = r - mu
    var = jnp.mean(rc * rc, axis=-1, keepdims=True)
    return rc * lax.rsqrt(var + LN_EPS) * g + b


def _const_spec(shape):
    nd = len(shape)
    return pl.BlockSpec(shape, lambda *_: (0,) * nd, pipeline_mode=pl.Buffered(1))


def _layer_spec(arr, layer, col_block=None):
    shape = (1,) + arr.shape[1:]
    idx = (layer,) + (0,) * (arr.ndim - 1)
    if col_block is not None:
        width, j = col_block
        shape = shape[:-1] + (width,)
        idx = idx[:-1] + (j,)
    return pl.BlockSpec(shape, lambda *_: idx, pipeline_mode=pl.Buffered(1))


def _params(sem):
    return pltpu.CompilerParams(dimension_semantics=sem, vmem_limit_bytes=VMEM_LIMIT)


def _ada_kernel(cb_ref, w_ref, b_ref, o_ref, ca_sc):
    nb, d, _ = cb_ref.shape
    tn = w_ref.shape[2]

    @pl.when((pl.program_id(0) == 0) & (pl.program_id(1) == 0))
    def _silu():
        cb = cb_ref[...]
        ca_sc[...] = cb * _sigmoid(cb)

    def body(g, accs):
        r0 = pl.multiple_of(g * SUBLANES, SUBLANES)
        w = w_ref[0, pl.ds(r0, SUBLANES), :]
        out = []
        for b in range(nb):
            scale = jnp.concatenate([ca_sc[b, pl.ds(r0, SUBLANES), :]] * (tn // LANES), axis=1)
            out.append(accs[b] + w * scale)
        return tuple(out)

    accs = lax.fori_loop(0, d // SUBLANES, body, tuple(jnp.zeros((SUBLANES, tn), F32) for _ in range(nb)))
    for b in range(nb):
        o_ref[0, b:b + 1, :] = jnp.sum(accs[b], axis=0, keepdims=True) + b_ref[0]


def _ada_call(c, w_ada, b_ada):
    depth, d, n6 = w_ada.shape
    batch = c.shape[0]
    tn = n6 // 8
    cb = jnp.broadcast_to(c[:, :, None], (batch, d, LANES))
    return pl.pallas_call(
        _ada_kernel,
        grid=(depth, n6 // tn),
        in_specs=[
            _const_spec(cb.shape),
            pl.BlockSpec((1, d, tn), lambda l, j: (l, 0, j)),
            pl.BlockSpec((1, 1, tn), lambda l, j: (l, 0, j)),
        ],
        out_specs=pl.BlockSpec((1, batch, tn), lambda l, j: (l, 0, j)),
        out_shape=jax.ShapeDtypeStruct((depth, batch, n6), F32),
        scratch_shapes=[pltpu.VMEM(cb.shape, F32)],
        compiler_params=_params(("arbitrary", "arbitrary")),
        name="ada",
    )(cb, w_ada, b_ada.reshape(depth, 1, n6))


def _win_prep_kernel(w_ref, wa_ref, wg_ref, *, o_kpe, o_usgu, o_gate):
    w = w_ref[0]
    half = QK_ROPE // 2
    kpe = w[:, o_kpe:o_kpe + QK_ROPE]
    last = jnp.concatenate([kpe, kpe[:, half:], kpe[:, :half],
                            jnp.zeros((w.shape[0], LANES - 2 * QK_ROPE), F32)], axis=1)
    wa_ref[0, :, 0:o_kpe] = w[:, 0:o_kpe].astype(BF16)
    wa_ref[0, :, o_kpe:OFF_KPE] = w[:, o_usgu:o_gate].astype(BF16)
    wa_ref[0, :, OFF_KPE:NA] = last.astype(BF16)
    wg_ref[0] = w[:, o_gate:].astype(BF16)


def _win_prep_call(w_in, o_kpe, o_usgu, o_gate):
    depth, d, n_in = w_in.shape
    tr = 256
    assert o_kpe == OFF_USGU and o_kpe + (o_gate - o_usgu) == OFF_KPE
    return pl.pallas_call(
        functools.partial(_win_prep_kernel, o_kpe=o_kpe, o_usgu=o_usgu, o_gate=o_gate),
        grid=(depth, d // tr),
        in_specs=[pl.BlockSpec((1, tr, n_in), lambda l, i: (l, i, 0))],
        out_specs=[pl.BlockSpec((1, tr, NA), lambda l, i: (l, i, 0)),
                   pl.BlockSpec((1, tr, n_in - o_gate), lambda l, i: (l, i, 0))],
        out_shape=[jax.ShapeDtypeStruct((depth, d, NA), BF16),
                   jax.ShapeDtypeStruct((depth, d, n_in - o_gate), BF16)],
        compiler_params=_params(("arbitrary", "arbitrary")),
        name="win_prep",
    )(w_in)


def _inproj_kernel(x_ref, ada_ref, w_ref, b_ref, qn_ref, kvn_ref, lng_ref, lnb_ref, ws_ref, bs_ref,
                   wqt_ref, wk_ref, wvt_ref, pk_ref, ones_ref, cosk_ref, sink_ref, aqt_ref, sqt_ref,
                   h_ref, u5_ref, ysgu_ref, k_ref, qt_ref, vt_ref):
    tm = x_ref.shape[0]
    ada = ada_ref[0, 0]
    sub = INPROJ_SUB
    gw = SGU_WIDTH // SGU_GROUPS

    def project(r):
        rows = slice(r * sub, (r + 1) * sub)
        h = (x_ref[rows, :] * (1.0 + ada[1:2]) + ada[0:1]).astype(BF16)
        h_ref[rows, :] = h
        return jnp.dot(h, w_ref[0], preferred_element_type=F32) + b_ref[0]

    def epilogue(r, acc):
        rows = slice(r * sub, (r + 1) * sub)
        u5_ref[rows, :] = acc[:, OFF_U5:OFF_U5 + S5_WIDTH]

        cq = acc[:, OFF_CQ:OFF_CQ + Q_LORA]
        cqn = (cq * lax.rsqrt(jnp.mean(cq * cq, axis=-1, keepdims=True) + RMS_EPS) * qn_ref[0]).astype(BF16)
        ckv = acc[:, OFF_CKV:OFF_CKV + KV_LORA]
        ckvn = (ckv * lax.rsqrt(jnp.mean(ckv * ckv, axis=-1, keepdims=True) + RMS_EPS) * kvn_ref[0]).astype(BF16)

        kb = acc[:, OFF_KPE:OFF_KPE + LANES]
        krot = (kb * cosk_ref[rows, :] + pltpu.roll(kb, LANES - QK_ROPE, 1) * sink_ref[rows, :]).astype(BF16)

        k_ref[rows, :] = (jnp.dot(ckvn, wk_ref[0], preferred_element_type=F32)
                          + jnp.dot(krot, pk_ref[...], preferred_element_type=F32)).astype(BF16)
        qt = lax.dot_general(wqt_ref[0], cqn, NT_DIMS, preferred_element_type=F32)
        aqt = aqt_ref[:, rows]
        sqt = sqt_ref[:, rows]
        for h in range(MLA_HEADS):
            hrow = slice(h * HEAD_PAD, (h + 1) * HEAD_PAD)
            qh = qt[hrow, :]
            qt_ref[0, hrow, rows] = (qh * aqt + pltpu.roll(qh, HEAD_PAD - QK_ROPE, 0) * sqt).astype(BF16)
        vt = lax.dot_general(wvt_ref[0], ckvn, NT_DIMS, preferred_element_type=F32) + ones_ref[...]
        vt_ref[0, :, rows] = vt.astype(BF16)

        us = _gelu(acc[:, OFF_USGU:OFF_USGU + SGU_WIDTH])
        vn = _layer_norm(_gelu(acc[:, OFF_VSGU:OFF_VSGU + SGU_WIDTH]), lng_ref[0], lnb_ref[0]).astype(BF16)
        for c in range(sub // SGU_CHUNK):
            crow = slice(c * SGU_CHUNK, (c + 1) * SGU_CHUNK)
            parts = [jnp.dot(ws_ref[0, g], vn[crow, g * gw:(g + 1) * gw], preferred_element_type=F32)
                     for g in range(SGU_GROUPS)]
            mixed = jnp.concatenate(parts, axis=1) + bs_ref[0]
            ysgu_ref[r * sub + c * SGU_CHUNK:r * sub + (c + 1) * SGU_CHUNK, :] = (us[crow, :] * mixed).astype(BF16)

    nsub = tm // sub
    acc = project(0)
    for r in range(nsub):
        nxt = project(r + 1) if r + 1 < nsub else None
        epilogue(r, acc)
        acc = nxt


def _inproj_call(x2, ada_all, layer, w, b, qn, kvn, lng, lnb, ws, bs, wqt, wk, wvt, pk, ones,
                 cosk, sink, aqt, sqt, seq, tm):
    t, d = x2.shape
    tpb = seq // tm
    hw = MLA_HEADS * HEAD_PAD
    vw = MLA_HEADS * V_PAD
    tok = lambda n: pl.BlockSpec((tm, n), lambda i: (i, 0))
    ktab = pl.BlockSpec((tm, LANES), lambda i: (i % tpb, 0))
    qtab = pl.BlockSpec((HEAD_PAD, tm), lambda i: (0, i % tpb))
    consts = [w, b, qn, kvn, lng, lnb, ws, bs, wqt, wk, wvt]
    return pl.pallas_call(
        _inproj_kernel,
        grid=(t // tm,),
        in_specs=[tok(d), pl.BlockSpec((1, 1, 6, d), lambda i: (layer, i // tpb, 0, 0))]
                 + [_layer_spec(a, layer) for a in consts]
                 + [_const_spec(pk.shape), _const_spec(ones.shape), ktab, ktab, qtab, qtab],
        out_specs=[tok(d), tok(S5_WIDTH), tok(SGU_WIDTH), tok(hw),
                   pl.BlockSpec((1, hw, tm), lambda i: (i, 0, 0)),
                   pl.BlockSpec((1, vw, tm), lambda i: (i, 0, 0))],
        out_shape=[
            jax.ShapeDtypeStruct((t, d), BF16),
            jax.ShapeDtypeStruct((t, S5_WIDTH), F32),
            jax.ShapeDtypeStruct((t, SGU_WIDTH), BF16),
            jax.ShapeDtypeStruct((t, hw), BF16),
            jax.ShapeDtypeStruct((t // tm, hw, tm), BF16),
            jax.ShapeDtypeStruct((t // tm, vw, tm), BF16),
        ],
        compiler_params=_params(("arbitrary",)),
        name="inproj",
    )(x2, ada_all, *consts, pk, ones, cosk, sink, aqt, sqt)


def _flash_kernel(qi_ref, ki_ref, kind_ref, k_ref, qt_ref, vt_ref, o_ref, m_sc, acc_sc):
    p_idx = pl.program_id(1)
    ki = ki_ref[p_idx]
    kind = kind_ref[p_idx]
    tk = k_ref.shape[0]
    tq = qt_ref.shape[2]
    kslabs = tk // tq

    @pl.when(ki == 0)
    def _init():
        m_sc[...] = jnp.full(m_sc.shape, NEG_INF, F32)
        acc_sc[...] = jnp.zeros(acc_sc.shape, F32)

    def step(q_off):
        sub = FLASH_SUB
        nkb, nqb = tk // sub, tq // sub
        state = [(h, qb) for h in range(MLA_HEADS) for qb in range(nqb)]
        m_all = {(h, qb): m_sc[h, :, qb * sub:(qb + 1) * sub] for h, qb in state}
        acc_all = {(h, qb): acc_sc[h, :, qb * sub:(qb + 1) * sub] for h, qb in state}
        blocks = [(h, qb, kb) for kb in range(nkb) for h, qb in state]
        if q_off is not None:
            blocks = [(h, qb, kb) for h, qb, kb in blocks
                      if kb * sub // CHUNK <= (q_off + (qb + 1) * sub - 1) // CHUNK]

        def scores(h, qb, kb):
            cols = slice(h * HEAD_PAD, (h + 1) * HEAD_PAD)
            st = jnp.dot(k_ref[kb * sub:(kb + 1) * sub, cols], qt_ref[0, cols, qb * sub:(qb + 1) * sub],
                         preferred_element_type=F32)
            if q_off is not None and ((kb + 1) * sub - 1) // CHUNK > (q_off + qb * sub) // CHUNK:
                kc = (lax.broadcasted_iota(jnp.int32, (sub, sub), 0) + kb * sub) // CHUNK
                qc = (lax.broadcasted_iota(jnp.int32, (sub, sub), 1) + q_off + qb * sub) // CHUNK
                st = jnp.where(kc <= qc, st, NEG_INF)
            return st

        def absorb(h, qb, kb, st):
            m_prev = m_all[(h, qb)]
            m_new = jnp.maximum(m_prev, jnp.max(st, axis=0, keepdims=True))
            m_all[(h, qb)] = m_new
            pt = jnp.exp2(st - m_new[0:1]).astype(BF16)
            slab, col = divmod(kb * sub, tq)
            acc_all[(h, qb)] = (jnp.exp2(m_prev - m_new)[0:1] * acc_all[(h, qb)]
                                + jnp.dot(vt_ref[slab, h * V_PAD:(h + 1) * V_PAD, col:col + sub], pt,
                                          preferred_element_type=F32))

        nblk = len(blocks)
        pending = {i: scores(*blocks[i]) for i in range(min(FLASH_DEPTH, nblk))}
        for i in range(nblk):
            if i + FLASH_DEPTH < nblk:
                pending[i + FLASH_DEPTH] = scores(*blocks[i + FLASH_DEPTH])
            absorb(*blocks[i], pending.pop(i))
        for h, qb in state:
            m_sc[h, :, qb * sub:(qb + 1) * sub] = m_all[(h, qb)]
            acc_sc[h, :, qb * sub:(qb + 1) * sub] = acc_all[(h, qb)]

    def finish():
        outs = []
        for h in range(MLA_HEADS):
            acc = acc_sc[h]
            outs.append(acc[:V_HEAD] / acc[V_HEAD:V_HEAD + 1])
        o_ref[...] = jnp.concatenate(outs, axis=0).T.astype(BF16)

    @pl.when(kind == 0)
    def _below():
        step(None)

    for slab in range(kslabs):
        @pl.when(kind == slab + 1)
        def _diag(slab=slab):
            step(slab * tq)
            finish()


def _flash_call(k, qt, vt, batch, seq, tq, tk):
    t, hw = k.shape
    vw = vt.shape[1]
    nq = seq // tq
    nk = seq // tk
    kslabs = tk // tq
    steps = [(a, b, 0 if b < a // kslabs else a % kslabs + 1) for a in range(nq) for b in range(a // kslabs + 1)]
    qi_tab, ki_tab, kind_tab = (jnp.asarray([s[i] for s in steps], jnp.int32) for i in range(3))
    grid_spec = pltpu.PrefetchScalarGridSpec(
        num_scalar_prefetch=3,
        grid=(batch, len(steps)),
        in_specs=[
            pl.BlockSpec((tk, hw), lambda b, p, qt_, kt_, kd_: (b * nk + kt_[p], 0)),
            pl.BlockSpec((1, hw, tq), lambda b, p, qt_, kt_, kd_: (b * nq + qt_[p], 0, 0)),
            pl.BlockSpec((kslabs, vw, tq), lambda b, p, qt_, kt_, kd_: (b * nk + kt_[p], 0, 0)),
        ],
        out_specs=pl.BlockSpec((tq, MLA_HEADS * V_HEAD), lambda b, p, qt_, kt_, kd_: (b * nq + qt_[p], 0)),
        scratch_shapes=[
            pltpu.VMEM((MLA_HEADS, SUBLANES, tq), F32),
            pltpu.VMEM((MLA_HEADS, V_PAD, tq), F32),
        ],
    )
    return pl.pallas_call(
        _flash_kernel,
        grid_spec=grid_spec,
        out_shape=jax.ShapeDtypeStruct((t, MLA_HEADS * V_HEAD), BF16),
        compiler_params=_params(("arbitrary", "arbitrary")),
        name="flash",
    )(qi_tab, ki_tab, kind_tab, k, qt, vt)


def _s5_prep_kernel(lr_ref, li_ref, ldt_ref, bt_ref, btsw_ref, cw1_ref, cw2_ref, dsk_ref,
                    dt_ref, wst_ref, wot_ref, ws_sc):
    for q in range(S5_PREP_GROUPS):
        _s5_prep_group(lr_ref.at[q], li_ref.at[q], ldt_ref.at[q], bt_ref.at[q], btsw_ref.at[q], cw1_ref.at[q],
                       cw2_ref.at[q], dsk_ref.at[q], dt_ref.at[q], wst_ref.at[q], wot_ref.at[q], ws_sc)


def _s5_prep_group(lr_ref, li_ref, ldt_ref, bt_ref, btsw_ref, cw1_ref, cw2_ref, dsk_ref,
                   dt_ref, wst_ref, wot_ref, ws_sc):
    pw = 2 * S5_STATE
    lr = lr_ref[...]
    li = li_ref[...]
    dt = jnp.exp(ldt_ref[...])
    lane = lax.broadcasted_iota(jnp.int32, (1, pw), 1)
    sgn = jnp.where(lane < S5_STATE, -1.0, 1.0)
    nk = CHUNK + SUBLANES
    kk = lax.broadcasted_iota(jnp.int32, (nk, pw), 0).astype(F32)
    mag = jnp.exp(kk * (lr * dt))
    ang = kk * (li * dt)
    ar = mag * jnp.cos(ang)
    ai = mag * jnp.sin(ang)
    ais = ai * sgn
    den = lr * lr + li * li
    a1r = ar[1:2]
    a1i = ai[1:2]
    f_re = ((a1r - 1.0) * lr + a1i * li) / den
    f_im = (a1i * lr - (a1r - 1.0) * li) / den
    bt = bt_ref[...]
    btsw = btsw_ref[...]
    bb = f_re * bt + (f_im * sgn) * btsw
    bbsw = f_re * btsw - (f_im * sgn) * bt
    cw1 = cw1_ref[...]
    cw2 = cw2_ref[...]
    for s in range(CHUNK):
        k = CHUNK - 1 - s
        ws_sc[s * S5_GROUP:(s + 1) * S5_GROUP, :] = ar[k:k + 1] * bb + ais[k:k + 1] * bbsw
        wot_ref[s * S5_GROUP:(s + 1) * S5_GROUP, :] = (ar[s + 1:s + 2] * cw1 + ai[s + 1:s + 2] * cw2).astype(BF16)
    ws = ws_sc[...]
    wst_ref[...] = ws.T.astype(BF16)
    strip = lax.dot_general(cw1, ws, NT_DIMS, preferred_element_type=F32, precision=HI)
    row = lax.broadcasted_iota(jnp.int32, (S5_GROUP, S5_FLAT), 0)
    col = lax.broadcasted_iota(jnp.int32, (S5_GROUP, S5_FLAT), 1)
    dsk = jnp.concatenate([dsk_ref[...]] * (S5_FLAT // LANES), axis=1)
    strip = strip + jnp.where(col == row + (S5_FLAT - S5_GROUP), dsk, 0.0)
    strip = jnp.concatenate([strip, jnp.zeros((S5_GROUP, S5_BLK), F32)], axis=1)
    for d in range(S5_NSUB):
        for tt in range(S5_SUB):
            i0 = CHUNK - 1 - S5_SUB * d - tt
            dt_ref[d, tt * S5_GROUP:(tt + 1) * S5_GROUP, :] = (
                strip[:, S5_GROUP * i0:S5_GROUP * i0 + S5_BLK].astype(BF16))


def _s5_prep_call(lr2, li2, ldt2, bt, btsw, cw1, cw2, dsk):
    n = lr2.shape[0]
    pw = 2 * S5_STATE
    ng = S5_PREP_GROUPS
    vec = pl.BlockSpec((ng, 1, pw), lambda i: (i, 0, 0))
    mat = pl.BlockSpec((ng, S5_GROUP, pw), lambda i: (i, 0, 0))
    return pl.pallas_call(
        _s5_prep_kernel,
        grid=(n // ng,),
        in_specs=[vec, vec, vec, mat, mat, mat, mat, mat],
        out_specs=[
            pl.BlockSpec((ng, S5_NSUB, S5_BLK, S5_BLK), lambda i: (i, 0, 0, 0)),
            pl.BlockSpec((ng, pw, S5_FLAT), lambda i: (i, 0, 0)),
            pl.BlockSpec((ng, S5_FLAT, pw), lambda i: (i, 0, 0)),
        ],
        out_shape=[
            jax.ShapeDtypeStruct((n, S5_NSUB, S5_BLK, S5_BLK), BF16),
            jax.ShapeDtypeStruct((n, pw, S5_FLAT), BF16),
            jax.ShapeDtypeStruct((n, S5_FLAT, pw), BF16),
        ],
        scratch_shapes=[pltpu.VMEM((S5_FLAT, pw), F32)],
        compiler_params=_params(("arbitrary",)),
        name="s5_prep",
    )(lr2, li2, ldt2, bt, btsw, cw1, cw2, dsk)


def _s5_kernel(u_ref, dt_ref, wst_ref, wot_ref, pa_ref, pb_ref, y_ref, ut_sc, yt_sc, *, nch):
    gg = pl.program_id(1)
    ncol = ut_sc.shape[2]
    pw = 2 * S5_STATE

    nhalf = S5_REGROUP_SPLIT
    hcol = ncol // nhalf

    @pl.when(gg == 0)
    def _regroup():
        for hb in range(nhalf):
            x = u_ref[hb * hcol * CHUNK:(hb + 1) * hcol * CHUNK, :].reshape(hcol, CHUNK, LANES)
            xs = jnp.swapaxes(x, 0, 1)
            for s in range(CHUNK):
                ut_sc[s, :, hb * hcol:(hb + 1) * hcol] = xs[s].T.astype(BF16)

    row0 = pl.multiple_of(gg * S5_GROUP, S5_GROUP)
    u = ut_sc[:, pl.ds(row0, S5_GROUP), :].reshape(S5_FLAT, ncol)
    h = jnp.dot(wst_ref[0], u, preferred_element_type=F32)
    n_idx = lax.broadcasted_iota(jnp.int32, (pw, ncol), 1) % nch
    pa = pa_ref[0]
    pb = pb_ref[0]
    step = 1
    j = 0
    while step < nch:
        sh = jnp.where(n_idx >= step, pltpu.roll(h, step, 1), 0.0)
        h = h + pa[:, j:j + 1] * sh + pb[:, j:j + 1] * pltpu.roll(sh, S5_STATE, 0)
        step *= 2
        j += 1
    hprev = jnp.where(n_idx >= 1, pltpu.roll(h, 1, 1), 0.0).astype(BF16)
    for jj in range(S5_NSUB):
        acc = jnp.dot(wot_ref[0, jj * S5_BLK:(jj + 1) * S5_BLK, :], hprev, preferred_element_type=F32)
        for ii in range(jj + 1):
            acc = acc + jnp.dot(dt_ref[0, jj - ii], u[ii * S5_BLK:(ii + 1) * S5_BLK, :],
                                preferred_element_type=F32)
        yt_sc[jj * S5_SUB:(jj + 1) * S5_SUB, pl.ds(row0, S5_GROUP), :] = acc.reshape(S5_SUB, S5_GROUP, ncol)

    @pl.when(gg == S5_GPB - 1)
    def _emit():
        for hb in range(nhalf):
            ys = jnp.stack([yt_sc[tt, :, hb * hcol:(hb + 1) * hcol].T for tt in range(CHUNK)], axis=0)
            y_ref[hb * hcol * CHUNK:(hb + 1) * hcol * CHUNK, :] = jnp.swapaxes(ys, 0, 1).reshape(hcol * CHUNK, LANES)


def _s5_call(u5, dtb, wst, wot, pa, pb, layer, nch):
    t, width = u5.shape
    ncol = t // CHUNK
    nblk = width // LANES
    base = layer * S5_GROUPS
    grp = lambda shape: pl.BlockSpec((1,) + shape, lambda j, g: (base + j * S5_GPB + g,) + (0,) * len(shape))
    return pl.pallas_call(
        functools.partial(_s5_kernel, nch=nch),
        grid=(nblk, S5_GPB),
        in_specs=[
            pl.BlockSpec((t, LANES), lambda j, g: (0, j)),
            grp(dtb.shape[1:]), grp(wst.shape[1:]), grp(wot.shape[1:]), grp(pa.shape[1:]), grp(pb.shape[1:]),
        ],
        out_specs=pl.BlockSpec((t, LANES), lambda j, g: (0, j)),
        out_shape=jax.ShapeDtypeStruct((t, width), F32),
        scratch_shapes=[pltpu.VMEM((CHUNK, LANES, ncol), BF16), pltpu.VMEM((CHUNK, LANES, ncol), F32)],
        compiler_params=_params(("arbitrary", "arbitrary")),
        name="s5",
    )(u5, dtb, wst, wot, pa, pb)


def _merge_kernel(h_ref, yssm_ref, ymla_ref, ysgu_ref, x_ref, ada_ref,
                  wg_ref, bg_ref, wglu_ref, bglu_ref, wb_ref, wout_ref, lng_ref, lnb_ref,
                  o_ref, *, alpha):
    tm, d = x_ref.shape
    sub = MERGE_SUB
    g1 = ada_ref[0, 0][2:3]

    def merge(r):
        rows = slice(r * sub, (r + 1) * sub)
        h = h_ref[rows, :]
        z = _gelu(yssm_ref[rows, :])
        y5 = (z * _sigmoid(jnp.dot(z.astype(BF16), wglu_ref[0], preferred_element_type=F32)
                           + bglu_ref[0])).astype(BF16)
        branches = (y5, ymla_ref[rows, :], ysgu_ref[rows, :])
        merged = None
        for i in range(N_BRANCH):
            cols = slice(i * d, (i + 1) * d)
            gate = _sigmoid(jnp.dot(h, wg_ref[0, :, cols], preferred_element_type=F32) + bg_ref[0, :, cols])
            term = gate * jnp.dot(branches[i], wb_ref[0, i], preferred_element_type=F32)
            merged = term if merged is None else merged + term
        return merged.astype(BF16)

    def project(r, merged):
        rows = slice(r * sub, (r + 1) * sub)
        y = jnp.dot(merged, wout_ref[0], preferred_element_type=F32)
        o_ref[rows, :] = _layer_norm(alpha * x_ref[rows, :] + (1.0 + g1) * y, lng_ref[0], lnb_ref[0])

    nsub = tm // sub
    merged = merge(0)
    for r in range(nsub):
        nxt = merge(r + 1) if r + 1 < nsub else None
        project(r, merged)
        merged = nxt


def _merge_call(h, yssm, ymla, ysgu, x2, ada_all, layer, wg, bg, wglu, bglu, wb, wout, lng, lnb, seq, tm, alpha):
    t, d = x2.shape
    tpb = seq // tm
    tok = lambda n: pl.BlockSpec((tm, n), lambda i: (i, 0))
    consts = [wg, bg, wglu, bglu, wb, wout, lng, lnb]
    return pl.pallas_call(
        functools.partial(_merge_kernel, alpha=alpha),
        grid=(t // tm,),
        in_specs=[tok(d), tok(S5_WIDTH), tok(BRANCH_WIDTH), tok(BRANCH_WIDTH), tok(d),
                  pl.BlockSpec((1, 1, 6, d), lambda i: (layer, i // tpb, 0, 0))]
                 + [_layer_spec(a, layer) for a in consts],
        out_specs=tok(d),
        out_shape=jax.ShapeDtypeStruct((t, d), F32),
        compiler_params=_params(("arbitrary",)),
        name="merge",
    )(h, yssm, ymla, ysgu, x2, ada_all, *consts)


def _ffn_kernel(x_ref, ada_ref, wa_ref, wb_ref, wo_ref, lng_ref, lnb_ref, o_ref, *, alpha):
    tm = x_ref.shape[0]
    sub = FFN_SUB
    ada = ada_ref[0, 0]

    def hidden(r):
        rows = slice(r * sub, (r + 1) * sub)
        h = (x_ref[rows, :] * (1.0 + ada[4:5]) + ada[3:4]).astype(BF16)
        a = jnp.dot(h, wa_ref[0], preferred_element_type=F32)
        b = jnp.dot(h, wb_ref[0], preferred_element_type=F32)
        return (a * _sigmoid(a) * b).astype(BF16)

    def project(r, act):
        rows = slice(r * sub, (r + 1) * sub)
        f = jnp.dot(act, wo_ref[0], preferred_element_type=F32)
        o_ref[rows, :] = _layer_norm(alpha * x_ref[rows, :] + (1.0 + ada[5:6]) * f, lng_ref[0], lnb_ref[0])

    nsub = tm // sub
    act = hidden(0)
    for r in range(nsub):
        nxt = hidden(r + 1) if r + 1 < nsub else None
        project(r, act)
        act = nxt


def _ffn_call(x2, ada_all, layer, w_in, w_out, lng, lnb, seq, tm, alpha):
    t, d = x2.shape
    ff = w_out.shape[1]
    tpb = seq // tm
    tok = pl.BlockSpec((tm, d), lambda i: (i, 0))
    return pl.pallas_call(
        functools.partial(_ffn_kernel, alpha=alpha),
        grid=(t // tm,),
        in_specs=[tok, pl.BlockSpec((1, 1, 6, d), lambda i: (layer, i // tpb, 0, 0)),
                  _layer_spec(w_in, layer, (ff, 0)), _layer_spec(w_in, layer, (ff, 1)), _layer_spec(w_out, layer),
                  _layer_spec(lng, layer), _layer_spec(lnb, layer)],
        out_specs=tok,
        out_shape=jax.ShapeDtypeStruct((t, d), F32),
        compiler_params=_params(("arbitrary",)),
        name="ffn",
    )(x2, ada_all, w_in, w_in, w_out, lng, lnb)


def _s5_prep_inputs(lam_re, lam_im, log_dt, b_re, b_im, c_re, c_im, dskip):
    depth, g, p = lam_re.shape
    n = depth * g
    dup = lambda a: jnp.concatenate([a, a], axis=-1).reshape(n, 1, 2 * p)
    lr2 = dup(lam_re)
    li2 = dup(lam_im)
    ldt2 = jnp.broadcast_to(log_dt[..., None], (depth, g, 2 * p)).reshape(n, 1, 2 * p)
    btr = b_re.transpose(0, 1, 3, 2).reshape(n, S5_GROUP, p)
    bti = b_im.transpose(0, 1, 3, 2).reshape(n, S5_GROUP, p)
    bt = jnp.concatenate([btr, bti], axis=-1)
    btsw = jnp.concatenate([bti, btr], axis=-1)
    cr = c_re.reshape(n, S5_GROUP, p)
    ci = c_im.reshape(n, S5_GROUP, p)
    cw1 = jnp.concatenate([cr, -ci], axis=-1)
    cw2 = jnp.concatenate([-ci, -cr], axis=-1)
    dsk = jnp.broadcast_to(dskip.reshape(n, S5_GROUP, 1), (n, S5_GROUP, LANES))
    return lr2, li2, ldt2, bt, btsw, cw1, cw2, dsk


def _s5_scan_multipliers(lam_re, lam_im, log_dt, nch):
    depth, g, p = lam_re.shape
    dt = jnp.exp(log_dt)[..., None, None]
    nsteps = max(1, (nch - 1).bit_length())
    k = jnp.asarray(CHUNK * (2.0 ** np.arange(nsteps)), F32)
    mag = jnp.exp(lam_re[..., None] * dt * k)
    ang = lam_im[..., None] * dt * k
    sr = (mag * jnp.cos(ang)).reshape(depth * g, p, nsteps)
    si = (mag * jnp.sin(ang)).reshape(depth * g, p, nsteps)
    pad = ((0, 0), (0, 0), (0, S5_SCAN_COLS - nsteps))
    pa = jnp.pad(jnp.concatenate([sr, sr], axis=1), pad)
    pb = jnp.pad(jnp.concatenate([-si, si], axis=1), pad)
    return pa, pb


def _rope_tables(seq, scale):
    inv_freq = 1.0 / (ROPE_THETA ** (jnp.arange(0, QK_ROPE, 2, dtype=F32) / QK_ROPE))
    ang = jnp.arange(seq, dtype=F32)[:, None] * inv_freq[None, :]
    cos = jnp.cos(ang)
    sin = jnp.sin(ang)
    zk = jnp.zeros((seq, LANES - QK_ROPE), F32)
    cosk = jnp.concatenate([cos, cos, zk], axis=1)
    sink = jnp.concatenate([-sin, sin, zk], axis=1)
    zq = jnp.zeros((seq, HEAD_PAD - QK_NOPE - QK_ROPE), F32)
    aq = jnp.concatenate([jnp.full((seq, QK_NOPE), scale, F32), scale * cos, scale * cos, zq], axis=1)
    sq = jnp.concatenate([jnp.zeros((seq, QK_NOPE), F32), -scale * sin, scale * sin, zq], axis=1)
    return cosk, sink, aq.T, sq.T


def _swap_halves(w):
    half = w.shape[-1] // 2
    return jnp.concatenate([w[..., half:], w[..., :half]], axis=-1)


def kernel(x, c, w_ada, b_ada, w_in, b_in, s5_lambda_re, s5_lambda_im, s5_log_dt, s5_b_re, s5_b_im, s5_c_re, s5_c_im, s5_d, s5_w_glu, s5_b_glu, mla_q_norm, mla_w_q_up, mla_kv_norm, mla_w_kv_up, sgu_ln_g, sgu_ln_b, sgu_w_s, sgu_b_s, w_branch, w_out, ln1_g, ln1_b, ffn_w_in, ffn_w_out, ln2_g, ln2_b):
    batch, seq, d = x.shape
    depth = w_ada.shape[0]
    t = batch * seq
    alpha = float((2 * depth) ** 0.25)
    tm = 512
    tq = 512
    nch = seq // CHUNK
    assert seq % FLASH_TK == 0 and FLASH_TK % tm == 0 and tm == tq and tm % SGU_CHUNK == 0
    assert (batch * nch) % LANES == 0

    ada_all = _ada_call(c, w_ada, b_ada).reshape(depth, batch, 6, d)

    scale = LOG2E * (QK_NOPE + QK_ROPE) ** -0.5
    cosk, sink, aqt, sqt = _rope_tables(seq, scale)

    offs = np.cumsum([0, S5_WIDTH, Q_LORA, KV_LORA, QK_ROPE, SGU_WIDTH, SGU_WIDTH])
    o_u5, o_cq, o_ckv, o_kpe, o_usgu, o_vsgu, o_gate = [int(o) for o in offs]
    row = lambda a: a[:, None, :]

    kpe_b = b_in[:, o_kpe:o_kpe + QK_ROPE]
    w_a, w_g = _win_prep_call(w_in, o_kpe, o_usgu, o_gate)
    b_a = row(jnp.concatenate([b_in[:, o_u5:o_kpe], b_in[:, o_usgu:o_gate], kpe_b, _swap_halves(kpe_b),
                               jnp.zeros((depth, LANES - 2 * QK_ROPE), F32)], axis=1))
    b_g = row(b_in[:, o_gate:])
    pos = np.arange(SGU_CHUNK) // CHUNK
    sgu_mask = jnp.asarray(pos[None, :] <= pos[:, None])
    ws_m = jnp.where(sgu_mask[None, None], sgu_w_s, 0.0).astype(BF16)
    bs_full = jnp.repeat(sgu_b_s.transpose(0, 2, 1), SGU_WIDTH // SGU_GROUPS, axis=2)

    hw = MLA_HEADS * HEAD_PAD
    vw = MLA_HEADS * V_PAD
    wq4 = mla_w_q_up.reshape(depth, Q_LORA, MLA_HEADS, QK_NOPE + QK_ROPE)
    wqt = (jnp.concatenate([wq4, _swap_halves(wq4[..., QK_NOPE:])], axis=3)
           .reshape(depth, Q_LORA, hw).transpose(0, 2, 1).astype(BF16))
    wkv4 = mla_w_kv_up.reshape(depth, KV_LORA, MLA_HEADS, QK_NOPE + V_HEAD)
    wk = jnp.concatenate([wkv4[..., :QK_NOPE], jnp.zeros((depth, KV_LORA, MLA_HEADS, HEAD_PAD - QK_NOPE), F32)],
                         axis=3).reshape(depth, KV_LORA, hw).astype(BF16)
    wvt = (jnp.concatenate([wkv4[..., QK_NOPE:], jnp.zeros((depth, KV_LORA, MLA_HEADS, V_PAD - V_HEAD), F32)],
                           axis=3).reshape(depth, KV_LORA, vw).transpose(0, 2, 1).astype(BF16))
    place = np.zeros((LANES, hw), np.float32)
    for hh in range(MLA_HEADS):
        place[np.arange(QK_ROPE), hh * HEAD_PAD + QK_NOPE + np.arange(QK_ROPE)] = 1.0
    pk = jnp.asarray(place, BF16)
    ones_col = np.zeros((vw, 1), np.float32)
    ones_col[np.arange(MLA_HEADS) * V_PAD + V_HEAD, 0] = 1.0
    ones_col = jnp.asarray(ones_col)

    wglu = s5_w_glu.astype(BF16)
    wbr = w_branch.astype(BF16)
    wout = w_out.astype(BF16)
    ffn_in = ffn_w_in.astype(BF16)
    ffn_out = ffn_w_out.astype(BF16)

    dtb, wst, wot = _s5_prep_call(*_s5_prep_inputs(s5_lambda_re, s5_lambda_im, s5_log_dt, s5_b_re, s5_b_im,
                                                   s5_c_re, s5_c_im, s5_d))
    pa, pb = _s5_scan_multipliers(s5_lambda_re, s5_lambda_im, s5_log_dt, nch)

    x2 = x.reshape(t, d)
    for l in range(depth):
        h, u5, ysgu, kmat, qt, vt = _inproj_call(
            x2, ada_all, l, w_a, b_a, row(mla_q_norm), row(mla_kv_norm), row(sgu_ln_g), row(sgu_ln_b),
            ws_m, bs_full, wqt, wk, wvt, pk, ones_col, cosk, sink, aqt, sqt, seq, tm)
        ymla = _flash_call(kmat, qt, vt, batch, seq, tq, FLASH_TK)
        yssm = _s5_call(u5, dtb, wst, wot, pa, pb, l, nch)
        x2 = _merge_call(h, yssm, ymla, ysgu, x2, ada_all, l, w_g, b_g, wglu, row(s5_b_glu), wbr, wout,
                         row(ln1_g), row(ln1_b), seq, tm, alpha)
        x2 = _ffn_call(x2, ada_all, l, ffn_in, ffn_out, row(ln2_g), row(ln2_b), seq, tm, alpha)
    return x2.reshape(batch, seq, d)
```

```python
import functools
import math

import numpy as np
import jax
import jax.numpy as jnp
from jax import lax
from jax.experimental import pallas as pl
from jax.experimental.pallas import tpu as pltpu

F32 = jnp.float32
BF16 = jnp.bfloat16

CHUNK = 64
S5_WIDTH = 512
S5_GROUP = 16
S5_GROUPS = S5_WIDTH // S5_GROUP
S5_STATE = 64
MLA_HEADS = 8
QK_NOPE = 64
QK_ROPE = 32
V_HEAD = 64
Q_LORA = 384
KV_LORA = 256
ROPE_THETA = 10000.0
SGU_WIDTH = 512
SGU_GROUPS = 4
SGU_CHUNK = 128
N_BRANCH = 3
BRANCH_WIDTH = 512
LN_EPS = 1e-5
RMS_EPS = 1e-6
NEG_INF = -1e30

LANES = 128
SUBLANES = 8
BF16_ROWS = 16
HEAD_PAD = 128
V_PAD = V_HEAD + BF16_ROWS
S5_SUB = 16
S5_NSUB = CHUNK // S5_SUB
S5_BLK = S5_SUB * S5_GROUP
S5_FLAT = CHUNK * S5_GROUP
S5_GPB = LANES // S5_GROUP
S5_SCAN_COLS = 8
S5_REGROUP_SPLIT = 2
INPROJ_SUB = 256
MERGE_SUB = 256
FFN_SUB = 256
S5_PREP_GROUPS = 8
FLASH_SUB = 256
FLASH_TK = 1024
FLASH_DEPTH = 6

OFF_U5 = 0
OFF_CQ = OFF_U5 + S5_WIDTH
OFF_CKV = OFF_CQ + Q_LORA
OFF_USGU = OFF_CKV + KV_LORA
OFF_VSGU = OFF_USGU + SGU_WIDTH
OFF_KPE = OFF_VSGU + SGU_WIDTH
NA = OFF_KPE + LANES

VMEM_LIMIT = 56 * 1024 * 1024

HI = lax.Precision.HIGHEST
NT_DIMS = (((1,), (1,)), ((), ()))
LOG2E = 1.4426950408889634


def _gelu(x):
    return 0.5 * x * (1.0 + jnp.tanh(0.7978845608028654 * (x + 0.044715 * (x * x * x))))


def _sigmoid(x):
    return 0.5 * jnp.tanh(0.5 * x) + 0.5


def _layer_norm(r, g, b):
    mu = jnp.mean(r, axis=-1, keepdims=True)
    rc = r - mu
    var = jnp.mean(rc * rc, axis=-1, keepdims=True)
    return rc * lax.rsqrt(var + LN_EPS) * g + b


def _const_spec(shape):
    nd = len(shape)
    return pl.BlockSpec(shape, lambda *_: (0,) * nd, pipeline_mode=pl.Buffered(1))


def _layer_spec(arr, layer, col_block=None):
    shape = (1,) + arr.shape[1:]
    idx = (layer,) + (0,) * (arr.ndim - 1)
    if col_block is not None:
        width, j = col_block
        shape = shape[:-1] + (width,)
        idx = idx[:-1] + (j,)
    return pl.BlockSpec(shape, lambda *_: idx, pipeline_mode=pl.Buffered(1))


def _params(sem):
    return pltpu.CompilerParams(dimension_semantics=sem, vmem_limit_bytes=VMEM_LIMIT)


def _ada_kernel(cb_ref, w_ref, b_ref, o_ref, ca_sc):
    nb, d, _ = cb_ref.shape
    tn = w_ref.shape[2]

    @pl.when((pl.program_id(0) == 0) & (pl.program_id(1) == 0))
    def _silu():
        cb = cb_ref[...]
        ca_sc[...] = cb * _sigmoid(cb)

    def body(g, accs):
        r0 = pl.multiple_of(g * SUBLANES, SUBLANES)
        w = w_ref[0, pl.ds(r0, SUBLANES), :]
        out = []
        for b in range(nb):
            scale = jnp.concatenate([ca_sc[b, pl.ds(r0, SUBLANES), :]] * (tn // LANES), axis=1)
            out.append(accs[b] + w * scale)
        return tuple(out)

    accs = lax.fori_loop(0, d // SUBLANES, body, tuple(jnp.zeros((SUBLANES, tn), F32) for _ in range(nb)),
                         unroll=8)
    for b in range(nb):
        o_ref[0, b:b + 1, :] = jnp.sum(accs[b], axis=0, keepdims=True) + b_ref[0]


def _ada_call(c, w_ada, b_ada):
    depth, d, n6 = w_ada.shape
    batch = c.shape[0]
    tn = n6 // 8
    cb = jnp.broadcast_to(c[:, :, None], (batch, d, LANES))
    return pl.pallas_call(
        _ada_kernel,
        grid=(depth, n6 // tn),
        in_specs=[
            _const_spec(cb.shape),
            pl.BlockSpec((1, d, tn), lambda l, j: (l, 0, j)),
            pl.BlockSpec((1, 1, tn), lambda l, j: (l, 0, j)),
        ],
        out_specs=pl.BlockSpec((1, batch, tn), lambda l, j: (l, 0, j)),
        out_shape=jax.ShapeDtypeStruct((depth, batch, n6), F32),
        scratch_shapes=[pltpu.VMEM(cb.shape, F32)],
        compiler_params=_params(("arbitrary", "arbitrary")),
        name="ada",
    )(cb, w_ada, b_ada.reshape(depth, 1, n6))


def _win_prep_kernel(w_ref, wa_ref, wg_ref, *, o_kpe, o_usgu, o_gate):
    w = w_ref[0]
    half = QK_ROPE // 2
    kpe = w[:, o_kpe:o_kpe + QK_ROPE]
    last = jnp.concatenate([kpe, kpe[:, half:], kpe[:, :half],
                            jnp.zeros((w.shape[0], LANES - 2 * QK_ROPE), F32)], axis=1)
    wa_ref[0, :, 0:o_kpe] = w[:, 0:o_kpe].astype(BF16)
    wa_ref[0, :, o_kpe:OFF_KPE] = w[:, o_usgu:o_gate].astype(BF16)
    wa_ref[0, :, OFF_KPE:NA] = last.astype(BF16)
    wg_ref[0] = w[:, o_gate:].astype(BF16)


def _win_prep_call(w_in, o_kpe, o_usgu, o_gate):
    depth, d, n_in = w_in.shape
    tr = 256
    assert o_kpe == OFF_USGU and o_kpe + (o_gate - o_usgu) == OFF_KPE
    return pl.pallas_call(
        functools.partial(_win_prep_kernel, o_kpe=o_kpe, o_usgu=o_usgu, o_gate=o_gate),
        grid=(depth, d // tr),
        in_specs=[pl.BlockSpec((1, tr, n_in), lambda l, i: (l, i, 0))],
        out_specs=[pl.BlockSpec((1, tr, NA), lambda l, i: (l, i, 0)),
                   pl.BlockSpec((1, tr, n_in - o_gate), lambda l, i: (l, i, 0))],
        out_shape=[jax.ShapeDtypeStruct((depth, d, NA), BF16),
                   jax.ShapeDtypeStruct((depth, d, n_in - o_gate), BF16)],
        compiler_params=_params(("arbitrary", "arbitrary")),
        name="win_prep",
    )(w_in)


def _inproj_kernel(x_ref, ada_ref, w_ref, b_ref, qn_ref, kvn_ref, lng_ref, lnb_ref, ws_ref, bs_ref,
                   wqt_ref, wk_ref, wvt_ref, pk_ref, ones_ref, cosk_ref, sink_ref, aqt_ref, sqt_ref,
                   h_ref, u5_ref, ysgu_ref, k_ref, qt_ref, vt_ref):
    tm = x_ref.shape[0]
    ada = ada_ref[0, 0]
    sub = INPROJ_SUB
    gw = SGU_WIDTH // SGU_GROUPS

    def project(r):
        rows = slice(r * sub, (r + 1) * sub)
        h = (x_ref[rows, :] * (1.0 + ada[1:2]) + ada[0:1]).astype(BF16)
        h_ref[rows, :] = h
        return jnp.dot(h, w_ref[0], preferred_element_type=F32) + b_ref[0]

    def epilogue(r, acc):
        rows = slice(r * sub, (r + 1) * sub)
        u5_ref[rows, :] = acc[:, OFF_U5:OFF_U5 + S5_WIDTH]

        cq = acc[:, OFF_CQ:OFF_CQ + Q_LORA]
        cqn = (cq * lax.rsqrt(jnp.mean(cq * cq, axis=-1, keepdims=True) + RMS_EPS) * qn_ref[0]).astype(BF16)
        ckv = acc[:, OFF_CKV:OFF_CKV + KV_LORA]
        ckvn = (ckv * lax.rsqrt(jnp.mean(ckv * ckv, axis=-1, keepdims=True) + RMS_EPS) * kvn_ref[0]).astype(BF16)

        kb = acc[:, OFF_KPE:OFF_KPE + LANES]
        krot = (kb * cosk_ref[rows, :] + pltpu.roll(kb, LANES - QK_ROPE, 1) * sink_ref[rows, :]).astype(BF16)

        k_ref[rows, :] = (jnp.dot(ckvn, wk_ref[0], preferred_element_type=F32)
                          + jnp.dot(krot, pk_ref[...], preferred_element_type=F32)).astype(BF16)
        qt = lax.dot_general(wqt_ref[0], cqn, NT_DIMS, preferred_element_type=F32)
        aqt = aqt_ref[:, rows]
        sqt = sqt_ref[:, rows]
        for h in range(MLA_HEADS):
            hrow = slice(h * HEAD_PAD, (h + 1) * HEAD_PAD)
            qh = qt[hrow, :]
            qt_ref[0, hrow, rows] = (qh * aqt + pltpu.roll(qh, HEAD_PAD - QK_ROPE, 0) * sqt).astype(BF16)
        vt = lax.dot_general(wvt_ref[0], ckvn, NT_DIMS, preferred_element_type=F32) + ones_ref[...]
        vt_ref[0, :, rows] = vt.astype(BF16)

        us = _gelu(acc[:, OFF_USGU:OFF_USGU + SGU_WIDTH])
        vn = _layer_norm(_gelu(acc[:, OFF_VSGU:OFF_VSGU + SGU_WIDTH]), lng_ref[0], lnb_ref[0]).astype(BF16)
        for c in range(sub // SGU_CHUNK):
            crow = slice(c * SGU_CHUNK, (c + 1) * SGU_CHUNK)
            parts = [jnp.dot(ws_ref[0, g], vn[crow, g * gw:(g + 1) * gw], preferred_element_type=F32)
                     for g in range(SGU_GROUPS)]
            mixed = jnp.concatenate(parts, axis=1) + bs_ref[0]
            ysgu_ref[r * sub + c * SGU_CHUNK:r * sub + (c + 1) * SGU_CHUNK, :] = (us[crow, :] * mixed).astype(BF16)

    nsub = tm // sub
    acc = project(0)
    for r in range(nsub):
        nxt = project(r + 1) if r + 1 < nsub else None
        epilogue(r, acc)
        acc = nxt


def _inproj_call(x2, ada_all, layer, w, b, qn, kvn, lng, lnb, ws, bs, wqt, wk, wvt, pk, ones,
                 cosk, sink, aqt, sqt, seq, tm):
    t, d = x2.shape
    tpb = seq // tm
    hw = MLA_HEADS * HEAD_PAD
    vw = MLA_HEADS * V_PAD
    tok = lambda n: pl.BlockSpec((tm, n), lambda i: (i, 0))
    ktab = pl.BlockSpec((tm, LANES), lambda i: (i % tpb, 0))
    qtab = pl.BlockSpec((HEAD_PAD, tm), lambda i: (0, i % tpb))
    consts = [w, b, qn, kvn, lng, lnb, ws, bs, wqt, wk, wvt]
    return pl.pallas_call(
        _inproj_kernel,
        grid=(t // tm,),
        in_specs=[tok(d), pl.BlockSpec((1, 1, 6, d), lambda i: (layer, i // tpb, 0, 0))]
                 + [_layer_spec(a, layer) for a in consts]
                 + [_const_spec(pk.shape), _const_spec(ones.shape), ktab, ktab, qtab, qtab],
        out_specs=[tok(d), tok(S5_WIDTH), tok(SGU_WIDTH), tok(hw),
                   pl.BlockSpec((1, hw, tm), lambda i: (i, 0, 0)),
                   pl.BlockSpec((1, vw, tm), lambda i: (i, 0, 0))],
        out_shape=[
            jax.ShapeDtypeStruct((t, d), BF16),
            jax.ShapeDtypeStruct((t, S5_WIDTH), F32),
            jax.ShapeDtypeStruct((t, SGU_WIDTH), BF16),
            jax.ShapeDtypeStruct((t, hw), BF16),
            jax.ShapeDtypeStruct((t // tm, hw, tm), BF16),
            jax.ShapeDtypeStruct((t // tm, vw, tm), BF16),
        ],
        compiler_params=_params(("arbitrary",)),
        name="inproj",
    )(x2, ada_all, *consts, pk, ones, cosk, sink, aqt, sqt)


def _flash_kernel(qi_ref, ki_ref, kind_ref, k_ref, qt_ref, vt_ref, o_ref, m_sc, acc_sc):
    p_idx = pl.program_id(1)
    ki = ki_ref[p_idx]
    kind = kind_ref[p_idx]
    tk = k_ref.shape[0]
    tq = qt_ref.shape[2]
    kslabs = tk // tq

    @pl.when(ki == 0)
    def _init():
        m_sc[...] = jnp.full(m_sc.shape, NEG_INF, F32)
        acc_sc[...] = jnp.zeros(acc_sc.shape, F32)

    def step(q_off):
        sub = FLASH_SUB
        nkb, nqb = tk // sub, tq // sub
        state = [(h, qb) for h in range(MLA_HEADS) for qb in range(nqb)]
        m_all = {(h, qb): m_sc[h, :, qb * sub:(qb + 1) * sub] for h, qb in state}
        acc_all = {(h, qb): acc_sc[h, :, qb * sub:(qb + 1) * sub] for h, qb in state}
        blocks = [(h, qb, kb) for kb in range(nkb) for h, qb in state]
        if q_off is not None:
            blocks = [(h, qb, kb) for h, qb, kb in blocks
                      if kb * sub // CHUNK <= (q_off + (qb + 1) * sub - 1) // CHUNK]

        def scores(h, qb, kb):
            cols = slice(h * HEAD_PAD, (h + 1) * HEAD_PAD)
            st = jnp.dot(k_ref[kb * sub:(kb + 1) * sub, cols], qt_ref[0, cols, qb * sub:(qb + 1) * sub],
                         preferred_element_type=F32)
            if q_off is not None and ((kb + 1) * sub - 1) // CHUNK > (q_off + qb * sub) // CHUNK:
                kc = (lax.broadcasted_iota(jnp.int32, (sub, sub), 0) + kb * sub) // CHUNK
                qc = (lax.broadcasted_iota(jnp.int32, (sub, sub), 1) + q_off + qb * sub) // CHUNK
                st = jnp.where(kc <= qc, st, NEG_INF)
            return st

        def absorb(h, qb, kb, st):
            m_prev = m_all[(h, qb)]
            m_new = jnp.maximum(m_prev, jnp.max(st, axis=0, keepdims=True))
            m_all[(h, qb)] = m_new
            pt = jnp.exp2(st - m_new[0:1]).astype(BF16)
            slab, col = divmod(kb * sub, tq)
            acc_all[(h, qb)] = (jnp.exp2(m_prev - m_new)[0:1] * acc_all[(h, qb)]
                                + jnp.dot(vt_ref[slab, h * V_PAD:(h + 1) * V_PAD, col:col + sub], pt,
                                          preferred_element_type=F32))

        nblk = len(blocks)
        pending = {i: scores(*blocks[i]) for i in range(min(FLASH_DEPTH, nblk))}
        for i in range(nblk):
            if i + FLASH_DEPTH < nblk:
                pending[i + FLASH_DEPTH] = scores(*blocks[i + FLASH_DEPTH])
            absorb(*blocks[i], pending.pop(i))
        for h, qb in state:
            m_sc[h, :, qb * sub:(qb + 1) * sub] = m_all[(h, qb)]
            acc_sc[h, :, qb * sub:(qb + 1) * sub] = acc_all[(h, qb)]

    def finish():
        outs = []
        for h in range(MLA_HEADS):
            acc = acc_sc[h]
            outs.append(acc[:V_HEAD] / acc[V_HEAD:V_HEAD + 1])
        o_ref[...] = jnp.concatenate(outs, axis=0).T.astype(BF16)

    @pl.when(kind == 0)
    def _below():
        step(None)

    for slab in range(kslabs):
        @pl.when(kind == slab + 1)
        def _diag(slab=slab):
            step(slab * tq)
            finish()


def _flash_call(k, qt, vt, batch, seq, tq, tk):
    t, hw = k.shape
    vw = vt.shape[1]
    nq = seq // tq
    nk = seq // tk
    kslabs = tk // tq
    steps = [(a, b, 0 if b < a // kslabs else a % kslabs + 1) for a in range(nq) for b in range(a // kslabs + 1)]
    qi_tab, ki_tab, kind_tab = (jnp.asarray([s[i] for s in steps], jnp.int32) for i in range(3))
    grid_spec = pltpu.PrefetchScalarGridSpec(
        num_scalar_prefetch=3,
        grid=(batch, len(steps)),
        in_specs=[
            pl.BlockSpec((tk, hw), lambda b, p, qt_, kt_, kd_: (b * nk + kt_[p], 0)),
            pl.BlockSpec((1, hw, tq), lambda b, p, qt_, kt_, kd_: (b * nq + qt_[p], 0, 0)),
            pl.BlockSpec((kslabs, vw, tq), lambda b, p, qt_, kt_, kd_: (b * nk + kt_[p], 0, 0)),
        ],
        out_specs=pl.BlockSpec((tq, MLA_HEADS * V_HEAD), lambda b, p, qt_, kt_, kd_: (b * nq + qt_[p], 0)),
        scratch_shapes=[
            pltpu.VMEM((MLA_HEADS, SUBLANES, tq), F32),
            pltpu.VMEM((MLA_HEADS, V_PAD, tq), F32),
        ],
    )
    return pl.pallas_call(
        _flash_kernel,
        grid_spec=grid_spec,
        out_shape=jax.ShapeDtypeStruct((t, MLA_HEADS * V_HEAD), BF16),
        compiler_params=_params(("arbitrary", "arbitrary")),
        name="flash",
    )(qi_tab, ki_tab, kind_tab, k, qt, vt)


def _s5_prep_kernel(lr_ref, li_ref, ldt_ref, bt_ref, btsw_ref, cw1_ref, cw2_ref, dsk_ref,
                    dt_ref, wst_ref, wot_ref, ws_sc):
    for q in range(S5_PREP_GROUPS):
        _s5_prep_group(lr_ref.at[q], li_ref.at[q], ldt_ref.at[q], bt_ref.at[q], btsw_ref.at[q], cw1_ref.at[q],
                       cw2_ref.at[q], dsk_ref.at[q], dt_ref.at[q], wst_ref.at[q], wot_ref.at[q], ws_sc)


def _s5_prep_group(lr_ref, li_ref, ldt_ref, bt_ref, btsw_ref, cw1_ref, cw2_ref, dsk_ref,
                   dt_ref, wst_ref, wot_ref, ws_sc):
    pw = 2 * S5_STATE
    lr = lr_ref[...]
    li = li_ref[...]
    dt = jnp.exp(ldt_ref[...])
    lane = lax.broadcasted_iota(jnp.int32, (1, pw), 1)
    sgn = jnp.where(lane < S5_STATE, -1.0, 1.0)
    nk = CHUNK + SUBLANES
    kk = lax.broadcasted_iota(jnp.int32, (nk, pw), 0).astype(F32)
    mag = jnp.exp(kk * (lr * dt))
    ang = kk * (li * dt)
    cs = jnp.cos(ang - jnp.where(lane < S5_STATE, 0.0, 0.5 * math.pi))
    sw = pltpu.roll(cs, S5_STATE, 1)
    ar = mag * jnp.where(lane < S5_STATE, cs, sw)
    ai = mag * jnp.where(lane < S5_STATE, sw, cs)
    ais = ai * sgn
    den = lr * lr + li * li
    a1r = ar[1:2]
    a1i = ai[1:2]
    f_re = ((a1r - 1.0) * lr + a1i * li) / den
    f_im = (a1i * lr - (a1r - 1.0) * li) / den
    bt = bt_ref[...]
    btsw = btsw_ref[...]
    bb = f_re * bt + (f_im * sgn) * btsw
    bbsw = f_re * btsw - (f_im * sgn) * bt
    cw1 = cw1_ref[...]
    cw2 = cw2_ref[...]
    for s in range(CHUNK):
        k = CHUNK - 1 - s
        ws_sc[s * S5_GROUP:(s + 1) * S5_GROUP, :] = ar[k:k + 1] * bb + ais[k:k + 1] * bbsw
        wot_ref[s * S5_GROUP:(s + 1) * S5_GROUP, :] = (ar[s + 1:s + 2] * cw1 + ai[s + 1:s + 2] * cw2).astype(BF16)
    ws = ws_sc[...]
    wst_ref[...] = ws.T.astype(BF16)
    strip = lax.dot_general(cw1, ws, NT_DIMS, preferred_element_type=F32, precision=HI)
    row = lax.broadcasted_iota(jnp.int32, (S5_GROUP, S5_FLAT), 0)
    col = lax.broadcasted_iota(jnp.int32, (S5_GROUP, S5_FLAT), 1)
    dsk = jnp.concatenate([dsk_ref[...]] * (S5_FLAT // LANES), axis=1)
    strip = strip + jnp.where(col == row + (S5_FLAT - S5_GROUP), dsk, 0.0)
    strip = jnp.concatenate([strip, jnp.zeros((S5_GROUP, S5_BLK), F32)], axis=1)
    for d in range(S5_NSUB):
        for tt in range(S5_SUB):
            i0 = CHUNK - 1 - S5_SUB * d - tt
            dt_ref[d, tt * S5_GROUP:(tt + 1) * S5_GROUP, :] = (
                strip[:, S5_GROUP * i0:S5_GROUP * i0 + S5_BLK].astype(BF16))


def _s5_prep_call(lr2, li2, ldt2, bt, btsw, cw1, cw2, dsk):
    n = lr2.shape[0]
    pw = 2 * S5_STATE
    ng = S5_PREP_GROUPS
    vec = pl.BlockSpec((ng, 1, pw), lambda i: (i, 0, 0))
    mat = pl.BlockSpec((ng, S5_GROUP, pw), lambda i: (i, 0, 0))
    return pl.pallas_call(
        _s5_prep_kernel,
        grid=(n // ng,),
        in_specs=[vec, vec, vec, mat, mat, mat, mat, mat],
        out_specs=[
            pl.BlockSpec((ng, S5_NSUB, S5_BLK, S5_BLK), lambda i: (i, 0, 0, 0)),
            pl.BlockSpec((ng, pw, S5_FLAT), lambda i: (i, 0, 0)),
            pl.BlockSpec((ng, S5_FLAT, pw), lambda i: (i, 0, 0)),
        ],
        out_shape=[
            jax.ShapeDtypeStruct((n, S5_NSUB, S5_BLK, S5_BLK), BF16),
            jax.ShapeDtypeStruct((n, pw, S5_FLAT), BF16),
            jax.ShapeDtypeStruct((n, S5_FLAT, pw), BF16),
        ],
        scratch_shapes=[pltpu.VMEM((S5_FLAT, pw), F32)],
        compiler_params=_params(("arbitrary",)),
        name="s5_prep",
    )(lr2, li2, ldt2, bt, btsw, cw1, cw2, dsk)


def _s5_kernel(u_ref, dt_ref, wst_ref, wot_ref, pa_ref, pb_ref, y_ref, ut_sc, yt_sc, *, nch):
    gg = pl.program_id(1)
    ncol = ut_sc.shape[2]
    pw = 2 * S5_STATE

    nhalf = S5_REGROUP_SPLIT
    hcol = ncol // nhalf

    @pl.when(gg == 0)
    def _regroup():
        for hb in range(nhalf):
            x = u_ref[hb * hcol * CHUNK:(hb + 1) * hcol * CHUNK, :].reshape(hcol, CHUNK, LANES)
            xs = jnp.swapaxes(x, 0, 1)
            for s in range(CHUNK):
                ut_sc[s, :, hb * hcol:(hb + 1) * hcol] = xs[s].T.astype(BF16)

    row0 = pl.multiple_of(gg * S5_GROUP, S5_GROUP)
    u = ut_sc[:, pl.ds(row0, S5_GROUP), :].reshape(S5_FLAT, ncol)
    h = jnp.dot(wst_ref[0], u, preferred_element_type=F32)
    intra = []
    for jj in range(S5_NSUB):
        acc = None
        for ii in range(jj + 1):
            term = jnp.dot(dt_ref[0, jj - ii], u[ii * S5_BLK:(ii + 1) * S5_BLK, :], preferred_element_type=F32)
            acc = term if acc is None else acc + term
        intra.append(acc)
    n_idx = lax.broadcasted_iota(jnp.int32, (pw, ncol), 1) % nch
    pa = pa_ref[0]
    pb = pb_ref[0]
    step = 1
    j = 0
    while step < nch:
        sh = jnp.where(n_idx >= step, pltpu.roll(h, step, 1), 0.0)
        h = h + pa[:, j:j + 1] * sh + pb[:, j:j + 1] * pltpu.roll(sh, S5_STATE, 0)
        step *= 2
        j += 1
    hprev = jnp.where(n_idx >= 1, pltpu.roll(h, 1, 1), 0.0).astype(BF16)
    for jj in range(S5_NSUB):
        acc = intra[jj] + jnp.dot(wot_ref[0, jj * S5_BLK:(jj + 1) * S5_BLK, :], hprev, preferred_element_type=F32)
        yt_sc[jj * S5_SUB:(jj + 1) * S5_SUB, pl.ds(row0, S5_GROUP), :] = acc.reshape(S5_SUB, S5_GROUP, ncol)

    @pl.when(gg == S5_GPB - 1)
    def _emit():
        for hb in range(nhalf):
            ys = jnp.stack([yt_sc[tt, :, hb * hcol:(hb + 1) * hcol].T for tt in range(CHUNK)], axis=0)
            y_ref[hb * hcol * CHUNK:(hb + 1) * hcol * CHUNK, :] = jnp.swapaxes(ys, 0, 1).reshape(hcol * CHUNK, LANES)


def _s5_call(u5, dtb, wst, wot, pa, pb, layer, nch):
    t, width = u5.shape
    ncol = t // CHUNK
    nblk = width // LANES
    base = layer * S5_GROUPS
    grp = lambda shape: pl.BlockSpec((1,) + shape, lambda j, g: (base + j * S5_GPB + g,) + (0,) * len(shape))
    return pl.pallas_call(
        functools.partial(_s5_kernel, nch=nch),
        grid=(nblk, S5_GPB),
        in_specs=[
            pl.BlockSpec((t, LANES), lambda j, g: (0, j)),
            grp(dtb.shape[1:]), grp(wst.shape[1:]), grp(wot.shape[1:]), grp(pa.shape[1:]), grp(pb.shape[1:]),
        ],
        out_specs=pl.BlockSpec((t, LANES), lambda j, g: (0, j)),
        out_shape=jax.ShapeDtypeStruct((t, width), F32),
        scratch_shapes=[pltpu.VMEM((CHUNK, LANES, ncol), BF16), pltpu.VMEM((CHUNK, LANES, ncol), F32)],
        compiler_params=_params(("arbitrary", "arbitrary")),
        name="s5",
    )(u5, dtb, wst, wot, pa, pb)


def _merge_kernel(h_ref, yssm_ref, ymla_ref, ysgu_ref, x_ref, ada_ref,
                  wg_ref, bg_ref, wglu_ref, bglu_ref, wb_ref, wout_ref, lng_ref, lnb_ref,
                  o_ref, *, alpha):
    tm, d = x_ref.shape
    sub = MERGE_SUB
    g1 = ada_ref[0, 0][2:3]

    def merge(r):
        rows = slice(r * sub, (r + 1) * sub)
        h = h_ref[rows, :]
        z = _gelu(yssm_ref[rows, :])
        y5 = (z * _sigmoid(jnp.dot(z.astype(BF16), wglu_ref[0], preferred_element_type=F32)
                           + bglu_ref[0])).astype(BF16)
        branches = (y5, ymla_ref[rows, :], ysgu_ref[rows, :])
        merged = None
        for i in range(N_BRANCH):
            cols = slice(i * d, (i + 1) * d)
            gate = _sigmoid(jnp.dot(h, wg_ref[0, :, cols], preferred_element_type=F32) + bg_ref[0, :, cols])
            term = gate * jnp.dot(branches[i], wb_ref[0, i], preferred_element_type=F32)
            merged = term if merged is None else merged + term
        return merged.astype(BF16)

    def project(r, merged):
        rows = slice(r * sub, (r + 1) * sub)
        y = jnp.dot(merged, wout_ref[0], preferred_element_type=F32)
        o_ref[rows, :] = _layer_norm(alpha * x_ref[rows, :] + (1.0 + g1) * y, lng_ref[0], lnb_ref[0])

    nsub = tm // sub
    merged = merge(0)
    for r in range(nsub):
        nxt = merge(r + 1) if r + 1 < nsub else None
        project(r, merged)
        merged = nxt


def _merge_call(h, yssm, ymla, ysgu, x2, ada_all, layer, wg, bg, wglu, bglu, wb, wout, lng, lnb, seq, tm, alpha):
    t, d = x2.shape
    tpb = seq // tm
    tok = lambda n: pl.BlockSpec((tm, n), lambda i: (i, 0))
    consts = [wg, bg, wglu, bglu, wb, wout, lng, lnb]
    return pl.pallas_call(
        functools.partial(_merge_kernel, alpha=alpha),
        grid=(t // tm,),
        in_specs=[tok(d), tok(S5_WIDTH), tok(BRANCH_WIDTH), tok(BRANCH_WIDTH), tok(d),
                  pl.BlockSpec((1, 1, 6, d), lambda i: (layer, i // tpb, 0, 0))]
                 + [_layer_spec(a, layer) for a in consts],
        out_specs=tok(d),
        out_shape=jax.ShapeDtypeStruct((t, d), F32),
        compiler_params=_params(("arbitrary",)),
        name="merge",
    )(h, yssm, ymla, ysgu, x2, ada_all, *consts)


def _ffn_kernel(x_ref, ada_ref, wa_ref, wb_ref, wo_ref, lng_ref, lnb_ref, o_ref, *, alpha):
    tm = x_ref.shape[0]
    sub = FFN_SUB
    ada = ada_ref[0, 0]

    def hidden(r):
        rows = slice(r * sub, (r + 1) * sub)
        h = (x_ref[rows, :] * (1.0 + ada[4:5]) + ada[3:4]).astype(BF16)
        a = jnp.dot(h, wa_ref[0], preferred_element_type=F32)
        b = jnp.dot(h, wb_ref[0], preferred_element_type=F32)
        return (a * _sigmoid(a) * b).astype(BF16)

    def project(r, act):
        rows = slice(r * sub, (r + 1) * sub)
        f = jnp.dot(act, wo_ref[0], preferred_element_type=F32)
        o_ref[rows, :] = _layer_norm(alpha * x_ref[rows, :] + (1.0 + ada[5:6]) * f, lng_ref[0], lnb_ref[0])

    nsub = tm // sub
    act = hidden(0)
    for r in range(nsub):
        nxt = hidden(r + 1) if r + 1 < nsub else None
        project(r, act)
        act = nxt


def _ffn_call(x2, ada_all, layer, w_in, w_out, lng, lnb, seq, tm, alpha):
    t, d = x2.shape
    ff = w_out.shape[1]
    tpb = seq // tm
    tok = pl.BlockSpec((tm, d), lambda i: (i, 0))
    return pl.pallas_call(
        functools.partial(_ffn_kernel, alpha=alpha),
        grid=(t // tm,),
        in_specs=[tok, pl.BlockSpec((1, 1, 6, d), lambda i: (layer, i // tpb, 0, 0)),
                  _layer_spec(w_in, layer, (ff, 0)), _layer_spec(w_in, layer, (ff, 1)), _layer_spec(w_out, layer),
                  _layer_spec(lng, layer), _layer_spec(lnb, layer)],
        out_specs=tok,
        out_shape=jax.ShapeDtypeStruct((t, d), F32),
        compiler_params=_params(("arbitrary",)),
        name="ffn",
    )(x2, ada_all, w_in, w_in, w_out, lng, lnb)


def _s5_prep_inputs(lam_re, lam_im, log_dt, b_re, b_im, c_re, c_im, dskip):
    depth, g, p = lam_re.shape
    n = depth * g
    dup = lambda a: jnp.concatenate([a, a], axis=-1).reshape(n, 1, 2 * p)
    lr2 = dup(lam_re)
    li2 = dup(lam_im)
    ldt2 = jnp.broadcast_to(log_dt[..., None], (depth, g, 2 * p)).reshape(n, 1, 2 * p)
    btr = b_re.transpose(0, 1, 3, 2).reshape(n, S5_GROUP, p)
    bti = b_im.transpose(0, 1, 3, 2).reshape(n, S5_GROUP, p)
    bt = jnp.concatenate([btr, bti], axis=-1)
    btsw = jnp.concatenate([bti, btr], axis=-1)
    cr = c_re.reshape(n, S5_GROUP, p)
    ci = c_im.reshape(n, S5_GROUP, p)
    cw1 = jnp.concatenate([cr, -ci], axis=-1)
    cw2 = jnp.concatenate([-ci, -cr], axis=-1)
    dsk = jnp.broadcast_to(dskip.reshape(n, S5_GROUP, 1), (n, S5_GROUP, LANES))
    return lr2, li2, ldt2, bt, btsw, cw1, cw2, dsk


def _s5_scan_multipliers(lam_re, lam_im, log_dt, nch):
    depth, g, p = lam_re.shape
    dt = jnp.exp(log_dt)[..., None, None]
    nsteps = max(1, (nch - 1).bit_length())
    k = jnp.asarray(CHUNK * (2.0 ** np.arange(nsteps)), F32)
    mag = jnp.exp(lam_re[..., None] * dt * k)
    ang = lam_im[..., None] * dt * k
    sr = (mag * jnp.cos(ang)).reshape(depth * g, p, nsteps)
    si = (mag * jnp.sin(ang)).reshape(depth * g, p, nsteps)
    pad = ((0, 0), (0, 0), (0, S5_SCAN_COLS - nsteps))
    pa = jnp.pad(jnp.concatenate([sr, sr], axis=1), pad)
    pb = jnp.pad(jnp.concatenate([-si, si], axis=1), pad)
    return pa, pb


def _rope_tables(seq, scale):
    inv_freq = 1.0 / (ROPE_THETA ** (jnp.arange(0, QK_ROPE, 2, dtype=F32) / QK_ROPE))
    ang = jnp.arange(seq, dtype=F32)[:, None] * inv_freq[None, :]
    cos = jnp.cos(ang)
    sin = jnp.sin(ang)
    zk = jnp.zeros((seq, LANES - QK_ROPE), F32)
    cosk = jnp.concatenate([cos, cos, zk], axis=1)
    sink = jnp.concatenate([-sin, sin, zk], axis=1)
    zq = jnp.zeros((seq, HEAD_PAD - QK_NOPE - QK_ROPE), F32)
    aq = jnp.concatenate([jnp.full((seq, QK_NOPE), scale, F32), scale * cos, scale * cos, zq], axis=1)
    sq = jnp.concatenate([jnp.zeros((seq, QK_NOPE), F32), -scale * sin, scale * sin, zq], axis=1)
    return cosk, sink, aq.T, sq.T


def _swap_halves(w):
    half = w.shape[-1] // 2
    return jnp.concatenate([w[..., half:], w[..., :half]], axis=-1)


def kernel(x, c, w_ada, b_ada, w_in, b_in, s5_lambda_re, s5_lambda_im, s5_log_dt, s5_b_re, s5_b_im, s5_c_re, s5_c_im, s5_d, s5_w_glu, s5_b_glu, mla_q_norm, mla_w_q_up, mla_kv_norm, mla_w_kv_up, sgu_ln_g, sgu_ln_b, sgu_w_s, sgu_b_s, w_branch, w_out, ln1_g, ln1_b, ffn_w_in, ffn_w_out, ln2_g, ln2_b):
    batch, seq, d = x.shape
    depth = w_ada.shape[0]
    t = batch * seq
    alpha = float((2 * depth) ** 0.25)
    tm = 512
    tq = 512
    nch = seq // CHUNK
    assert seq % FLASH_TK == 0 and FLASH_TK % tm == 0 and tm == tq and tm % SGU_CHUNK == 0
    assert (batch * nch) % LANES == 0

    ada_all = _ada_call(c, w_ada, b_ada).reshape(depth, batch, 6, d)

    scale = LOG2E * (QK_NOPE + QK_ROPE) ** -0.5
    cosk, sink, aqt, sqt = _rope_tables(seq, scale)

    offs = np.cumsum([0, S5_WIDTH, Q_LORA, KV_LORA, QK_ROPE, SGU_WIDTH, SGU_WIDTH])
    o_u5, o_cq, o_ckv, o_kpe, o_usgu, o_vsgu, o_gate = [int(o) for o in offs]
    row = lambda a: a[:, None, :]

    kpe_b = b_in[:, o_kpe:o_kpe + QK_ROPE]
    w_a, w_g = _win_prep_call(w_in, o_kpe, o_usgu, o_gate)
    b_a = row(jnp.concatenate([b_in[:, o_u5:o_kpe], b_in[:, o_usgu:o_gate], kpe_b, _swap_halves(kpe_b),
                               jnp.zeros((depth, LANES - 2 * QK_ROPE), F32)], axis=1))
    b_g = row(b_in[:, o_gate:])
    pos = np.arange(SGU_CHUNK) // CHUNK
    sgu_mask = jnp.asarray(pos[None, :] <= pos[:, None])
    ws_m = jnp.where(sgu_mask[None, None], sgu_w_s, 0.0).astype(BF16)
    bs_full = jnp.repeat(sgu_b_s.transpose(0, 2, 1), SGU_WIDTH // SGU_GROUPS, axis=2)

    hw = MLA_HEADS * HEAD_PAD
    vw = MLA_HEADS * V_PAD
    wq4 = mla_w_q_up.reshape(depth, Q_LORA, MLA_HEADS, QK_NOPE + QK_ROPE)
    wqt = (jnp.concatenate([wq4, _swap_halves(wq4[..., QK_NOPE:])], axis=3)
           .reshape(depth, Q_LORA, hw).transpose(0, 2, 1).astype(BF16))
    wkv4 = mla_w_kv_up.reshape(depth, KV_LORA, MLA_HEADS, QK_NOPE + V_HEAD)
    wk = jnp.concatenate([wkv4[..., :QK_NOPE], jnp.zeros((depth, KV_LORA, MLA_HEADS, HEAD_PAD - QK_NOPE), F32)],
                         axis=3).reshape(depth, KV_LORA, hw).astype(BF16)
    wvt = (jnp.concatenate([wkv4[..., QK_NOPE:], jnp.zeros((depth, KV_LORA, MLA_HEADS, V_PAD - V_HEAD), F32)],
                           axis=3).reshape(depth, KV_LORA, vw).transpose(0, 2, 1).astype(BF16))
    place = np.zeros((LANES, hw), np.float32)
    for hh in range(MLA_HEADS):
        place[np.arange(QK_ROPE), hh * HEAD_PAD + QK_NOPE + np.arange(QK_ROPE)] = 1.0
    pk = jnp.asarray(place, BF16)
    ones_col = np.zeros((vw, 1), np.float32)
    ones_col[np.arange(MLA_HEADS) * V_PAD + V_HEAD, 0] = 1.0
    ones_col = jnp.asarray(ones_col)

    wglu = s5_w_glu.astype(BF16)
    wbr = w_branch.astype(BF16)
    wout = w_out.astype(BF16)
    ffn_in = ffn_w_in.astype(BF16)
    ffn_out = ffn_w_out.astype(BF16)

    dtb, wst, wot = _s5_prep_call(*_s5_prep_inputs(s5_lambda_re, s5_lambda_im, s5_log_dt, s5_b_re, s5_b_im,
                                                   s5_c_re, s5_c_im, s5_d))
    pa, pb = _s5_scan_multipliers(s5_lambda_re, s5_lambda_im, s5_log_dt, nch)

    x2 = x.reshape(t, d)
    for l in range(depth):
        h, u5, ysgu, kmat, qt, vt = _inproj_call(
            x2, ada_all, l, w_a, b_a, row(mla_q_norm), row(mla_kv_norm), row(sgu_ln_g), row(sgu_ln_b),
            ws_m, bs_full, wqt, wk, wvt, pk, ones_col, cosk, sink, aqt, sqt, seq, tm)
        ymla = _flash_call(kmat, qt, vt, batch, seq, tq, FLASH_TK)
        yssm = _s5_call(u5, dtb, wst, wot, pa, pb, l, nch)
        x2 = _merge_call(h, yssm, ymla, ysgu, x2, ada_all, l, w_g, b_g, wglu, row(s5_b_glu), wbr, wout,
                         row(ln1_g), row(ln1_b), seq, tm, alpha)
        x2 = _ffn_call(x2, ada_all, l, ffn_in, ffn_out, row(ln2_g), row(ln2_b), seq, tm, alpha)
    return x2.reshape(batch, seq, d)
```

```python
import functools
import math

import numpy as np
import jax
import jax.numpy as jnp
from jax import lax
from jax.experimental import pallas as pl
from jax.experimental.pallas import tpu as pltpu

F32 = jnp.float32
BF16 = jnp.bfloat16

CHUNK = 64
S5_WIDTH = 512
S5_GROUP = 16
S5_GROUPS = S5_WIDTH // S5_GROUP
S5_STATE = 64
MLA_HEADS = 8
QK_NOPE = 64
QK_ROPE = 32
V_HEAD = 64
Q_LORA = 384
KV_LORA = 256
ROPE_THETA = 10000.0
SGU_WIDTH = 512
SGU_GROUPS = 4
SGU_CHUNK = 128
N_BRANCH = 3
BRANCH_WIDTH = 512
LN_EPS = 1e-5
RMS_EPS = 1e-6
NEG_INF = -1e30

LANES = 128
SUBLANES = 8
BF16_ROWS = 16
HEAD_PAD = 128
V_PAD = V_HEAD + BF16_ROWS
S5_SUB = 16
S5_NSUB = CHUNK // S5_SUB
S5_BLK = S5_SUB * S5_GROUP
S5_FLAT = CHUNK * S5_GROUP
S5_GPB = LANES // S5_GROUP
S5_SCAN_COLS = 8
S5_REGROUP_SPLIT = 2
INPROJ_SUB = 256
MERGE_SUB = 256
FFN_SUB = 256
S5_PREP_GROUPS = 8
FLASH_SUB = 256
FLASH_TK = 1024
FLASH_DEPTH = 6

OFF_U5 = 0
OFF_CQ = OFF_U5 + S5_WIDTH
OFF_CKV = OFF_CQ + Q_LORA
OFF_USGU = OFF_CKV + KV_LORA
OFF_VSGU = OFF_USGU + SGU_WIDTH
OFF_KPE = OFF_VSGU + SGU_WIDTH
NA = OFF_KPE + LANES

VMEM_LIMIT = 56 * 1024 * 1024

HI = lax.Precision.HIGHEST
NT_DIMS = (((1,), (1,)), ((), ()))
LOG2E = 1.4426950408889634


def _gelu(x):
    return 0.5 * x * (1.0 + jnp.tanh(0.7978845608028654 * (x + 0.044715 * (x * x * x))))


def _sigmoid(x):
    return 0.5 * jnp.tanh(0.5 * x) + 0.5


def _layer_norm(r, g, b):
    mu = jnp.mean(r, axis=-1, keepdims=True)
    rc = r - mu
    var = jnp.mean(rc * rc, axis=-1, keepdims=True)
    return rc * lax.rsqrt(var + LN_EPS) * g + b


def _const_spec(shape):
    nd = len(shape)
    return pl.BlockSpec(shape, lambda *_: (0,) * nd, pipeline_mode=pl.Buffered(1))


def _layer_spec(arr, layer, col_block=None):
    shape = (1,) + arr.shape[1:]
    idx = (layer,) + (0,) * (arr.ndim - 1)
    if col_block is not None:
        width, j = col_block
        shape = shape[:-1] + (width,)
        idx = idx[:-1] + (j,)
    return pl.BlockSpec(shape, lambda *_: idx, pipeline_mode=pl.Buffered(1))


def _params(sem):
    return pltpu.CompilerParams(dimension_semantics=sem, vmem_limit_bytes=VMEM_LIMIT)


def _ada_kernel(cb_ref, w_ref, b_ref, o_ref, ca_sc):
    nb, d, _ = cb_ref.shape
    tn = w_ref.shape[2]

    @pl.when((pl.program_id(0) == 0) & (pl.program_id(1) == 0))
    def _silu():
        cb = cb_ref[...]
        ca_sc[...] = cb * _sigmoid(cb)

    def body(g, accs):
        r0 = pl.multiple_of(g * SUBLANES, SUBLANES)
        w = w_ref[0, pl.ds(r0, SUBLANES), :]
        out = []
        for b in range(nb):
            scale = jnp.concatenate([ca_sc[b, pl.ds(r0, SUBLANES), :]] * (tn // LANES), axis=1)
            out.append(accs[b] + w * scale)
        return tuple(out)

    accs = lax.fori_loop(0, d // SUBLANES, body, tuple(jnp.zeros((SUBLANES, tn), F32) for _ in range(nb)),
                         unroll=8)
    for b in range(nb):
        o_ref[0, b:b + 1, :] = jnp.sum(accs[b], axis=0, keepdims=True) + b_ref[0]


def _ada_call(c, w_ada, b_ada):
    depth, d, n6 = w_ada.shape
    batch = c.shape[0]
    tn = n6 // 8
    cb = jnp.broadcast_to(c[:, :, None], (batch, d, LANES))
    return pl.pallas_call(
        _ada_kernel,
        grid=(depth, n6 // tn),
        in_specs=[
            _const_spec(cb.shape),
            pl.BlockSpec((1, d, tn), lambda l, j: (l, 0, j)),
            pl.BlockSpec((1, 1, tn), lambda l, j: (l, 0, j)),
        ],
        out_specs=pl.BlockSpec((1, batch, tn), lambda l, j: (l, 0, j)),
        out_shape=jax.ShapeDtypeStruct((depth, batch, n6), F32),
        scratch_shapes=[pltpu.VMEM(cb.shape, F32)],
        compiler_params=_params(("arbitrary", "arbitrary")),
        name="ada",
    )(cb, w_ada, b_ada.reshape(depth, 1, n6))


def _win_prep_kernel(w_ref, wa_ref, wg_ref, *, o_kpe, o_usgu, o_gate):
    w = w_ref[0]
    half = QK_ROPE // 2
    kpe = w[:, o_kpe:o_kpe + QK_ROPE]
    last = jnp.concatenate([kpe, kpe[:, half:], kpe[:, :half],
                            jnp.zeros((w.shape[0], LANES - 2 * QK_ROPE), F32)], axis=1)
    wa_ref[0, :, 0:o_kpe] = w[:, 0:o_kpe].astype(BF16)
    wa_ref[0, :, o_kpe:OFF_KPE] = w[:, o_usgu:o_gate].astype(BF16)
    wa_ref[0, :, OFF_KPE:NA] = last.astype(BF16)
    wg_ref[0] = w[:, o_gate:].astype(BF16)


def _win_prep_call(w_in, o_kpe, o_usgu, o_gate):
    depth, d, n_in = w_in.shape
    tr = 256
    assert o_kpe == OFF_USGU and o_kpe + (o_gate - o_usgu) == OFF_KPE
    return pl.pallas_call(
        functools.partial(_win_prep_kernel, o_kpe=o_kpe, o_usgu=o_usgu, o_gate=o_gate),
        grid=(depth, d // tr),
        in_specs=[pl.BlockSpec((1, tr, n_in), lambda l, i: (l, i, 0))],
        out_specs=[pl.BlockSpec((1, tr, NA), lambda l, i: (l, i, 0)),
                   pl.BlockSpec((1, tr, n_in - o_gate), lambda l, i: (l, i, 0))],
        out_shape=[jax.ShapeDtypeStruct((depth, d, NA), BF16),
                   jax.ShapeDtypeStruct((depth, d, n_in - o_gate), BF16)],
        compiler_params=_params(("arbitrary", "arbitrary")),
        name="win_prep",
    )(w_in)


def _inproj_kernel(x_ref, ada_ref, w_ref, b_ref, qn_ref, kvn_ref, lng_ref, lnb_ref, ws_ref, bs_ref,
                   wqt_ref, wk_ref, wvt_ref, pk_ref, ones_ref, cosk_ref, sink_ref, aqt_ref, sqt_ref,
                   h_ref, u5_ref, ysgu_ref, k_ref, qt_ref, vt_ref):
    tm = x_ref.shape[0]
    ada = ada_ref[0, 0]
    sub = INPROJ_SUB
    gw = SGU_WIDTH // SGU_GROUPS

    def project(r):
        rows = slice(r * sub, (r + 1) * sub)
        h = (x_ref[rows, :] * (1.0 + ada[1:2]) + ada[0:1]).astype(BF16)
        h_ref[rows, :] = h
        return jnp.dot(h, w_ref[0], preferred_element_type=F32) + b_ref[0]

    def epilogue(r, acc):
        rows = slice(r * sub, (r + 1) * sub)
        u5_ref[rows, :] = acc[:, OFF_U5:OFF_U5 + S5_WIDTH]

        cq = acc[:, OFF_CQ:OFF_CQ + Q_LORA]
        cqn = (cq * lax.rsqrt(jnp.mean(cq * cq, axis=-1, keepdims=True) + RMS_EPS) * qn_ref[0]).astype(BF16)
        ckv = acc[:, OFF_CKV:OFF_CKV + KV_LORA]
        ckvn = (ckv * lax.rsqrt(jnp.mean(ckv * ckv, axis=-1, keepdims=True) + RMS_EPS) * kvn_ref[0]).astype(BF16)

        kb = acc[:, OFF_KPE:OFF_KPE + LANES]
        krot = (kb * cosk_ref[rows, :] + pltpu.roll(kb, LANES - QK_ROPE, 1) * sink_ref[rows, :]).astype(BF16)

        k_ref[rows, :] = (jnp.dot(ckvn, wk_ref[0], preferred_element_type=F32)
                          + jnp.dot(krot, pk_ref[...], preferred_element_type=F32)).astype(BF16)
        qt = lax.dot_general(wqt_ref[0], cqn, NT_DIMS, preferred_element_type=F32)
        aqt = aqt_ref[:, rows]
        sqt = sqt_ref[:, rows]
        for h in range(MLA_HEADS):
            hrow = slice(h * HEAD_PAD, (h + 1) * HEAD_PAD)
            qh = qt[hrow, :]
            qt_ref[0, hrow, rows] = (qh * aqt + pltpu.roll(qh, HEAD_PAD - QK_ROPE, 0) * sqt).astype(BF16)
        vt = lax.dot_general(wvt_ref[0], ckvn, NT_DIMS, preferred_element_type=F32) + ones_ref[...]
        vt_ref[0, :, rows] = vt.astype(BF16)

        us = _gelu(acc[:, OFF_USGU:OFF_USGU + SGU_WIDTH])
        vn = _layer_norm(_gelu(acc[:, OFF_VSGU:OFF_VSGU + SGU_WIDTH]), lng_ref[0], lnb_ref[0]).astype(BF16)
        for c in range(sub // SGU_CHUNK):
            crow = slice(c * SGU_CHUNK, (c + 1) * SGU_CHUNK)
            parts = [jnp.dot(ws_ref[0, g], vn[crow, g * gw:(g + 1) * gw], preferred_element_type=F32)
                     for g in range(SGU_GROUPS)]
            mixed = jnp.concatenate(parts, axis=1) + bs_ref[0]
            ysgu_ref[r * sub + c * SGU_CHUNK:r * sub + (c + 1) * SGU_CHUNK, :] = (us[crow, :] * mixed).astype(BF16)

    nsub = tm // sub
    acc = project(0)
    for r in range(nsub):
        nxt = project(r + 1) if r + 1 < nsub else None
        epilogue(r, acc)
        acc = nxt


def _inproj_call(x2, ada_all, layer, w, b, qn, kvn, lng, lnb, ws, bs, wqt, wk, wvt, pk, ones,
                 cosk, sink, aqt, sqt, seq, tm):
    t, d = x2.shape
    tpb = seq // tm
    hw = MLA_HEADS * HEAD_PAD
    vw = MLA_HEADS * V_PAD
    tok = lambda n: pl.BlockSpec((tm, n), lambda i: (i, 0))
    ktab = pl.BlockSpec((tm, LANES), lambda i: (i % tpb, 0))
    qtab = pl.BlockSpec((HEAD_PAD, tm), lambda i: (0, i % tpb))
    consts = [w, b, qn, kvn, lng, lnb, ws, bs, wqt, wk, wvt]
    return pl.pallas_call(
        _inproj_kernel,
        grid=(t // tm,),
        in_specs=[tok(d), pl.BlockSpec((1, 1, 6, d), lambda i: (layer, i // tpb, 0, 0))]
                 + [_layer_spec(a, layer) for a in consts]
                 + [_const_spec(pk.shape), _const_spec(ones.shape), ktab, ktab, qtab, qtab],
        out_specs=[tok(d), tok(S5_WIDTH), tok(SGU_WIDTH), tok(hw),
                   pl.BlockSpec((1, hw, tm), lambda i: (i, 0, 0)),
                   pl.BlockSpec((1, vw, tm), lambda i: (i, 0, 0))],
        out_shape=[
            jax.ShapeDtypeStruct((t, d), BF16),
            jax.ShapeDtypeStruct((t, S5_WIDTH), F32),
            jax.ShapeDtypeStruct((t, SGU_WIDTH), BF16),
            jax.ShapeDtypeStruct((t, hw), BF16),
            jax.ShapeDtypeStruct((t // tm, hw, tm), BF16),
            jax.ShapeDtypeStruct((t // tm, vw, tm), BF16),
        ],
        compiler_params=_params(("arbitrary",)),
        name="inproj",
    )(x2, ada_all, *consts, pk, ones, cosk, sink, aqt, sqt)


def _flash_kernel(qi_ref, ki_ref, kind_ref, k_ref, qt_ref, vt_ref, o_ref, m_sc, acc_sc):
    p_idx = pl.program_id(1)
    ki = ki_ref[p_idx]
    kind = kind_ref[p_idx]
    tk = k_ref.shape[0]
    tq = qt_ref.shape[2]
    kslabs = tk // tq

    @pl.when(ki == 0)
    def _init():
        m_sc[...] = jnp.full(m_sc.shape, NEG_INF, F32)
        acc_sc[...] = jnp.zeros(acc_sc.shape, F32)

    def step(q_off):
        sub = FLASH_SUB
        nkb, nqb = tk // sub, tq // sub
        state = [(h, qb) for h in range(MLA_HEADS) for qb in range(nqb)]
        m_all = {(h, qb): m_sc[h, :, qb * sub:(qb + 1) * sub] for h, qb in state}
        acc_all = {(h, qb): acc_sc[h, :, qb * sub:(qb + 1) * sub] for h, qb in state}
        blocks = [(h, qb, kb) for kb in range(nkb) for h, qb in state]
        if q_off is not None:
            blocks = [(h, qb, kb) for h, qb, kb in blocks
                      if kb * sub // CHUNK <= (q_off + (qb + 1) * sub - 1) // CHUNK]

        def scores(h, qb, kb):
            cols = slice(h * HEAD_PAD, (h + 1) * HEAD_PAD)
            st = jnp.dot(k_ref[kb * sub:(kb + 1) * sub, cols], qt_ref[0, cols, qb * sub:(qb + 1) * sub],
                         preferred_element_type=F32)
            if q_off is not None and ((kb + 1) * sub - 1) // CHUNK > (q_off + qb * sub) // CHUNK:
                kc = (lax.broadcasted_iota(jnp.int32, (sub, sub), 0) + kb * sub) // CHUNK
                qc = (lax.broadcasted_iota(jnp.int32, (sub, sub), 1) + q_off + qb * sub) // CHUNK
                st = jnp.where(kc <= qc, st, NEG_INF)
            return st

        def absorb(h, qb, kb, st):
            m_prev = m_all[(h, qb)]
            m_new = jnp.maximum(m_prev, jnp.max(st, axis=0, keepdims=True))
            m_all[(h, qb)] = m_new
            pt = jnp.exp2(st - m_new[0:1]).astype(BF16)
            slab, col = divmod(kb * sub, tq)
            acc_all[(h, qb)] = (jnp.exp2(m_prev - m_new)[0:1] * acc_all[(h, qb)]
                                + jnp.dot(vt_ref[slab, h * V_PAD:(h + 1) * V_PAD, col:col + sub], pt,
                                          preferred_element_type=F32))

        nblk = len(blocks)
        pending = {i: scores(*blocks[i]) for i in range(min(FLASH_DEPTH, nblk))}
        for i in range(nblk):
            if i + FLASH_DEPTH < nblk:
                pending[i + FLASH_DEPTH] = scores(*blocks[i + FLASH_DEPTH])
            absorb(*blocks[i], pending.pop(i))
        for h, qb in state:
            m_sc[h, :, qb * sub:(qb + 1) * sub] = m_all[(h, qb)]
            acc_sc[h, :, qb * sub:(qb + 1) * sub] = acc_all[(h, qb)]

    def finish():
        outs = []
        for h in range(MLA_HEADS):
            acc = acc_sc[h]
            outs.append(acc[:V_HEAD] / acc[V_HEAD:V_HEAD + 1])
        o_ref[...] = jnp.concatenate(outs, axis=0).T.astype(BF16)

    @pl.when(kind == 0)
    def _below():
        step(None)

    for slab in range(kslabs):
        @pl.when(kind == slab + 1)
        def _diag(slab=slab):
            step(slab * tq)
            finish()


def _flash_call(k, qt, vt, batch, seq, tq, tk):
    t, hw = k.shape
    vw = vt.shape[1]
    nq = seq // tq
    nk = seq // tk
    kslabs = tk // tq
    steps = [(a, b, 0 if b < a // kslabs else a % kslabs + 1) for a in range(nq) for b in range(a // kslabs + 1)]
    qi_tab, ki_tab, kind_tab = (jnp.asarray([s[i] for s in steps], jnp.int32) for i in range(3))
    grid_spec = pltpu.PrefetchScalarGridSpec(
        num_scalar_prefetch=3,
        grid=(batch, len(steps)),
        in_specs=[
            pl.BlockSpec((tk, hw), lambda b, p, qt_, kt_, kd_: (b * nk + kt_[p], 0)),
            pl.BlockSpec((1, hw, tq), lambda b, p, qt_, kt_, kd_: (b * nq + qt_[p], 0, 0)),
            pl.BlockSpec((kslabs, vw, tq), lambda b, p, qt_, kt_, kd_: (b * nk + kt_[p], 0, 0)),
        ],
        out_specs=pl.BlockSpec((tq, MLA_HEADS * V_HEAD), lambda b, p, qt_, kt_, kd_: (b * nq + qt_[p], 0)),
        scratch_shapes=[
            pltpu.VMEM((MLA_HEADS, SUBLANES, tq), F32),
            pltpu.VMEM((MLA_HEADS, V_PAD, tq), F32),
        ],
    )
    return pl.pallas_call(
        _flash_kernel,
        grid_spec=grid_spec,
        out_shape=jax.ShapeDtypeStruct((t, MLA_HEADS * V_HEAD), BF16),
        compiler_params=_params(("arbitrary", "arbitrary")),
        name="flash",
    )(qi_tab, ki_tab, kind_tab, k, qt, vt)


def _s5_prep_kernel(lr_ref, li_ref, ldt_ref, bt_ref, btsw_ref, cw1_ref, cw2_ref, dsk_ref,
                    dt_ref, wst_ref, wot_ref, ws_sc):
    for q in range(S5_PREP_GROUPS):
        _s5_prep_group(lr_ref.at[q], li_ref.at[q], ldt_ref.at[q], bt_ref.at[q], btsw_ref.at[q], cw1_ref.at[q],
                       cw2_ref.at[q], dsk_ref.at[q], dt_ref.at[q], wst_ref.at[q], wot_ref.at[q], ws_sc)


def _s5_prep_group(lr_ref, li_ref, ldt_ref, bt_ref, btsw_ref, cw1_ref, cw2_ref, dsk_ref,
                   dt_ref, wst_ref, wot_ref, ws_sc):
    pw = 2 * S5_STATE
    lr = lr_ref[...]
    li = li_ref[...]
    dt = jnp.exp(ldt_ref[...])
    lane = lax.broadcasted_iota(jnp.int32, (1, pw), 1)
    sgn = jnp.where(lane < S5_STATE, -1.0, 1.0)
    nk = CHUNK + SUBLANES
    kk = lax.broadcasted_iota(jnp.int32, (nk, pw), 0).astype(F32)
    mag = jnp.exp(kk * (lr * dt))
    ang = kk * (li * dt)
    cs = jnp.cos(ang - jnp.where(lane < S5_STATE, 0.0, 0.5 * math.pi))
    sw = pltpu.roll(cs, S5_STATE, 1)
    ar = mag * jnp.where(lane < S5_STATE, cs, sw)
    ai = mag * jnp.where(lane < S5_STATE, sw, cs)
    ais = ai * sgn
    den = lr * lr + li * li
    a1r = ar[1:2]
    a1i = ai[1:2]
    f_re = ((a1r - 1.0) * lr + a1i * li) / den
    f_im = (a1i * lr - (a1r - 1.0) * li) / den
    bt = bt_ref[...]
    btsw = btsw_ref[...]
    bb = f_re * bt + (f_im * sgn) * btsw
    bbsw = f_re * btsw - (f_im * sgn) * bt
    cw1 = cw1_ref[...]
    cw2 = cw2_ref[...]
    for s in range(CHUNK):
        k = CHUNK - 1 - s
        ws_sc[s * S5_GROUP:(s + 1) * S5_GROUP, :] = ar[k:k + 1] * bb + ais[k:k + 1] * bbsw
        wot_ref[s * S5_GROUP:(s + 1) * S5_GROUP, :] = (ar[s + 1:s + 2] * cw1 + ai[s + 1:s + 2] * cw2).astype(BF16)
    ws = ws_sc[...]
    wst_ref[...] = ws.T.astype(BF16)
    strip = lax.dot_general(cw1, ws, NT_DIMS, preferred_element_type=F32, precision=HI)
    row = lax.broadcasted_iota(jnp.int32, (S5_GROUP, S5_FLAT), 0)
    col = lax.broadcasted_iota(jnp.int32, (S5_GROUP, S5_FLAT), 1)
    dsk = jnp.concatenate([dsk_ref[...]] * (S5_FLAT // LANES), axis=1)
    strip = strip + jnp.where(col == row + (S5_FLAT - S5_GROUP), dsk, 0.0)
    strip = jnp.concatenate([strip, jnp.zeros((S5_GROUP, S5_BLK), F32)], axis=1)
    for d in range(S5_NSUB):
        for tt in range(S5_SUB):
            i0 = CHUNK - 1 - S5_SUB * d - tt
            dt_ref[d, tt * S5_GROUP:(tt + 1) * S5_GROUP, :] = (
                strip[:, S5_GROUP * i0:S5_GROUP * i0 + S5_BLK].astype(BF16))


def _s5_prep_call(lr2, li2, ldt2, bt, btsw, cw1, cw2, dsk):
    n = lr2.shape[0]
    pw = 2 * S5_STATE
    ng = S5_PREP_GROUPS
    vec = pl.BlockSpec((ng, 1, pw), lambda i: (i, 0, 0))
    mat = pl.BlockSpec((ng, S5_GROUP, pw), lambda i: (i, 0, 0))
    return pl.pallas_call(
        _s5_prep_kernel,
        grid=(n // ng,),
        in_specs=[vec, vec, vec, mat, mat, mat, mat, mat],
        out_specs=[
            pl.BlockSpec((ng, S5_NSUB, S5_BLK, S5_BLK), lambda i: (i, 0, 0, 0)),
            pl.BlockSpec((ng, pw, S5_FLAT), lambda i: (i, 0, 0)),
            pl.BlockSpec((ng, S5_FLAT, pw), lambda i: (i, 0, 0)),
        ],
        out_shape=[
            jax.ShapeDtypeStruct((n, S5_NSUB, S5_BLK, S5_BLK), BF16),
            jax.ShapeDtypeStruct((n, pw, S5_FLAT), BF16),
            jax.ShapeDtypeStruct((n, S5_FLAT, pw), BF16),
        ],
        scratch_shapes=[pltpu.VMEM((S5_FLAT, pw), F32)],
        compiler_params=_params(("arbitrary",)),
        name="s5_prep",
    )(lr2, li2, ldt2, bt, btsw, cw1, cw2, dsk)


def _s5_kernel(u_ref, dt_ref, wst_ref, wot_ref, pa_ref, pb_ref, y_ref, ut_sc, yt_sc, *, nch):
    gg = pl.program_id(1)
    ncol = ut_sc.shape[2]
    pw = 2 * S5_STATE

    nhalf = S5_REGROUP_SPLIT
    hcol = ncol // nhalf

    @pl.when(gg == 0)
    def _regroup():
        for hb in range(nhalf):
            x = u_ref[hb * hcol * CHUNK:(hb + 1) * hcol * CHUNK, :].reshape(hcol, CHUNK, LANES)
            xs = jnp.swapaxes(x, 0, 1)
            for s in range(CHUNK):
                ut_sc[s, :, hb * hcol:(hb + 1) * hcol] = xs[s].T.astype(BF16)

    row0 = pl.multiple_of(gg * S5_GROUP, S5_GROUP)
    u = ut_sc[:, pl.ds(row0, S5_GROUP), :].reshape(S5_FLAT, ncol)
    h = jnp.dot(wst_ref[0], u, preferred_element_type=F32)
    intra = []
    for jj in range(S5_NSUB):
        acc = None
        for ii in range(jj + 1):
            term = jnp.dot(dt_ref[0, jj - ii], u[ii * S5_BLK:(ii + 1) * S5_BLK, :], preferred_element_type=F32)
            acc = term if acc is None else acc + term
        intra.append(acc)
    n_idx = lax.broadcasted_iota(jnp.int32, (pw, ncol), 1) % nch
    pa = pa_ref[0]
    pb = pb_ref[0]
    step = 1
    j = 0
    while step < nch:
        sh = jnp.where(n_idx >= step, pltpu.roll(h, step, 1), 0.0)
        h = h + pa[:, j:j + 1] * sh + pb[:, j:j + 1] * pltpu.roll(sh, S5_STATE, 0)
        step *= 2
        j += 1
    hprev = jnp.where(n_idx >= 1, pltpu.roll(h, 1, 1), 0.0).astype(BF16)
    for jj in range(S5_NSUB):
        acc = intra[jj] + jnp.dot(wot_ref[0, jj * S5_BLK:(jj + 1) * S5_BLK, :], hprev, preferred_element_type=F32)
        yt_sc[jj * S5_SUB:(jj + 1) * S5_SUB, pl.ds(row0, S5_GROUP), :] = acc.reshape(S5_SUB, S5_GROUP, ncol)

    @pl.when(gg == S5_GPB - 1)
    def _emit():
        for hb in range(nhalf):
            ys = jnp.stack([yt_sc[tt, :, hb * hcol:(hb + 1) * hcol].T for tt in range(CHUNK)], axis=0)
            y_ref[hb * hcol * CHUNK:(hb + 1) * hcol * CHUNK, :] = jnp.swapaxes(ys, 0, 1).reshape(hcol * CHUNK, LANES)


def _s5_call(u5, dtb, wst, wot, pa, pb, layer, nch):
    t, width = u5.shape
    ncol = t // CHUNK
    nblk = width // LANES
    base = layer * S5_GROUPS
    grp = lambda shape: pl.BlockSpec((1,) + shape, lambda j, g: (base + j * S5_GPB + g,) + (0,) * len(shape))
    return pl.pallas_call(
        functools.partial(_s5_kernel, nch=nch),
        grid=(nblk, S5_GPB),
        in_specs=[
            pl.BlockSpec((t, LANES), lambda j, g: (0, j)),
            grp(dtb.shape[1:]), grp(wst.shape[1:]), grp(wot.shape[1:]), grp(pa.shape[1:]), grp(pb.shape[1:]),
        ],
        out_specs=pl.BlockSpec((t, LANES), lambda j, g: (0, j)),
        out_shape=jax.ShapeDtypeStruct((t, width), F32),
        scratch_shapes=[pltpu.VMEM((CHUNK, LANES, ncol), BF16), pltpu.VMEM((CHUNK, LANES, ncol), F32)],
        compiler_params=_params(("arbitrary", "arbitrary")),
        name="s5",
    )(u5, dtb, wst, wot, pa, pb)


def _merge_kernel(h_ref, yssm_ref, ymla_ref, ysgu_ref, x_ref, ada_ref,
                  wg_ref, bg_ref, wglu_ref, bglu_ref, wb_ref, wout_ref, lng_ref, lnb_ref,
                  o_ref, *, alpha):
    tm, d = x_ref.shape
    sub = MERGE_SUB
    g1 = ada_ref[0, 0][2:3]

    def merge(r):
        rows = slice(r * sub, (r + 1) * sub)
        h = h_ref[rows, :]
        z = _gelu(yssm_ref[rows, :])
        y5 = (z * _sigmoid(jnp.dot(z.astype(BF16), wglu_ref[0], preferred_element_type=F32)
                           + bglu_ref[0])).astype(BF16)
        branches = (y5, ymla_ref[rows, :], ysgu_ref[rows, :])
        merged = None
        for i in range(N_BRANCH):
            cols = slice(i * d, (i + 1) * d)
            gate = _sigmoid(jnp.dot(h, wg_ref[0, :, cols], preferred_element_type=F32) + bg_ref[0, :, cols])
            term = gate * jnp.dot(branches[i], wb_ref[0, i], preferred_element_type=F32)
            merged = term if merged is None else merged + term
        return merged.astype(BF16)

    def project(r, merged):
        rows = slice(r * sub, (r + 1) * sub)
        y = jnp.dot(merged, wout_ref[0], preferred_element_type=F32)
        o_ref[rows, :] = _layer_norm(alpha * x_ref[rows, :] + (1.0 + g1) * y, lng_ref[0], lnb_ref[0])

    nsub = tm // sub
    merged = merge(0)
    for r in range(nsub):
        nxt = merge(r + 1) if r + 1 < nsub else None
        project(r, merged)
        merged = nxt


def _merge_call(h, yssm, ymla, ysgu, x2, ada_all, layer, wg, bg, wglu, bglu, wb, wout, lng, lnb, seq, tm, alpha):
    t, d = x2.shape
    tpb = seq // tm
    tok = lambda n: pl.BlockSpec((tm, n), lambda i: (i, 0))
    consts = [wg, bg, wglu, bglu, wb, wout, lng, lnb]
    return pl.pallas_call(
        functools.partial(_merge_kernel, alpha=alpha),
        grid=(t // tm,),
        in_specs=[tok(d), tok(S5_WIDTH), tok(BRANCH_WIDTH), tok(BRANCH_WIDTH), tok(d),
                  pl.BlockSpec((1, 1, 6, d), lambda i: (layer, i // tpb, 0, 0))]
                 + [_layer_spec(a, layer) for a in consts],
        out_specs=tok(d),
        out_shape=jax.ShapeDtypeStruct((t, d), F32),
        compiler_params=_params(("arbitrary",)),
        name="merge",
    )(h, yssm, ymla, ysgu, x2, ada_all, *consts)


def _ffn_kernel(x_ref, ada_ref, wa_ref, wb_ref, wo_ref, lng_ref, lnb_ref, o_ref, *, alpha):
    tm = x_ref.shape[0]
    sub = FFN_SUB
    ada = ada_ref[0, 0]

    def hidden(r):
        rows = slice(r * sub, (r + 1) * sub)
        h = (x_ref[rows, :] * (1.0 + ada[4:5]) + ada[3:4]).astype(BF16)
        a = jnp.dot(h, wa_ref[0], preferred_element_type=F32)
        b = jnp.dot(h, wb_ref[0], preferred_element_type=F32)
        return (a * _sigmoid(a) * b).astype(BF16)

    def project(r, act):
        rows = slice(r * sub, (r + 1) * sub)
        f = jnp.dot(act, wo_ref[0], preferred_element_type=F32)
        o_ref[rows, :] = _layer_norm(alpha * x_ref[rows, :] + (1.0 + ada[5:6]) * f, lng_ref[0], lnb_ref[0])

    nsub = tm // sub
    act = hidden(0)
    for r in range(nsub):
        nxt = hidden(r + 1) if r + 1 < nsub else None
        project(r, act)
        act = nxt


def _ffn_call(x2, ada_all, layer, w_in, w_out, lng, lnb, seq, tm, alpha):
    t, d = x2.shape
    ff = w_out.shape[1]
    tpb = seq // tm
    tok = pl.BlockSpec((tm, d), lambda i: (i, 0))
    return pl.pallas_call(
        functools.partial(_ffn_kernel, alpha=alpha),
        grid=(t // tm,),
        in_specs=[tok, pl.BlockSpec((1, 1, 6, d), lambda i: (layer, i // tpb, 0, 0)),
                  _layer_spec(w_in, layer, (ff, 0)), _layer_spec(w_in, layer, (ff, 1)), _layer_spec(w_out, layer),
                  _layer_spec(lng, layer), _layer_spec(lnb, layer)],
        out_specs=tok,
        out_shape=jax.ShapeDtypeStruct((t, d), F32),
        compiler_params=_params(("arbitrary",)),
        name="ffn",
    )(x2, ada_all, w_in, w_in, w_out, lng, lnb)


def _s5_prep_inputs(lam_re, lam_im, log_dt, b_re, b_im, c_re, c_im, dskip):
    depth, g, p = lam_re.shape
    n = depth * g
    dup = lambda a: jnp.concatenate([a, a], axis=-1).reshape(n, 1, 2 * p)
    lr2 = dup(lam_re)
    li2 = dup(lam_im)
    ldt2 = jnp.broadcast_to(log_dt[..., None], (depth, g, 2 * p)).reshape(n, 1, 2 * p)
    btr = b_re.transpose(0, 1, 3, 2).reshape(n, S5_GROUP, p)
    bti = b_im.transpose(0, 1, 3, 2).reshape(n, S5_GROUP, p)
    bt = jnp.concatenate([btr, bti], axis=-1)
    btsw = jnp.concatenate([bti, btr], axis=-1)
    cr = c_re.reshape(n, S5_GROUP, p)
    ci = c_im.reshape(n, S5_GROUP, p)
    cw1 = jnp.concatenate([cr, -ci], axis=-1)
    cw2 = jnp.concatenate([-ci, -cr], axis=-1)
    dsk = jnp.broadcast_to(dskip.reshape(n, S5_GROUP, 1), (n, S5_GROUP, LANES))
    return lr2, li2, ldt2, bt, btsw, cw1, cw2, dsk


def _s5_scan_multipliers(lam_re, lam_im, log_dt, nch):
    depth, g, p = lam_re.shape
    dt = jnp.exp(log_dt)[..., None, None]
    nsteps = max(1, (nch - 1).bit_length())
    assert nsteps <= S5_SCAN_COLS
    exps = np.zeros(S5_SCAN_COLS)
    exps[:nsteps] = CHUNK * (2.0 ** np.arange(nsteps))
    k = jnp.asarray(exps, F32)
    mag = jnp.exp(lam_re[..., None] * dt * k)
    ang = lam_im[..., None] * dt * k
    sr = (mag * jnp.cos(ang)).reshape(depth * g, p, S5_SCAN_COLS)
    si = (mag * jnp.sin(ang)).reshape(depth * g, p, S5_SCAN_COLS)
    return jnp.concatenate([sr, sr], axis=1), jnp.concatenate([-si, si], axis=1)


def _rope_tables(seq, scale):
    inv_freq = 1.0 / (ROPE_THETA ** (jnp.arange(0, QK_ROPE, 2, dtype=F32) / QK_ROPE))
    ang = jnp.arange(seq, dtype=F32)[:, None] * inv_freq[None, :]
    cos = jnp.cos(ang)
    sin = jnp.sin(ang)
    zk = jnp.zeros((seq, LANES - QK_ROPE), F32)
    cosk = jnp.concatenate([cos, cos, zk], axis=1)
    sink = jnp.concatenate([-sin, sin, zk], axis=1)
    zq = jnp.zeros((seq, HEAD_PAD - QK_NOPE - QK_ROPE), F32)
    aq = jnp.concatenate([jnp.full((seq, QK_NOPE), scale, F32), scale * cos, scale * cos, zq], axis=1)
    sq = jnp.concatenate([jnp.zeros((seq, QK_NOPE), F32), -scale * sin, scale * sin, zq], axis=1)
    return cosk, sink, aq.T, sq.T


def _swap_halves(w):
    half = w.shape[-1] // 2
    return jnp.concatenate([w[..., half:], w[..., :half]], axis=-1)


def kernel(x, c, w_ada, b_ada, w_in, b_in, s5_lambda_re, s5_lambda_im, s5_log_dt, s5_b_re, s5_b_im, s5_c_re, s5_c_im, s5_d, s5_w_glu, s5_b_glu, mla_q_norm, mla_w_q_up, mla_kv_norm, mla_w_kv_up, sgu_ln_g, sgu_ln_b, sgu_w_s, sgu_b_s, w_branch, w_out, ln1_g, ln1_b, ffn_w_in, ffn_w_out, ln2_g, ln2_b):
    batch, seq, d = x.shape
    depth = w_ada.shape[0]
    t = batch * seq
    alpha = float((2 * depth) ** 0.25)
    tm = 512
    tq = 512
    tm_wide = 1024
    assert seq % tm_wide == 0
    nch = seq // CHUNK
    assert seq % FLASH_TK == 0 and FLASH_TK % tm == 0 and tm == tq and tm % SGU_CHUNK == 0
    assert (batch * nch) % LANES == 0

    ada_all = _ada_call(c, w_ada, b_ada).reshape(depth, batch, 6, d)

    scale = LOG2E * (QK_NOPE + QK_ROPE) ** -0.5
    cosk, sink, aqt, sqt = _rope_tables(seq, scale)

    offs = np.cumsum([0, S5_WIDTH, Q_LORA, KV_LORA, QK_ROPE, SGU_WIDTH, SGU_WIDTH])
    o_u5, o_cq, o_ckv, o_kpe, o_usgu, o_vsgu, o_gate = [int(o) for o in offs]
    row = lambda a: a[:, None, :]

    kpe_b = b_in[:, o_kpe:o_kpe + QK_ROPE]
    w_a, w_g = _win_prep_call(w_in, o_kpe, o_usgu, o_gate)
    b_a = row(jnp.concatenate([b_in[:, o_u5:o_kpe], b_in[:, o_usgu:o_gate], kpe_b, _swap_halves(kpe_b),
                               jnp.zeros((depth, LANES - 2 * QK_ROPE), F32)], axis=1))
    b_g = row(b_in[:, o_gate:])
    pos = np.arange(SGU_CHUNK) // CHUNK
    sgu_mask = jnp.asarray(pos[None, :] <= pos[:, None])
    ws_m = jnp.where(sgu_mask[None, None], sgu_w_s, 0.0).astype(BF16)
    bs_full = jnp.repeat(sgu_b_s.transpose(0, 2, 1), SGU_WIDTH // SGU_GROUPS, axis=2)

    hw = MLA_HEADS * HEAD_PAD
    vw = MLA_HEADS * V_PAD
    wq4 = mla_w_q_up.reshape(depth, Q_LORA, MLA_HEADS, QK_NOPE + QK_ROPE)
    wqt = (jnp.concatenate([wq4, _swap_halves(wq4[..., QK_NOPE:])], axis=3)
           .reshape(depth, Q_LORA, hw).transpose(0, 2, 1).astype(BF16))
    wkv4 = mla_w_kv_up.reshape(depth, KV_LORA, MLA_HEADS, QK_NOPE + V_HEAD)
    wk = jnp.concatenate([wkv4[..., :QK_NOPE], jnp.zeros((depth, KV_LORA, MLA_HEADS, HEAD_PAD - QK_NOPE), F32)],
                         axis=3).reshape(depth, KV_LORA, hw).astype(BF16)
    wvt = (jnp.concatenate([wkv4[..., QK_NOPE:], jnp.zeros((depth, KV_LORA, MLA_HEADS, V_PAD - V_HEAD), F32)],
                           axis=3).reshape(depth, KV_LORA, vw).transpose(0, 2, 1).astype(BF16))
    place = np.zeros((LANES, hw), np.float32)
    for hh in range(MLA_HEADS):
        place[np.arange(QK_ROPE), hh * HEAD_PAD + QK_NOPE + np.arange(QK_ROPE)] = 1.0
    pk = jnp.asarray(place, BF16)
    ones_col = np.zeros((vw, 1), np.float32)
    ones_col[np.arange(MLA_HEADS) * V_PAD + V_HEAD, 0] = 1.0
    ones_col = jnp.asarray(ones_col)

    wglu = s5_w_glu.astype(BF16)
    wbr = w_branch.astype(BF16)
    wout = w_out.astype(BF16)
    ffn_in = ffn_w_in.astype(BF16)
    ffn_out = ffn_w_out.astype(BF16)

    dtb, wst, wot = _s5_prep_call(*_s5_prep_inputs(s5_lambda_re, s5_lambda_im, s5_log_dt, s5_b_re, s5_b_im,
                                                   s5_c_re, s5_c_im, s5_d))
    pa, pb = _s5_scan_multipliers(s5_lambda_re, s5_lambda_im, s5_log_dt, nch)

    x2 = x.reshape(t, d)
    for l in range(depth):
        h, u5, ysgu, kmat, qt, vt = _inproj_call(
            x2, ada_all, l, w_a, b_a, row(mla_q_norm), row(mla_kv_norm), row(sgu_ln_g), row(sgu_ln_b),
            ws_m, bs_full, wqt, wk, wvt, pk, ones_col, cosk, sink, aqt, sqt, seq, tm)
        ymla = _flash_call(kmat, qt, vt, batch, seq, tq, FLASH_TK)
        yssm = _s5_call(u5, dtb, wst, wot, pa, pb, l, nch)
        x2 = _merge_call(h, yssm, ymla, ysgu, x2, ada_all, l, w_g, b_g, wglu, row(s5_b_glu), wbr, wout,
                         row(ln1_g), row(ln1_b), seq, tm_wide, alpha)
        x2 = _ffn_call(x2, ada_all, l, ffn_in, ffn_out, row(ln2_g), row(ln2_b), seq, tm_wide, alpha)
    return x2.reshape(batch, seq, d)
```

```python
import functools
import math

import numpy as np
import jax
import jax.numpy as jnp
from jax import lax
from jax.experimental import pallas as pl
from jax.experimental.pallas import tpu as pltpu

F32 = jnp.float32
BF16 = jnp.bfloat16

CHUNK = 64
S5_WIDTH = 512
S5_GROUP = 16
S5_GROUPS = S5_WIDTH // S5_GROUP
S5_STATE = 64
MLA_HEADS = 8
QK_NOPE = 64
QK_ROPE = 32
V_HEAD = 64
Q_LORA = 384
KV_LORA = 256
ROPE_THETA = 10000.0
SGU_WIDTH = 512
SGU_GROUPS = 4
SGU_CHUNK = 128
N_BRANCH = 3
BRANCH_WIDTH = 512
LN_EPS = 1e-5
RMS_EPS = 1e-6
NEG_INF = -1e30

LANES = 128
SUBLANES = 8
BF16_ROWS = 16
HEAD_PAD = 128
V_PAD = V_HEAD + BF16_ROWS
S5_SUB = 16
S5_NSUB = CHUNK // S5_SUB
S5_BLK = S5_SUB * S5_GROUP
S5_FLAT = CHUNK * S5_GROUP
S5_GPB = LANES // S5_GROUP
S5_SCAN_COLS = 8
S5_STEP_GROUPS = 2
S5_REGROUP_SPLIT = 2
INPROJ_SUB = 256
MERGE_SUB = 256
FFN_SUB = 256
S5_PREP_GROUPS = 8
FLASH_SUB = 256
FLASH_TK = 1024
FLASH_DEPTH = 6

OFF_U5 = 0
OFF_CQ = OFF_U5 + S5_WIDTH
OFF_CKV = OFF_CQ + Q_LORA
OFF_USGU = OFF_CKV + KV_LORA
OFF_VSGU = OFF_USGU + SGU_WIDTH
OFF_KPE = OFF_VSGU + SGU_WIDTH
NA = OFF_KPE + LANES

VMEM_LIMIT = 56 * 1024 * 1024

HI = lax.Precision.HIGHEST
NT_DIMS = (((1,), (1,)), ((), ()))
LOG2E = 1.4426950408889634


def _gelu(x):
    return 0.5 * x * (1.0 + jnp.tanh(0.7978845608028654 * (x + 0.044715 * (x * x * x))))


def _sigmoid(x):
    return 0.5 * jnp.tanh(0.5 * x) + 0.5


def _layer_norm(r, g, b):
    mu = jnp.mean(r, axis=-1, keepdims=True)
    rc = r - mu
    var = jnp.mean(rc * rc, axis=-1, keepdims=True)
    return rc * lax.rsqrt(var + LN_EPS) * g + b


def _const_spec(shape):
    nd = len(shape)
    return pl.BlockSpec(shape, lambda *_: (0,) * nd, pipeline_mode=pl.Buffered(1))


def _layer_spec(arr, layer, col_block=None):
    shape = (1,) + arr.shape[1:]
    idx = (layer,) + (0,) * (arr.ndim - 1)
    if col_block is not None:
        width, j = col_block
        shape = shape[:-1] + (width,)
        idx = idx[:-1] + (j,)
    return pl.BlockSpec(shape, lambda *_: idx, pipeline_mode=pl.Buffered(1))


def _params(sem):
    return pltpu.CompilerParams(dimension_semantics=sem, vmem_limit_bytes=VMEM_LIMIT)


def _ada_kernel(cb_ref, w_ref, b_ref, o_ref, ca_sc):
    nb, d, _ = cb_ref.shape
    tn = w_ref.shape[2]

    @pl.when((pl.program_id(0) == 0) & (pl.program_id(1) == 0))
    def _silu():
        cb = cb_ref[...]
        ca_sc[...] = cb * _sigmoid(cb)

    def body(g, accs):
        r0 = pl.multiple_of(g * SUBLANES, SUBLANES)
        w = w_ref[0, pl.ds(r0, SUBLANES), :]
        out = []
        for b in range(nb):
            scale = jnp.concatenate([ca_sc[b, pl.ds(r0, SUBLANES), :]] * (tn // LANES), axis=1)
            out.append(accs[b] + w * scale)
        return tuple(out)

    accs = lax.fori_loop(0, d // SUBLANES, body, tuple(jnp.zeros((SUBLANES, tn), F32) for _ in range(nb)),
                         unroll=8)
    for b in range(nb):
        o_ref[0, b:b + 1, :] = jnp.sum(accs[b], axis=0, keepdims=True) + b_ref[0]


def _ada_call(c, w_ada, b_ada):
    depth, d, n6 = w_ada.shape
    batch = c.shape[0]
    tn = n6 // 8
    cb = jnp.broadcast_to(c[:, :, None], (batch, d, LANES))
    return pl.pallas_call(
        _ada_kernel,
        grid=(depth, n6 // tn),
        in_specs=[
            _const_spec(cb.shape),
            pl.BlockSpec((1, d, tn), lambda l, j: (l, 0, j)),
            pl.BlockSpec((1, 1, tn), lambda l, j: (l, 0, j)),
        ],
        out_specs=pl.BlockSpec((1, batch, tn), lambda l, j: (l, 0, j)),
        out_shape=jax.ShapeDtypeStruct((depth, batch, n6), F32),
        scratch_shapes=[pltpu.VMEM(cb.shape, F32)],
        compiler_params=_params(("arbitrary", "arbitrary")),
        name="ada",
    )(cb, w_ada, b_ada.reshape(depth, 1, n6))


def _win_prep_kernel(w_ref, wa_ref, wg_ref, *, o_kpe, o_usgu, o_gate):
    w = w_ref[0]
    half = QK_ROPE // 2
    kpe = w[:, o_kpe:o_kpe + QK_ROPE]
    last = jnp.concatenate([kpe, kpe[:, half:], kpe[:, :half],
                            jnp.zeros((w.shape[0], LANES - 2 * QK_ROPE), F32)], axis=1)
    wa_ref[0, :, 0:o_kpe] = w[:, 0:o_kpe].astype(BF16)
    wa_ref[0, :, o_kpe:OFF_KPE] = w[:, o_usgu:o_gate].astype(BF16)
    wa_ref[0, :, OFF_KPE:NA] = last.astype(BF16)
    wg_ref[0] = w[:, o_gate:].astype(BF16)


def _win_prep_call(w_in, o_kpe, o_usgu, o_gate):
    depth, d, n_in = w_in.shape
    tr = 256
    assert o_kpe == OFF_USGU and o_kpe + (o_gate - o_usgu) == OFF_KPE
    return pl.pallas_call(
        functools.partial(_win_prep_kernel, o_kpe=o_kpe, o_usgu=o_usgu, o_gate=o_gate),
        grid=(depth, d // tr),
        in_specs=[pl.BlockSpec((1, tr, n_in), lambda l, i: (l, i, 0))],
        out_specs=[pl.BlockSpec((1, tr, NA), lambda l, i: (l, i, 0)),
                   pl.BlockSpec((1, tr, n_in - o_gate), lambda l, i: (l, i, 0))],
        out_shape=[jax.ShapeDtypeStruct((depth, d, NA), BF16),
                   jax.ShapeDtypeStruct((depth, d, n_in - o_gate), BF16)],
        compiler_params=_params(("arbitrary", "arbitrary")),
        name="win_prep",
    )(w_in)


def _inproj_kernel(x_ref, ada_ref, w_ref, b_ref, qn_ref, kvn_ref, lng_ref, lnb_ref, ws_ref, bs_ref,
                   wqt_ref, wk_ref, wvt_ref, pk_ref, ones_ref, cosk_ref, sink_ref, aqt_ref, sqt_ref,
                   h_ref, u5_ref, ysgu_ref, k_ref, qt_ref, vt_ref):
    tm = x_ref.shape[0]
    ada = ada_ref[0, 0]
    sub = INPROJ_SUB
    gw = SGU_WIDTH // SGU_GROUPS

    def project(r):
        rows = slice(r * sub, (r + 1) * sub)
        h = (x_ref[rows, :] * (1.0 + ada[1:2]) + ada[0:1]).astype(BF16)
        h_ref[rows, :] = h
        return jnp.dot(h, w_ref[0], preferred_element_type=F32) + b_ref[0]

    def epilogue(r, acc):
        rows = slice(r * sub, (r + 1) * sub)
        u5_ref[rows, :] = acc[:, OFF_U5:OFF_U5 + S5_WIDTH]

        cq = acc[:, OFF_CQ:OFF_CQ + Q_LORA]
        cqn = (cq * lax.rsqrt(jnp.mean(cq * cq, axis=-1, keepdims=True) + RMS_EPS) * qn_ref[0]).astype(BF16)
        ckv = acc[:, OFF_CKV:OFF_CKV + KV_LORA]
        ckvn = (ckv * lax.rsqrt(jnp.mean(ckv * ckv, axis=-1, keepdims=True) + RMS_EPS) * kvn_ref[0]).astype(BF16)

        kb = acc[:, OFF_KPE:OFF_KPE + LANES]
        krot = (kb * cosk_ref[rows, :] + pltpu.roll(kb, LANES - QK_ROPE, 1) * sink_ref[rows, :]).astype(BF16)

        k_ref[rows, :] = (jnp.dot(ckvn, wk_ref[0], preferred_element_type=F32)
                          + jnp.dot(krot, pk_ref[...], preferred_element_type=F32)).astype(BF16)
        qt = lax.dot_general(wqt_ref[0], cqn, NT_DIMS, preferred_element_type=F32)
        aqt = aqt_ref[:, rows]
        sqt = sqt_ref[:, rows]
        for h in range(MLA_HEADS):
            hrow = slice(h * HEAD_PAD, (h + 1) * HEAD_PAD)
            qh = qt[hrow, :]
            qt_ref[0, hrow, rows] = (qh * aqt + pltpu.roll(qh, HEAD_PAD - QK_ROPE, 0) * sqt).astype(BF16)
        vt = lax.dot_general(wvt_ref[0], ckvn, NT_DIMS, preferred_element_type=F32) + ones_ref[...]
        vt_ref[0, :, rows] = vt.astype(BF16)

        us = _gelu(acc[:, OFF_USGU:OFF_USGU + SGU_WIDTH])
        vn = _layer_norm(_gelu(acc[:, OFF_VSGU:OFF_VSGU + SGU_WIDTH]), lng_ref[0], lnb_ref[0]).astype(BF16)
        for c in range(sub // SGU_CHUNK):
            crow = slice(c * SGU_CHUNK, (c + 1) * SGU_CHUNK)
            parts = [jnp.dot(ws_ref[0, g], vn[crow, g * gw:(g + 1) * gw], preferred_element_type=F32)
                     for g in range(SGU_GROUPS)]
            mixed = jnp.concatenate(parts, axis=1) + bs_ref[0]
            ysgu_ref[r * sub + c * SGU_CHUNK:r * sub + (c + 1) * SGU_CHUNK, :] = (us[crow, :] * mixed).astype(BF16)

    nsub = tm // sub
    acc = project(0)
    for r in range(nsub):
        nxt = project(r + 1) if r + 1 < nsub else None
        epilogue(r, acc)
        acc = nxt


def _inproj_call(x2, ada_all, layer, w, b, qn, kvn, lng, lnb, ws, bs, wqt, wk, wvt, pk, ones,
                 cosk, sink, aqt, sqt, seq, tm):
    t, d = x2.shape
    tpb = seq // tm
    hw = MLA_HEADS * HEAD_PAD
    vw = MLA_HEADS * V_PAD
    tok = lambda n: pl.BlockSpec((tm, n), lambda i: (i, 0))
    ktab = pl.BlockSpec((tm, LANES), lambda i: (i % tpb, 0))
    qtab = pl.BlockSpec((HEAD_PAD, tm), lambda i: (0, i % tpb))
    consts = [w, b, qn, kvn, lng, lnb, ws, bs, wqt, wk, wvt]
    return pl.pallas_call(
        _inproj_kernel,
        grid=(t // tm,),
        in_specs=[tok(d), pl.BlockSpec((1, 1, 6, d), lambda i: (layer, i // tpb, 0, 0))]
                 + [_layer_spec(a, layer) for a in consts]
                 + [_const_spec(pk.shape), _const_spec(ones.shape), ktab, ktab, qtab, qtab],
        out_specs=[tok(d), tok(S5_WIDTH), tok(SGU_WIDTH), tok(hw),
                   pl.BlockSpec((1, hw, tm), lambda i: (i, 0, 0)),
                   pl.BlockSpec((1, vw, tm), lambda i: (i, 0, 0))],
        out_shape=[
            jax.ShapeDtypeStruct((t, d), BF16),
            jax.ShapeDtypeStruct((t, S5_WIDTH), F32),
            jax.ShapeDtypeStruct((t, SGU_WIDTH), BF16),
            jax.ShapeDtypeStruct((t, hw), BF16),
            jax.ShapeDtypeStruct((t // tm, hw, tm), BF16),
            jax.ShapeDtypeStruct((t // tm, vw, tm), BF16),
        ],
        compiler_params=_params(("arbitrary",)),
        name="inproj",
    )(x2, ada_all, *consts, pk, ones, cosk, sink, aqt, sqt)


def _flash_kernel(qi_ref, ki_ref, kind_ref, k_ref, qt_ref, vt_ref, o_ref, m_sc, acc_sc):
    p_idx = pl.program_id(1)
    ki = ki_ref[p_idx]
    kind = kind_ref[p_idx]
    tk = k_ref.shape[0]
    tq = qt_ref.shape[2]
    kslabs = tk // tq

    @pl.when(ki == 0)
    def _init():
        m_sc[...] = jnp.full(m_sc.shape, NEG_INF, F32)
        acc_sc[...] = jnp.zeros(acc_sc.shape, F32)

    def step(q_off):
        sub = FLASH_SUB
        nkb, nqb = tk // sub, tq // sub
        state = [(h, qb) for h in range(MLA_HEADS) for qb in range(nqb)]
        m_all = {(h, qb): m_sc[h, :, qb * sub:(qb + 1) * sub] for h, qb in state}
        acc_all = {(h, qb): acc_sc[h, :, qb * sub:(qb + 1) * sub] for h, qb in state}
        blocks = [(h, qb, kb) for kb in range(nkb) for h, qb in state]
        if q_off is not None:
            blocks = [(h, qb, kb) for h, qb, kb in blocks
                      if kb * sub // CHUNK <= (q_off + (qb + 1) * sub - 1) // CHUNK]

        def scores(h, qb, kb):
            cols = slice(h * HEAD_PAD, (h + 1) * HEAD_PAD)
            st = jnp.dot(k_ref[kb * sub:(kb + 1) * sub, cols], qt_ref[0, cols, qb * sub:(qb + 1) * sub],
                         preferred_element_type=F32)
            if q_off is not None and ((kb + 1) * sub - 1) // CHUNK > (q_off + qb * sub) // CHUNK:
                kc = (lax.broadcasted_iota(jnp.int32, (sub, sub), 0) + kb * sub) // CHUNK
                qc = (lax.broadcasted_iota(jnp.int32, (sub, sub), 1) + q_off + qb * sub) // CHUNK
                st = jnp.where(kc <= qc, st, NEG_INF)
            return st

        def absorb(h, qb, kb, st):
            m_prev = m_all[(h, qb)]
            m_new = jnp.maximum(m_prev, jnp.max(st, axis=0, keepdims=True))
            m_all[(h, qb)] = m_new
            pt = jnp.exp2(st - m_new[0:1]).astype(BF16)
            slab, col = divmod(kb * sub, tq)
            acc_all[(h, qb)] = (jnp.exp2(m_prev - m_new)[0:1] * acc_all[(h, qb)]
                                + jnp.dot(vt_ref[slab, h * V_PAD:(h + 1) * V_PAD, col:col + sub], pt,
                                          preferred_element_type=F32))

        nblk = len(blocks)
        pending = {i: scores(*blocks[i]) for i in range(min(FLASH_DEPTH, nblk))}
        for i in range(nblk):
            if i + FLASH_DEPTH < nblk:
                pending[i + FLASH_DEPTH] = scores(*blocks[i + FLASH_DEPTH])
            absorb(*blocks[i], pending.pop(i))
        for h, qb in state:
            m_sc[h, :, qb * sub:(qb + 1) * sub] = m_all[(h, qb)]
            acc_sc[h, :, qb * sub:(qb + 1) * sub] = acc_all[(h, qb)]

    def finish():
        outs = []
        for h in range(MLA_HEADS):
            acc = acc_sc[h]
            outs.append(acc[:V_HEAD] / acc[V_HEAD:V_HEAD + 1])
        o_ref[...] = jnp.concatenate(outs, axis=0).T.astype(BF16)

    @pl.when(kind == 0)
    def _below():
        step(None)

    for slab in range(kslabs):
        @pl.when(kind == slab + 1)
        def _diag(slab=slab):
            step(slab * tq)
            finish()


def _flash_call(k, qt, vt, batch, seq, tq, tk):
    t, hw = k.shape
    vw = vt.shape[1]
    nq = seq // tq
    nk = seq // tk
    kslabs = tk // tq
    steps = [(a, b, 0 if b < a // kslabs else a % kslabs + 1) for a in range(nq) for b in range(a // kslabs + 1)]
    qi_tab, ki_tab, kind_tab = (jnp.asarray([s[i] for s in steps], jnp.int32) for i in range(3))
    grid_spec = pltpu.PrefetchScalarGridSpec(
        num_scalar_prefetch=3,
        grid=(batch, len(steps)),
        in_specs=[
            pl.BlockSpec((tk, hw), lambda b, p, qt_, kt_, kd_: (b * nk + kt_[p], 0)),
            pl.BlockSpec((1, hw, tq), lambda b, p, qt_, kt_, kd_: (b * nq + qt_[p], 0, 0)),
            pl.BlockSpec((kslabs, vw, tq), lambda b, p, qt_, kt_, kd_: (b * nk + kt_[p], 0, 0)),
        ],
        out_specs=pl.BlockSpec((tq, MLA_HEADS * V_HEAD), lambda b, p, qt_, kt_, kd_: (b * nq + qt_[p], 0)),
        scratch_shapes=[
            pltpu.VMEM((MLA_HEADS, SUBLANES, tq), F32),
            pltpu.VMEM((MLA_HEADS, V_PAD, tq), F32),
        ],
    )
    return pl.pallas_call(
        _flash_kernel,
        grid_spec=grid_spec,
        out_shape=jax.ShapeDtypeStruct((t, MLA_HEADS * V_HEAD), BF16),
        compiler_params=_params(("arbitrary", "arbitrary")),
        name="flash",
    )(qi_tab, ki_tab, kind_tab, k, qt, vt)


def _s5_prep_kernel(lr_ref, li_ref, ldt_ref, bt_ref, btsw_ref, cw1_ref, cw2_ref, dsk_ref,
                    dt_ref, wst_ref, wot_ref, ws_sc):
    for q in range(S5_PREP_GROUPS):
        _s5_prep_group(lr_ref.at[q], li_ref.at[q], ldt_ref.at[q], bt_ref.at[q], btsw_ref.at[q], cw1_ref.at[q],
                       cw2_ref.at[q], dsk_ref.at[q], dt_ref.at[q], wst_ref.at[q], wot_ref.at[q], ws_sc)


def _s5_prep_group(lr_ref, li_ref, ldt_ref, bt_ref, btsw_ref, cw1_ref, cw2_ref, dsk_ref,
                   dt_ref, wst_ref, wot_ref, ws_sc):
    pw = 2 * S5_STATE
    lr = lr_ref[...]
    li = li_ref[...]
    dt = jnp.exp(ldt_ref[...])
    lane = lax.broadcasted_iota(jnp.int32, (1, pw), 1)
    sgn = jnp.where(lane < S5_STATE, -1.0, 1.0)
    nk = CHUNK + SUBLANES
    kk = lax.broadcasted_iota(jnp.int32, (nk, pw), 0).astype(F32)
    mag = jnp.exp(kk * (lr * dt))
    ang = kk * (li * dt)
    cs = jnp.cos(ang - jnp.where(lane < S5_STATE, 0.0, 0.5 * math.pi))
    sw = pltpu.roll(cs, S5_STATE, 1)
    ar = mag * jnp.where(lane < S5_STATE, cs, sw)
    ai = mag * jnp.where(lane < S5_STATE, sw, cs)
    ais = ai * sgn
    den = lr * lr + li * li
    a1r = ar[1:2]
    a1i = ai[1:2]
    f_re = ((a1r - 1.0) * lr + a1i * li) / den
    f_im = (a1i * lr - (a1r - 1.0) * li) / den
    bt = bt_ref[...]
    btsw = btsw_ref[...]
    bb = f_re * bt + (f_im * sgn) * btsw
    bbsw = f_re * btsw - (f_im * sgn) * bt
    cw1 = cw1_ref[...]
    cw2 = cw2_ref[...]
    for s in range(CHUNK):
        k = CHUNK - 1 - s
        ws_sc[s * S5_GROUP:(s + 1) * S5_GROUP, :] = ar[k:k + 1] * bb + ais[k:k + 1] * bbsw
        wot_ref[s * S5_GROUP:(s + 1) * S5_GROUP, :] = (ar[s + 1:s + 2] * cw1 + ai[s + 1:s + 2] * cw2).astype(BF16)
    ws = ws_sc[...]
    wst_ref[...] = ws.T.astype(BF16)
    strip = lax.dot_general(cw1, ws, NT_DIMS, preferred_element_type=F32, precision=HI)
    row = lax.broadcasted_iota(jnp.int32, (S5_GROUP, S5_FLAT), 0)
    col = lax.broadcasted_iota(jnp.int32, (S5_GROUP, S5_FLAT), 1)
    dsk = jnp.concatenate([dsk_ref[...]] * (S5_FLAT // LANES), axis=1)
    strip = strip + jnp.where(col == row + (S5_FLAT - S5_GROUP), dsk, 0.0)
    strip = jnp.concatenate([strip, jnp.zeros((S5_GROUP, S5_BLK), F32)], axis=1)
    for d in range(S5_NSUB):
        for tt in range(S5_SUB):
            i0 = CHUNK - 1 - S5_SUB * d - tt
            dt_ref[d, tt * S5_GROUP:(tt + 1) * S5_GROUP, :] = (
                strip[:, S5_GROUP * i0:S5_GROUP * i0 + S5_BLK].astype(BF16))


def _s5_prep_call(lr2, li2, ldt2, bt, btsw, cw1, cw2, dsk):
    n = lr2.shape[0]
    pw = 2 * S5_STATE
    ng = S5_PREP_GROUPS
    vec = pl.BlockSpec((ng, 1, pw), lambda i: (i, 0, 0))
    mat = pl.BlockSpec((ng, S5_GROUP, pw), lambda i: (i, 0, 0))
    return pl.pallas_call(
        _s5_prep_kernel,
        grid=(n // ng,),
        in_specs=[vec, vec, vec, mat, mat, mat, mat, mat],
        out_specs=[
            pl.BlockSpec((ng, S5_NSUB, S5_BLK, S5_BLK), lambda i: (i, 0, 0, 0)),
            pl.BlockSpec((ng, pw, S5_FLAT), lambda i: (i, 0, 0)),
            pl.BlockSpec((ng, S5_FLAT, pw), lambda i: (i, 0, 0)),
        ],
        out_shape=[
            jax.ShapeDtypeStruct((n, S5_NSUB, S5_BLK, S5_BLK), BF16),
            jax.ShapeDtypeStruct((n, pw, S5_FLAT), BF16),
            jax.ShapeDtypeStruct((n, S5_FLAT, pw), BF16),
        ],
        scratch_shapes=[pltpu.VMEM((S5_FLAT, pw), F32)],
        compiler_params=_params(("arbitrary",)),
        name="s5_prep",
    )(lr2, li2, ldt2, bt, btsw, cw1, cw2, dsk)


def _s5_kernel(u_ref, dt_ref, wst_ref, wot_ref, pa_ref, pb_ref, y_ref, ut_sc, yt_sc, *, nch):
    gg = pl.program_id(1)
    ncol = ut_sc.shape[2]
    pw = 2 * S5_STATE

    nhalf = S5_REGROUP_SPLIT
    hcol = ncol // nhalf

    @pl.when(gg == 0)
    def _regroup():
        for hb in range(nhalf):
            x = u_ref[hb * hcol * CHUNK:(hb + 1) * hcol * CHUNK, :].reshape(hcol, CHUNK, LANES)
            xs = jnp.swapaxes(x, 0, 1)
            for s in range(CHUNK):
                ut_sc[s, :, hb * hcol:(hb + 1) * hcol] = xs[s].T.astype(BF16)

    n_idx = lax.broadcasted_iota(jnp.int32, (pw, ncol), 1) % nch

    def within_chunk(q):
        row0 = pl.multiple_of((gg * S5_STEP_GROUPS + q) * S5_GROUP, S5_GROUP)
        u = ut_sc[:, pl.ds(row0, S5_GROUP), :].reshape(S5_FLAT, ncol)
        h = jnp.dot(wst_ref[q], u, preferred_element_type=F32)
        intra = []
        for jj in range(S5_NSUB):
            acc = None
            for ii in range(jj + 1):
                term = jnp.dot(dt_ref[q, jj - ii], u[ii * S5_BLK:(ii + 1) * S5_BLK, :],
                               preferred_element_type=F32)
                acc = term if acc is None else acc + term
            intra.append(acc)
        return row0, h, intra

    def across_chunks(q, row0, h, intra):
        pa = pa_ref[q]
        pb = pb_ref[q]
        step = 1
        j = 0
        while step < nch:
            sh = jnp.where(n_idx >= step, pltpu.roll(h, step, 1), 0.0)
            h = h + pa[:, j:j + 1] * sh + pb[:, j:j + 1] * pltpu.roll(sh, S5_STATE, 0)
            step *= 2
            j += 1
        hprev = jnp.where(n_idx >= 1, pltpu.roll(h, 1, 1), 0.0).astype(BF16)
        for jj in range(S5_NSUB):
            acc = intra[jj] + jnp.dot(wot_ref[q, jj * S5_BLK:(jj + 1) * S5_BLK, :], hprev,
                                      preferred_element_type=F32)
            yt_sc[jj * S5_SUB:(jj + 1) * S5_SUB, pl.ds(row0, S5_GROUP), :] = acc.reshape(S5_SUB, S5_GROUP, ncol)

    parts = [within_chunk(q) for q in range(S5_STEP_GROUPS)]
    for q in range(S5_STEP_GROUPS):
        across_chunks(q, *parts[q])

    @pl.when(gg == S5_GPB // S5_STEP_GROUPS - 1)
    def _emit():
        for hb in range(nhalf):
            ys = jnp.stack([yt_sc[tt, :, hb * hcol:(hb + 1) * hcol].T for tt in range(CHUNK)], axis=0)
            y_ref[hb * hcol * CHUNK:(hb + 1) * hcol * CHUNK, :] = jnp.swapaxes(ys, 0, 1).reshape(hcol * CHUNK, LANES)


def _s5_call(u5, dtb, wst, wot, pa, pb, layer, nch):
    t, width = u5.shape
    ncol = t // CHUNK
    nblk = width // LANES
    ngs = S5_STEP_GROUPS
    base = layer * S5_GROUPS // ngs
    spb = S5_GPB // ngs
    grp = lambda shape: pl.BlockSpec((ngs,) + shape, lambda j, g: (base + j * spb + g,) + (0,) * len(shape))
    return pl.pallas_call(
        functools.partial(_s5_kernel, nch=nch),
        grid=(nblk, spb),
        in_specs=[
            pl.BlockSpec((t, LANES), lambda j, g: (0, j)),
            grp(dtb.shape[1:]), grp(wst.shape[1:]), grp(wot.shape[1:]), grp(pa.shape[1:]), grp(pb.shape[1:]),
        ],
        out_specs=pl.BlockSpec((t, LANES), lambda j, g: (0, j)),
        out_shape=jax.ShapeDtypeStruct((t, width), F32),
        scratch_shapes=[pltpu.VMEM((CHUNK, LANES, ncol), BF16), pltpu.VMEM((CHUNK, LANES, ncol), F32)],
        compiler_params=_params(("arbitrary", "arbitrary")),
        name="s5",
    )(u5, dtb, wst, wot, pa, pb)


def _merge_kernel(h_ref, yssm_ref, ymla_ref, ysgu_ref, x_ref, ada_ref,
                  wg_ref, bg_ref, wglu_ref, bglu_ref, wb_ref, wout_ref, lng_ref, lnb_ref,
                  o_ref, *, alpha):
    tm, d = x_ref.shape
    sub = MERGE_SUB
    g1 = ada_ref[0, 0][2:3]

    def merge(r):
        rows = slice(r * sub, (r + 1) * sub)
        h = h_ref[rows, :]
        z = _gelu(yssm_ref[rows, :])
        y5 = (z * _sigmoid(jnp.dot(z.astype(BF16), wglu_ref[0], preferred_element_type=F32)
                           + bglu_ref[0])).astype(BF16)
        branches = (y5, ymla_ref[rows, :], ysgu_ref[rows, :])
        merged = None
        for i in range(N_BRANCH):
            cols = slice(i * d, (i + 1) * d)
            gate = _sigmoid(jnp.dot(h, wg_ref[0, :, cols], preferred_element_type=F32) + bg_ref[0, :, cols])
            term = gate * jnp.dot(branches[i], wb_ref[0, i], preferred_element_type=F32)
            merged = term if merged is None else merged + term
        return merged.astype(BF16)

    def project(r, merged):
        rows = slice(r * sub, (r + 1) * sub)
        y = jnp.dot(merged, wout_ref[0], preferred_element_type=F32)
        o_ref[rows, :] = _layer_norm(alpha * x_ref[rows, :] + (1.0 + g1) * y, lng_ref[0], lnb_ref[0])

    nsub = tm // sub
    merged = merge(0)
    for r in range(nsub):
        nxt = merge(r + 1) if r + 1 < nsub else None
        project(r, merged)
        merged = nxt


def _merge_call(h, yssm, ymla, ysgu, x2, ada_all, layer, wg, bg, wglu, bglu, wb, wout, lng, lnb, seq, tm, alpha):
    t, d = x2.shape
    tpb = seq // tm
    tok = lambda n: pl.BlockSpec((tm, n), lambda i: (i, 0))
    consts = [wg, bg, wglu, bglu, wb, wout, lng, lnb]
    return pl.pallas_call(
        functools.partial(_merge_kernel, alpha=alpha),
        grid=(t // tm,),
        in_specs=[tok(d), tok(S5_WIDTH), tok(BRANCH_WIDTH), tok(BRANCH_WIDTH), tok(d),
                  pl.BlockSpec((1, 1, 6, d), lambda i: (layer, i // tpb, 0, 0))]
                 + [_layer_spec(a, layer) for a in consts],
        out_specs=tok(d),
        out_shape=jax.ShapeDtypeStruct((t, d), F32),
        compiler_params=_params(("arbitrary",)),
        name="merge",
    )(h, yssm, ymla, ysgu, x2, ada_all, *consts)


def _ffn_kernel(x_ref, ada_ref, wa_ref, wb_ref, wo_ref, lng_ref, lnb_ref, o_ref, *, alpha):
    tm = x_ref.shape[0]
    sub = FFN_SUB
    ada = ada_ref[0, 0]

    def hidden(r):
        rows = slice(r * sub, (r + 1) * sub)
        h = (x_ref[rows, :] * (1.0 + ada[4:5]) + ada[3:4]).astype(BF16)
        a = jnp.dot(h, wa_ref[0], preferred_element_type=F32)
        b = jnp.dot(h, wb_ref[0], preferred_element_type=F32)
        return (a * _sigmoid(a) * b).astype(BF16)

    def project(r, act):
        rows = slice(r * sub, (r + 1) * sub)
        f = jnp.dot(act, wo_ref[0], preferred_element_type=F32)
        o_ref[rows, :] = _layer_norm(alpha * x_ref[rows, :] + (1.0 + ada[5:6]) * f, lng_ref[0], lnb_ref[0])

    nsub = tm // sub
    act = hidden(0)
    for r in range(nsub):
        nxt = hidden(r + 1) if r + 1 < nsub else None
        project(r, act)
        act = nxt


def _ffn_call(x2, ada_all, layer, w_in, w_out, lng, lnb, seq, tm, alpha):
    t, d = x2.shape
    ff = w_out.shape[1]
    tpb = seq // tm
    tok = pl.BlockSpec((tm, d), lambda i: (i, 0))
    return pl.pallas_call(
        functools.partial(_ffn_kernel, alpha=alpha),
        grid=(t // tm,),
        in_specs=[tok, pl.BlockSpec((1, 1, 6, d), lambda i: (layer, i // tpb, 0, 0)),
                  _layer_spec(w_in, layer, (ff, 0)), _layer_spec(w_in, layer, (ff, 1)), _layer_spec(w_out, layer),
                  _layer_spec(lng, layer), _layer_spec(lnb, layer)],
        out_specs=tok,
        out_shape=jax.ShapeDtypeStruct((t, d), F32),
        compiler_params=_params(("arbitrary",)),
        name="ffn",
    )(x2, ada_all, w_in, w_in, w_out, lng, lnb)


def _s5_prep_inputs(lam_re, lam_im, log_dt, b_re, b_im, c_re, c_im, dskip):
    depth, g, p = lam_re.shape
    n = depth * g
    dup = lambda a: jnp.concatenate([a, a], axis=-1).reshape(n, 1, 2 * p)
    lr2 = dup(lam_re)
    li2 = dup(lam_im)
    ldt2 = jnp.broadcast_to(log_dt[..., None], (depth, g, 2 * p)).reshape(n, 1, 2 * p)
    btr = b_re.transpose(0, 1, 3, 2).reshape(n, S5_GROUP, p)
    bti = b_im.transpose(0, 1, 3, 2).reshape(n, S5_GROUP, p)
    bt = jnp.concatenate([btr, bti], axis=-1)
    btsw = jnp.concatenate([bti, btr], axis=-1)
    cr = c_re.reshape(n, S5_GROUP, p)
    ci = c_im.reshape(n, S5_GROUP, p)
    cw1 = jnp.concatenate([cr, -ci], axis=-1)
    cw2 = jnp.concatenate([-ci, -cr], axis=-1)
    dsk = jnp.broadcast_to(dskip.reshape(n, S5_GROUP, 1), (n, S5_GROUP, LANES))
    return lr2, li2, ldt2, bt, btsw, cw1, cw2, dsk


def _s5_scan_multipliers(lam_re, lam_im, log_dt, nch):
    depth, g, p = lam_re.shape
    dt = jnp.exp(log_dt)[..., None, None]
    nsteps = max(1, (nch - 1).bit_length())
    assert nsteps <= S5_SCAN_COLS
    exps = np.zeros(S5_SCAN_COLS)
    exps[:nsteps] = CHUNK * (2.0 ** np.arange(nsteps))
    k = jnp.asarray(exps, F32)
    mag = jnp.exp(lam_re[..., None] * dt * k)
    ang = lam_im[..., None] * dt * k
    sr = (mag * jnp.cos(ang)).reshape(depth * g, p, S5_SCAN_COLS)
    si = (mag * jnp.sin(ang)).reshape(depth * g, p, S5_SCAN_COLS)
    return jnp.concatenate([sr, sr], axis=1), jnp.concatenate([-si, si], axis=1)


def _rope_tables(seq, scale):
    inv_freq = 1.0 / (ROPE_THETA ** (jnp.arange(0, QK_ROPE, 2, dtype=F32) / QK_ROPE))
    ang = jnp.arange(seq, dtype=F32)[:, None] * inv_freq[None, :]
    cos = jnp.cos(ang)
    sin = jnp.sin(ang)
    zk = jnp.zeros((seq, LANES - QK_ROPE), F32)
    cosk = jnp.concatenate([cos, cos, zk], axis=1)
    sink = jnp.concatenate([-sin, sin, zk], axis=1)
    zq = jnp.zeros((seq, HEAD_PAD - QK_NOPE - QK_ROPE), F32)
    aq = jnp.concatenate([jnp.full((seq, QK_NOPE), scale, F32), scale * cos, scale * cos, zq], axis=1)
    sq = jnp.concatenate([jnp.zeros((seq, QK_NOPE), F32), -scale * sin, scale * sin, zq], axis=1)
    return cosk, sink, aq.T, sq.T


def _swap_halves(w):
    half = w.shape[-1] // 2
    return jnp.concatenate([w[..., half:], w[..., :half]], axis=-1)


def kernel(x, c, w_ada, b_ada, w_in, b_in, s5_lambda_re, s5_lambda_im, s5_log_dt, s5_b_re, s5_b_im, s5_c_re, s5_c_im, s5_d, s5_w_glu, s5_b_glu, mla_q_norm, mla_w_q_up, mla_kv_norm, mla_w_kv_up, sgu_ln_g, sgu_ln_b, sgu_w_s, sgu_b_s, w_branch, w_out, ln1_g, ln1_b, ffn_w_in, ffn_w_out, ln2_g, ln2_b):
    batch, seq, d = x.shape
    depth = w_ada.shape[0]
    t = batch * seq
    alpha = float((2 * depth) ** 0.25)
    tm = 512
    tq = 512
    tm_wide = 1024
    assert seq % tm_wide == 0
    nch = seq // CHUNK
    assert seq % FLASH_TK == 0 and FLASH_TK % tm == 0 and tm == tq and tm % SGU_CHUNK == 0
    assert (batch * nch) % LANES == 0

    ada_all = _ada_call(c, w_ada, b_ada).reshape(depth, batch, 6, d)

    scale = LOG2E * (QK_NOPE + QK_ROPE) ** -0.5
    cosk, sink, aqt, sqt = _rope_tables(seq, scale)

    offs = np.cumsum([0, S5_WIDTH, Q_LORA, KV_LORA, QK_ROPE, SGU_WIDTH, SGU_WIDTH])
    o_u5, o_cq, o_ckv, o_kpe, o_usgu, o_vsgu, o_gate = [int(o) for o in offs]
    row = lambda a: a[:, None, :]

    kpe_b = b_in[:, o_kpe:o_kpe + QK_ROPE]
    w_a, w_g = _win_prep_call(w_in, o_kpe, o_usgu, o_gate)
    b_a = row(jnp.concatenate([b_in[:, o_u5:o_kpe], b_in[:, o_usgu:o_gate], kpe_b, _swap_halves(kpe_b),
                               jnp.zeros((depth, LANES - 2 * QK_ROPE), F32)], axis=1))
    b_g = row(b_in[:, o_gate:])
    pos = np.arange(SGU_CHUNK) // CHUNK
    sgu_mask = jnp.asarray(pos[None, :] <= pos[:, None])
    ws_m = jnp.where(sgu_mask[None, None], sgu_w_s, 0.0).astype(BF16)
    bs_full = jnp.repeat(sgu_b_s.transpose(0, 2, 1), SGU_WIDTH // SGU_GROUPS, axis=2)

    hw = MLA_HEADS * HEAD_PAD
    vw = MLA_HEADS * V_PAD
    wq4 = mla_w_q_up.reshape(depth, Q_LORA, MLA_HEADS, QK_NOPE + QK_ROPE)
    wqt = (jnp.concatenate([wq4, _swap_halves(wq4[..., QK_NOPE:])], axis=3)
           .reshape(depth, Q_LORA, hw).transpose(0, 2, 1).astype(BF16))
    wkv4 = mla_w_kv_up.reshape(depth, KV_LORA, MLA_HEADS, QK_NOPE + V_HEAD)
    wk = jnp.concatenate([wkv4[..., :QK_NOPE], jnp.zeros((depth, KV_LORA, MLA_HEADS, HEAD_PAD - QK_NOPE), F32)],
                         axis=3).reshape(depth, KV_LORA, hw).astype(BF16)
    wvt = (jnp.concatenate([wkv4[..., QK_NOPE:], jnp.zeros((depth, KV_LORA, MLA_HEADS, V_PAD - V_HEAD), F32)],
                           axis=3).reshape(depth, KV_LORA, vw).transpose(0, 2, 1).astype(BF16))
    place = np.zeros((LANES, hw), np.float32)
    for hh in range(MLA_HEADS):
        place[np.arange(QK_ROPE), hh * HEAD_PAD + QK_NOPE + np.arange(QK_ROPE)] = 1.0
    pk = jnp.asarray(place, BF16)
    ones_col = np.zeros((vw, 1), np.float32)
    ones_col[np.arange(MLA_HEADS) * V_PAD + V_HEAD, 0] = 1.0
    ones_col = jnp.asarray(ones_col)

    wglu = s5_w_glu.astype(BF16)
    wbr = w_branch.astype(BF16)
    wout = w_out.astype(BF16)
    ffn_in = ffn_w_in.astype(BF16)
    ffn_out = ffn_w_out.astype(BF16)

    dtb, wst, wot = _s5_prep_call(*_s5_prep_inputs(s5_lambda_re, s5_lambda_im, s5_log_dt, s5_b_re, s5_b_im,
                                                   s5_c_re, s5_c_im, s5_d))
    pa, pb = _s5_scan_multipliers(s5_lambda_re, s5_lambda_im, s5_log_dt, nch)

    x2 = x.reshape(t, d)
    for l in range(depth):
        h, u5, ysgu, kmat, qt, vt = _inproj_call(
            x2, ada_all, l, w_a, b_a, row(mla_q_norm), row(mla_kv_norm), row(sgu_ln_g), row(sgu_ln_b),
            ws_m, bs_full, wqt, wk, wvt, pk, ones_col, cosk, sink, aqt, sqt, seq, tm)
        ymla = _flash_call(kmat, qt, vt, batch, seq, tq, FLASH_TK)
        yssm = _s5_call(u5, dtb, wst, wot, pa, pb, l, nch)
        x2 = _merge_call(h, yssm, ymla, ysgu, x2, ada_all, l, w_g, b_g, wglu, row(s5_b_glu), wbr, wout,
                         row(ln1_g), row(ln1_b), seq, tm_wide, alpha)
        x2 = _ffn_call(x2, ada_all, l, ffn_in, ffn_out, row(ln2_g), row(ln2_b), seq, tm_wide, alpha)
    return x2.reshape(batch, seq, d)
```
